```python
import jax
import jax.numpy as jnp
from jax import lax
import numpy as np

D_MODEL = 2048
BATCH = 2
SEQ = 4096
DEPTH = 4
DEC_BATCH = 8
DEC_SEQ = 8
PAST_LEN = 16384
PAGE_SIZE = 128

HEAD_DIM = 128
N_EVEN = (DEPTH + 1) // 2
N_ODD = DEPTH // 2
NSA_HEADS = 8
NSA_KV_HEADS = 2
NSA_GROUP = NSA_HEADS // NSA_KV_HEADS
NSA_BLOCK = 64
NSA_TOP_N = 16
NSA_WINDOW = 512
NSA_FORCE_SCORE = 1.0e4
FOX_HEADS = 8
FORGET_BIAS_INIT = 3.0
DIL_HEADS = 16
DIL_PAIRS = ((128, 1), (512, 4), (2048, 16))
DIL_MAX_WINDOW = 2048
ROPE_THETA = 10000.0
Q_BLOCK = 128
EPS = 1e-6

NSA_WIDTH = NSA_HEADS * HEAD_DIM
NSA_KV_WIDTH = NSA_KV_HEADS * HEAD_DIM
FOX_WIDTH = FOX_HEADS * HEAD_DIM
DIL_WIDTH = DIL_HEADS * HEAD_DIM
EVEN_SPLITS = (NSA_WIDTH,) + (NSA_KV_WIDTH,) * 6 + (3 * NSA_HEADS, NSA_WIDTH, FOX_WIDTH, FOX_WIDTH, FOX_WIDTH, FOX_HEADS, FOX_WIDTH)
EVEN_IN = sum(EVEN_SPLITS)
EVEN_CUTS = tuple(int(c) for c in np.cumsum(EVEN_SPLITS)[:-1])
ODD_SPLITS = (DIL_WIDTH,) * 4
ODD_IN = sum(ODD_SPLITS)
ODD_CUTS = tuple(int(c) for c in np.cumsum(ODD_SPLITS)[:-1])

kernel_name = "nsa_fox_dilated_hybrid_step"


def rms_norm(x, g):
    xf = x.astype(jnp.float32)
    y = xf * lax.rsqrt(jnp.mean(xf * xf, axis=-1, keepdims=True) + EPS)
    return (y * g.astype(jnp.float32)).astype(x.dtype)


def rope(x, pos):
    half = x.shape[-1] // 2
    inv = ROPE_THETA ** (-jnp.arange(half, dtype=jnp.float32) / half)
    ang = pos.astype(jnp.float32)[:, None] * inv[None, :]
    cos, sin = jnp.cos(ang)[None, :, None, :], jnp.sin(ang)[None, :, None, :]
    xf = x.astype(jnp.float32)
    x1, x2 = xf[..., :half], xf[..., half:]
    return jnp.concatenate([x1 * cos - x2 * sin, x2 * cos + x1 * sin], axis=-1).astype(x.dtype)


def masked_softmax(s, mask):
    s = jnp.where(mask, s.astype(jnp.float32), -jnp.inf)
    m = jnp.max(s, axis=-1, keepdims=True)
    m = jnp.where(jnp.isfinite(m), m, 0.0)
    e = jnp.exp(s - m)
    den = jnp.sum(e, axis=-1, keepdims=True)
    p = e / jnp.maximum(den, 1e-30)
    lse = m[..., 0] + jnp.log(den[..., 0])
    return p, lse


def sweep_query_blocks(f, n_q):
    if n_q <= Q_BLOCK:
        return f(0, n_q)
    n_blk = n_q // Q_BLOCK
    out = lax.map(lambda i: f(i * Q_BLOCK, Q_BLOCK), jnp.arange(n_blk, dtype=jnp.int32))

    def unblock(a):
        a = jnp.moveaxis(a, 0, 1)
        return a.reshape((a.shape[0], n_blk * Q_BLOCK) + a.shape[3:])

    return jax.tree_util.tree_map(unblock, out)


def gather_pages(pool, layer, page_table):
    g = pool[layer, page_table]
    return g.reshape((g.shape[0], g.shape[1] * g.shape[2]) + g.shape[3:])


def nsa_attend(q, pos0, gate_logit, gate_bias, k_cmp, v_cmp, k_sel, v_sel,
               k_win, v_win, win_start, cmp_pos, g_kc):
    B, T = q.shape[:2]
    G, R, Dh, BLK = NSA_KV_HEADS, NSA_GROUP, HEAD_DIM, NSA_BLOCK
    L = k_cmp.shape[1]
    nb = -(-L // BLK)

    def to_blocks(a):
        a = jnp.pad(a, ((0, 0), (0, nb * BLK - L), (0, 0), (0, 0)))
        return a.reshape(B, nb, BLK, G, Dh)

    kc = rms_norm(jnp.mean(to_blocks(k_cmp) + cmp_pos[None, None, :, None, :], axis=2), g_kc)
    vc = jnp.mean(to_blocks(v_cmp), axis=2)
    kc_last = (jnp.arange(nb, dtype=jnp.int32) + 1) * BLK - 1
    ks_blk = jnp.transpose(to_blocks(k_sel), (0, 3, 1, 2, 4))
    vs_blk = jnp.transpose(to_blocks(v_sel), (0, 3, 1, 2, 4))
    kw_pad = jnp.pad(k_win, ((0, 0), (NSA_WINDOW, 0), (0, 0), (0, 0)))
    vw_pad = jnp.pad(v_win, ((0, 0), (NSA_WINDOW, 0), (0, 0), (0, 0)))
    pos = pos0 + jnp.arange(T, dtype=jnp.int32)
    q_n = q.reshape(B, T, G, R, Dh)
    q_r = rope(q, pos).reshape(B, T, G, R, Dh)
    gates = jax.nn.sigmoid(gate_logit.astype(jnp.float32) + gate_bias).reshape(B, T, G, R, 3)
    n_sel = min(NSA_TOP_N, nb)
    blk_ids = jnp.arange(nb, dtype=jnp.int32)
    b_ix = jnp.arange(B)[:, None, None, None]
    g_ix = jnp.arange(G)[None, :, None, None]
    scale = HEAD_DIM ** -0.5

    def block(start, size):
        sl = lambda a: lax.dynamic_slice_in_dim(a, start, size, axis=1)
        qn, qr, gb = sl(q_n), sl(q_r), sl(gates)
        qp = lax.dynamic_slice_in_dim(pos, start, size)
        s = jnp.einsum('bqgrd,bjgd->bgrqj', qn, kc).astype(jnp.float32) * scale
        p_c, _ = masked_softmax(s, kc_last[None, :] <= qp[:, None])
        o_c = jnp.einsum('bgrqj,bjgd->bqgrd', p_c.astype(vc.dtype), vc)
        cur = qp // BLK
        imp = jnp.sum(p_c, axis=2)
        forced = (blk_ids[None, :] == 0) | (blk_ids[None, :] == cur[:, None]) | (blk_ids[None, :] == cur[:, None] - 1)
        imp = jnp.where(forced, NSA_FORCE_SCORE, imp)
        imp = jnp.where(blk_ids[None, :] <= cur[:, None], imp, -jnp.inf)
        _, sel = lax.top_k(imp, n_sel)
        ksel = ks_blk[b_ix, g_ix, sel].reshape(B, G, size, n_sel * BLK, Dh)
        vsel = vs_blk[b_ix, g_ix, sel].reshape(B, G, size, n_sel * BLK, Dh)
        kpos = (sel[..., None] * BLK + jnp.arange(BLK, dtype=jnp.int32)).reshape(B, G, size, n_sel * BLK)
        s = jnp.einsum('bqgrd,bgqkd->bgrqk', qr, ksel).astype(jnp.float32) * scale
        p_s, _ = masked_softmax(s, (kpos <= qp[None, None, :, None])[:, :, None])
        o_s = jnp.einsum('bgrqk,bgqkd->bqgrd', p_s.astype(vsel.dtype), vsel)
        w0 = pos0 + start - win_start
        kwb = lax.dynamic_slice_in_dim(kw_pad, w0, size + NSA_WINDOW, axis=1)
        vwb = lax.dynamic_slice_in_dim(vw_pad, w0, size + NSA_WINDOW, axis=1)
        kwp = pos0 + start - NSA_WINDOW + jnp.arange(size + NSA_WINDOW, dtype=jnp.int32)
        dist = qp[:, None] - kwp[None, :]
        s = jnp.einsum('bqgrd,bkgd->bgrqk', qr, kwb).astype(jnp.float32) * scale
        p_w, _ = masked_softmax(s, (dist >= 0) & (dist < NSA_WINDOW) & (kwp[None, :] >= 0))
        o_w = jnp.einsum('bgrqk,bkgd->bqgrd', p_w.astype(vwb.dtype), vwb)
        o = gb[..., 0:1] * o_c + gb[..., 1:2] * o_s + gb[..., 2:3] * o_w
        return o.astype(q.dtype).reshape(B, size, G * R * Dh)

    return sweep_query_blocks(block, T)


def fox_attend(q, k, v, c_q, c_k, pos0):
    B, T = q.shape[:2]
    L = k.shape[1]
    pos = pos0 + jnp.arange(T, dtype=jnp.int32)
    k_pos = jnp.arange(L, dtype=jnp.int32)
    ck = jnp.transpose(c_k, (0, 2, 1))[:, :, None, :]
    scale = HEAD_DIM ** -0.5

    def block(start, size):
        qb = lax.dynamic_slice_in_dim(q, start, size, axis=1)
        cb = jnp.transpose(lax.dynamic_slice_in_dim(c_q, start, size, axis=1), (0, 2, 1))[..., None]
        qp = lax.dynamic_slice_in_dim(pos, start, size)
        s = jnp.einsum('bqhd,bkhd->bhqk', qb, k).astype(jnp.float32) * scale + (cb - ck)
        p, _ = masked_softmax(s, qp[:, None] >= k_pos[None, :])
        return jnp.einsum('bhqk,bkhd->bqhd', p.astype(v.dtype), v)

    return sweep_query_blocks(block, T)


def dilated_attend(q, k, v, q_idx):
    T = q.shape[1]
    scale = HEAD_DIM ** -0.5
    outs, lses = [], []
    for window, dil in DIL_PAIRS:
        offs = jnp.arange(window // dil + 1, dtype=jnp.int32) * dil

        def block(start, size, offs=offs):
            qb = lax.dynamic_slice_in_dim(q, start, size, axis=1)
            idx = lax.dynamic_slice_in_dim(q_idx, start, size)[:, None] - offs[None, :]
            valid = idx >= 0
            idx = jnp.maximum(idx, 0)
            kb, vb = k[:, idx], v[:, idx]
            s = jnp.einsum('bqhd,bqmhd->bhqm', qb, kb).astype(jnp.float32) * scale
            p, lse = masked_softmax(s, valid)
            o = jnp.einsum('bhqm,bqmhd->bqhd', p.astype(vb.dtype), vb)
            return o, jnp.transpose(lse, (0, 2, 1))

        o, lse = sweep_query_blocks(block, T)
        outs.append(o)
        lses.append(lse)
    w = jax.nn.softmax(jnp.stack(lses, axis=0), axis=0)
    o = jnp.sum(w[..., None] * jnp.stack(outs, axis=0).astype(jnp.float32), axis=0)
    return o.astype(q.dtype)


def even_layer(x, pos0, params, past):
    g_norm, w_in, w_out, gate_b, forget_b, cmp_pos, nsa_g, fox_g = params
    B, T, _ = x.shape
    pos = pos0 + jnp.arange(T, dtype=jnp.int32)
    h = rms_norm(x, g_norm)
    (q_a, kc, vc, ks, vs, kw, vw, gate_a, z_a,
     q_b, k_b, v_b, f_b, z_b) = jnp.split(h @ w_in, EVEN_CUTS, axis=-1)

    def heads(a, n):
        return a.reshape(B, T, n, HEAD_DIM)

    G = NSA_KV_HEADS
    q_a = rms_norm(heads(q_a, NSA_HEADS), nsa_g[0])
    ks = rope(rms_norm(heads(ks, G), nsa_g[2]), pos)
    kw = rope(rms_norm(heads(kw, G), nsa_g[3]), pos)
    rows_nsa = jnp.stack([heads(kc, G), heads(vc, G), ks, heads(vs, G)], axis=2)
    rows_win = jnp.stack([kw, heads(vw, G)], axis=2)
    q_b = rms_norm(heads(q_b, FOX_HEADS), fox_g[0])
    rows_fox = jnp.stack([rms_norm(heads(k_b, FOX_HEADS), fox_g[1]), heads(v_b, FOX_HEADS)], axis=2)
    logf = jax.nn.log_sigmoid(f_b.astype(jnp.float32) + forget_b)
    if past is None:
        all_nsa, all_fox, all_logf, win = rows_nsa, rows_fox, logf, rows_win
        win_start = pos0
        new_win = rows_win[:, T - min(NSA_WINDOW, T):]
    else:
        past_nsa, past_fox, past_logf, win_buf = past
        all_nsa = jnp.concatenate([past_nsa, rows_nsa], axis=1)
        all_fox = jnp.concatenate([past_fox, rows_fox], axis=1)
        all_logf = jnp.concatenate([past_logf.astype(jnp.float32), logf], axis=1)
        lw = win_buf.shape[1]
        win = jnp.concatenate([win_buf, rows_win], axis=1)
        win_start = pos0 - lw
        new_win = win[:, win.shape[1] - lw:]
    o_a = nsa_attend(q_a, pos0, gate_a, gate_b, all_nsa[:, :, 0], all_nsa[:, :, 1],
                     all_nsa[:, :, 2], all_nsa[:, :, 3], win[:, :, 0], win[:, :, 1],
                     win_start, cmp_pos, nsa_g[1])
    c_k = jnp.cumsum(all_logf, axis=1)
    o_b = fox_attend(q_b, all_fox[:, :, 0], all_fox[:, :, 1], c_k[:, c_k.shape[1] - T:], c_k, pos0)
    mix = jnp.concatenate([o_a * jax.nn.silu(z_a),
                           o_b.reshape(B, T, FOX_WIDTH) * jax.nn.silu(z_b)], axis=-1)
    y = x + mix @ w_out
    return y, (rows_nsa, rows_fox, logf, new_win)


def odd_layer(x, pos0, params, past):
    g_norm, w_in, w_out, qk_g = params
    B, T, _ = x.shape
    pos = pos0 + jnp.arange(T, dtype=jnp.int32)
    h = rms_norm(x, g_norm)
    q, k, v, z = jnp.split(h @ w_in, ODD_CUTS, axis=-1)
    q = rope(rms_norm(q.reshape(B, T, DIL_HEADS, HEAD_DIM), qk_g[0]), pos)
    k = rope(rms_norm(k.reshape(B, T, DIL_HEADS, HEAD_DIM), qk_g[1]), pos)
    rows = jnp.stack([k, v.reshape(B, T, DIL_HEADS, HEAD_DIM)], axis=2)
    if past is None:
        all_rows = rows
        q_idx = jnp.arange(T, dtype=jnp.int32)
        new_buf = rows[:, T - min(DIL_MAX_WINDOW, T):]
    else:
        lw = past.shape[1]
        all_rows = jnp.concatenate([past, rows], axis=1)
        q_idx = lw + jnp.arange(T, dtype=jnp.int32)
        new_buf = all_rows[:, T:]
    o = dilated_attend(q, all_rows[:, :, 0], all_rows[:, :, 1], q_idx)
    y = x + (o.reshape(B, T, DIL_WIDTH) * jax.nn.silu(z)) @ w_out
    return y, new_buf


def setup_inputs(seed: int = 0) -> dict:
    key = jax.random.key(seed)
    ks = jax.random.split(key, 20)
    f32 = jnp.float32
    n_pages = PAST_LEN // PAGE_SIZE
    n_used = DEC_BATCH * n_pages
    n_pool = n_used + max(1, n_used // 4)
    lw_a = min(NSA_WINDOW, PAST_LEN)
    lw_c = min(DIL_MAX_WINDOW, PAST_LEN)

    def nrm(k, shape, s=1.0):
        return s * jax.random.normal(k, shape, f32)

    mix_width = NSA_WIDTH + FOX_WIDTH
    return {
        'x_prompt': nrm(ks[0], (BATCH, SEQ, D_MODEL)),
        'x_sample': nrm(ks[1], (DEC_BATCH, DEC_SEQ, D_MODEL)),
        'cache_nsa_kv': nrm(ks[2], (N_EVEN, n_pool, PAGE_SIZE, 4, NSA_KV_HEADS, HEAD_DIM)),
        'cache_fox_kv': nrm(ks[3], (N_EVEN, n_pool, PAGE_SIZE, 2, FOX_HEADS, HEAD_DIM)),
        'cache_fox_logf': jax.nn.log_sigmoid(FORGET_BIAS_INIT + nrm(ks[4], (N_EVEN, n_pool, PAGE_SIZE, FOX_HEADS))),
        'state_nsa_win_kv': nrm(ks[5], (N_EVEN, DEC_BATCH, lw_a, 2, NSA_KV_HEADS, HEAD_DIM)),
        'state_dil_kv': nrm(ks[6], (N_ODD, DEC_BATCH, lw_c, 2, DIL_HEADS, HEAD_DIM)),
        'page_table': jax.random.permutation(ks[7], n_pool)[:n_used].reshape(DEC_BATCH, n_pages).astype(jnp.int32),
        'norm_even': 1.0 + nrm(ks[8], (N_EVEN, D_MODEL), 0.02),
        'w_in_even': nrm(ks[9], (N_EVEN, D_MODEL, EVEN_IN), D_MODEL ** -0.5),
        'w_out_even': nrm(ks[10], (N_EVEN, mix_width, D_MODEL), mix_width ** -0.5),
        'nsa_gate_bias': nrm(ks[11], (N_EVEN, 3 * NSA_HEADS), 0.1),
        'fox_forget_bias': FORGET_BIAS_INIT + nrm(ks[12], (N_EVEN, FOX_HEADS), 0.1),
        'nsa_cmp_pos': nrm(ks[13], (N_EVEN, NSA_BLOCK, HEAD_DIM), 0.1),
        'nsa_qk_gain': 1.0 + nrm(ks[14], (N_EVEN, 4, HEAD_DIM), 0.02),
        'fox_qk_gain': 1.0 + nrm(ks[15], (N_EVEN, 2, HEAD_DIM), 0.02),
        'norm_odd': 1.0 + nrm(ks[16], (N_ODD, D_MODEL), 0.02),
        'w_in_odd': nrm(ks[17], (N_ODD, D_MODEL, ODD_IN), D_MODEL ** -0.5),
        'w_out_odd': nrm(ks[18], (N_ODD, DIL_WIDTH, D_MODEL), DIL_WIDTH ** -0.5),
        'dil_qk_gain': 1.0 + nrm(ks[19], (N_ODD, 2, HEAD_DIM), 0.02),
    }


def reference(x_prompt, x_sample, cache_nsa_kv, cache_fox_kv, cache_fox_logf,
              state_nsa_win_kv, state_dil_kv, page_table, norm_even, w_in_even,
              w_out_even, nsa_gate_bias, fox_forget_bias, nsa_cmp_pos, nsa_qk_gain,
              fox_qk_gain, norm_odd, w_in_odd, w_out_odd, dil_qk_gain):
    y_prompt, y_sample = x_prompt, x_sample
    nsa_p, nsa_s, fox_p, fox_s, logf_p, logf_s = [], [], [], [], [], []
    win_p, win_s, dil_p, dil_s = [], [], [], []
    for layer in range(DEPTH):
        if layer % 2 == 0:
            e = layer // 2
            params = (norm_even[e], w_in_even[e], w_out_even[e], nsa_gate_bias[e],
                      fox_forget_bias[e], nsa_cmp_pos[e], nsa_qk_gain[e], fox_qk_gain[e])
            y_prompt, (a, b, c, d) = even_layer(y_prompt, 0, params, None)
            past = (gather_pages(cache_nsa_kv, e, page_table),
                    gather_pages(cache_fox_kv, e, page_table),
                    gather_pages(cache_fox_logf, e, page_table),
                    state_nsa_win_kv[e])
            y_sample, (a2, b2, c2, d2) = even_layer(y_sample, PAST_LEN, params, past)
            nsa_p.append(a); fox_p.append(b); logf_p.append(c); win_p.append(d)
            nsa_s.append(a2); fox_s.append(b2); logf_s.append(c2); win_s.append(d2)
        else:
            o = layer // 2
            params = (norm_odd[o], w_in_odd[o], w_out_odd[o], dil_qk_gain[o])
            y_prompt, buf_p = odd_layer(y_prompt, 0, params, None)
            y_sample, buf_s = odd_layer(y_sample, PAST_LEN, params, state_dil_kv[o])
            dil_p.append(buf_p); dil_s.append(buf_s)
    nsa_kv_prompt, nsa_kv_sample = jnp.stack(nsa_p), jnp.stack(nsa_s)
    fox_kv_prompt, fox_kv_sample = jnp.stack(fox_p), jnp.stack(fox_s)
    fox_logf_prompt, fox_logf_sample = jnp.stack(logf_p), jnp.stack(logf_s)
    nsa_win_prompt, nsa_win_sample = jnp.stack(win_p), jnp.stack(win_s)
    dil_kv_prompt, dil_kv_sample = jnp.stack(dil_p), jnp.stack(dil_s)
    return (y_prompt, y_sample, nsa_kv_prompt, nsa_kv_sample, fox_kv_prompt, fox_kv_sample,
            fox_logf_prompt, fox_logf_sample, nsa_win_prompt, nsa_win_sample,
            dil_kv_prompt, dil_kv_sample)
```

```python
import functools
import math

import jax
import jax.numpy as jnp
from jax import lax
from jax.experimental import pallas as pl
from jax.experimental.pallas import tpu as pltpu

F32 = jnp.float32
BF16 = jnp.bfloat16

HEAD_DIM = 128
LANES = 128
NSA_HEADS = 8
NSA_KV_HEADS = 2
NSA_GROUP = NSA_HEADS // NSA_KV_HEADS
NSA_BLOCK = 64
NSA_TOP_N = 16
NSA_WINDOW = 512
NSA_FORCE_SCORE = 1.0e4
FOX_HEADS = 8
DIL_HEADS = 16
DIL_PAIRS = ((128, 1), (512, 4), (2048, 16))
ROPE_THETA = 10000.0
EPS = 1e-6
SCALE = HEAD_DIM ** -0.5
NEG = -1.0e30
VMEM_LIMIT = 56 * 1024 * 1024


def _params(*sem):
    return pltpu.CompilerParams(dimension_semantics=sem, vmem_limit_bytes=VMEM_LIMIT)


def _bdot(a, b):
    return jnp.dot(a.astype(BF16), b.astype(BF16), preferred_element_type=F32)


def _bdot_nt(a, b):
    return lax.dot_general(a.astype(BF16), b.astype(BF16), (((1,), (1,)), ((), ())),
                           preferred_element_type=F32)


def _split3(x):
    hi = x.astype(BF16)
    r1 = x - hi.astype(F32)
    mid = r1.astype(BF16)
    lo = (r1 - mid.astype(F32)).astype(BF16)
    return hi, mid, lo


def _exact_dot(x, m):
    hi, mid, lo = _split3(x)
    return (jnp.dot(hi, m, preferred_element_type=F32)
            + jnp.dot(mid, m, preferred_element_type=F32)
            + jnp.dot(lo, m, preferred_element_type=F32))


def _exact_dot_left(m, x):
    hi, mid, lo = _split3(x)
    return (jnp.dot(m, hi, preferred_element_type=F32)
            + jnp.dot(m, mid, preferred_element_type=F32)
            + jnp.dot(m, lo, preferred_element_type=F32))


def _token_of_row(n_rep, tq):
    return jnp.concatenate([lax.broadcasted_iota(jnp.int32, (tq, 1), 0)] * n_rep, axis=0)


BLOCK_SHIFT = 6


def _sigmoid(x):
    return 1.0 / (1.0 + jnp.exp(-x))


def _log_sigmoid(x):
    return -(jnp.maximum(-x, 0.0) + jnp.log(1.0 + jnp.exp(-jnp.abs(x))))


def _head_rms(y, g):
    return y * lax.rsqrt(jnp.mean(y * y, axis=-1, keepdims=True) + EPS) * g


def _softmax_rows(s, mask):
    s = jnp.where(mask, s, -jnp.inf)
    m = jnp.max(s, axis=-1, keepdims=True)
    m = jnp.where(m > -jnp.inf, m, 0.0)
    e = jnp.exp(s - m)
    den = jnp.sum(e, axis=-1, keepdims=True)
    return e / jnp.maximum(den, 1e-30), m, den


def _rmsnorm_kernel(x_ref, g_ref, h_ref):
    x = x_ref[...]
    y = x * lax.rsqrt(jnp.mean(x * x, axis=-1, keepdims=True) + EPS)
    h_ref[...] = (y * g_ref[...]).astype(BF16)


def _rmsnorm(x, g):
    n, d = x.shape
    tm = min(n, 512)
    return pl.pallas_call(
        _rmsnorm_kernel,
        grid=(n // tm,),
        in_specs=[pl.BlockSpec((tm, d), lambda i: (i, 0)),
                  pl.BlockSpec((1, d), lambda i: (0, 0))],
        out_specs=pl.BlockSpec((tm, d), lambda i: (i, 0)),
        out_shape=jax.ShapeDtypeStruct((n, d), BF16),
        compiler_params=_params("parallel"),
        name="rmsnorm",
    )(x, g.reshape(1, d))


def _proj_kernel(h_ref, w_ref, aux_ref, cos_ref, sin_ref, *out_refs, plans, n_tiles):
    acc = jnp.dot(h_ref[...], w_ref[...], preferred_element_type=F32)

    def emit(tile):
        for out_ref, plan in zip(out_refs, plans):
            for c, (src, op, gi) in enumerate(plan[tile]):
                y = acc[:, src * LANES:(src + 1) * LANES]
                if op in ("rms", "rms_rope"):
                    y = _head_rms(y, aux_ref[gi:gi + 1, :])
                if op == "rms_rope":
                    y = y * cos_ref[...] + pltpu.roll(y, HEAD_DIM // 2, 1) * sin_ref[...]
                if op == "logsig":
                    y = _log_sigmoid(y + aux_ref[gi:gi + 1, :])
                out_ref[:, c * LANES:(c + 1) * LANES] = y

    if n_tiles == 1:
        emit(0)
    else:
        for t in range(n_tiles):
            pl.when(pl.program_id(0) == t)(functools.partial(emit, t))


def _proj(h, w, aux, cos2, sin2, plans, tn):
    n, k = h.shape
    n_tiles = w.shape[1] // tn
    tm = min(n, 256)
    widths = [LANES * len(plan[0]) for plan in plans]
    kern = functools.partial(_proj_kernel, plans=plans, n_tiles=n_tiles)
    outs = pl.pallas_call(
        kern,
        grid=(n_tiles, n // tm),
        in_specs=[pl.BlockSpec((tm, k), lambda j, i: (i, 0)),
                  pl.BlockSpec((k, tn), lambda j, i: (0, j)),
                  pl.BlockSpec(aux.shape, lambda j, i: (0, 0)),
                  pl.BlockSpec((tm, LANES), lambda j, i: (i, 0)),
                  pl.BlockSpec((tm, LANES), lambda j, i: (i, 0))],
        out_specs=[pl.BlockSpec((tm, wd), lambda j, i: (i, j)) for wd in widths],
        out_shape=[jax.ShapeDtypeStruct((n, n_tiles * wd), F32) for wd in widths],
        compiler_params=_params("arbitrary", "arbitrary"),
        name="proj",
    )(h, w, aux, cos2, sin2)
    return outs


def _outproj_kernel(*refs, n_parts):
    x_ref = refs[0]
    y_ref = refs[-1]
    acc = x_ref[...]
    for p in range(n_parts):
        o = refs[1 + p][...]
        z = refs[1 + n_parts + p][...]
        mix = o * (z * _sigmoid(z))
        acc = acc + jnp.dot(mix.astype(BF16), refs[1 + 2 * n_parts + p][...],
                            preferred_element_type=F32)
    y_ref[...] = acc


def _outproj(x, o_parts, z_parts, w_parts):
    n, d = x.shape
    tm = min(n, 256)
    n_parts = len(o_parts)
    row = lambda a: pl.BlockSpec((tm, a.shape[1]), lambda i: (i, 0))
    return pl.pallas_call(
        functools.partial(_outproj_kernel, n_parts=n_parts),
        grid=(n // tm,),
        in_specs=([row(x)] + [row(o) for o in o_parts] + [row(z) for z in z_parts]
                  + [pl.BlockSpec(w.shape, lambda i: (0, 0)) for w in w_parts]),
        out_specs=row(x),
        out_shape=jax.ShapeDtypeStruct((n, d), F32),
        compiler_params=_params("parallel"),
        name="outproj",
    )(x, *o_parts, *z_parts, *w_parts)


CUM_BLOCK = 256


def _cumsum_kernel(x_ref, tri_ref, c_ref, carry_ref):
    @pl.when(pl.program_id(1) == 0)
    def _():
        carry_ref[...] = jnp.zeros_like(carry_ref)

    c = _exact_dot_left(tri_ref[...], x_ref[...]) + carry_ref[...]
    c_ref[...] = c
    carry_ref[...] = c[CUM_BLOCK - 1:CUM_BLOCK, :]


def _cumsum_rows(x, batch):
    n, w = x.shape
    t = n // batch
    nb = t // CUM_BLOCK
    r = lax.broadcasted_iota(jnp.int32, (CUM_BLOCK, CUM_BLOCK), 0)
    c = lax.broadcasted_iota(jnp.int32, (CUM_BLOCK, CUM_BLOCK), 1)
    tri = (c <= r).astype(BF16)
    return pl.pallas_call(
        _cumsum_kernel,
        grid=(batch, nb),
        in_specs=[pl.BlockSpec((CUM_BLOCK, w), lambda b, j: (b * nb + j, 0)),
                  pl.BlockSpec((CUM_BLOCK, CUM_BLOCK), lambda b, j: (0, 0))],
        out_specs=pl.BlockSpec((CUM_BLOCK, w), lambda b, j: (b * nb + j, 0)),
        out_shape=jax.ShapeDtypeStruct((n, w), F32),
        scratch_shapes=[pltpu.VMEM((1, w), F32)],
        compiler_params=_params("parallel", "arbitrary"),
        name="logf_cumsum",
    )(x, tri)


FOX_TILE = 512


def _fox_prompt_kernel(q_ref, k_ref, v_ref, cq_ref, ck_ref, o_ref, m_ref, l_ref, acc_ref):
    i = pl.program_id(1)
    j = pl.program_id(2)
    tq = q_ref.shape[0]
    tk = k_ref.shape[0]

    @pl.when(j == 0)
    def _():
        m_ref[...] = jnp.full_like(m_ref, NEG)
        l_ref[...] = jnp.zeros_like(l_ref)
        acc_ref[...] = jnp.zeros_like(acc_ref)

    @pl.when(j <= i)
    def _():
        qpos = i * tq + lax.broadcasted_iota(jnp.int32, (tq, tk), 0)
        kpos = j * tk + lax.broadcasted_iota(jnp.int32, (tq, tk), 1)
        mask = qpos >= kpos
        for h in range(FOX_HEADS):
            sl = slice(h * HEAD_DIM, (h + 1) * HEAD_DIM)
            s = _bdot_nt(q_ref[:, sl], k_ref[:, sl]) * SCALE
            s = s + (cq_ref[:, h:h + 1] - ck_ref[0, h:h + 1, :])
            s = jnp.where(mask, s, NEG)
            m_old = m_ref[h]
            m_new = jnp.maximum(m_old, jnp.max(s, axis=-1, keepdims=True))
            p = jnp.where(mask, jnp.exp(s - m_new), 0.0)
            alpha = jnp.exp(m_old - m_new)
            l_ref[h] = alpha * l_ref[h] + jnp.sum(p, axis=-1, keepdims=True)
            acc_ref[:, sl] = alpha * acc_ref[:, sl] + _bdot(p, v_ref[:, sl])
            m_ref[h] = m_new

    @pl.when(j == i)
    def _():
        for h in range(FOX_HEADS):
            sl = slice(h * HEAD_DIM, (h + 1) * HEAD_DIM)
            o_ref[:, sl] = acc_ref[:, sl] / jnp.maximum(l_ref[h], 1e-30)


def _fox_prompt(qb, rows_fox, c_all, ck_t, batch):
    n, w = qb.shape
    t = n // batch
    tq = min(FOX_TILE, t)
    nq = t // tq
    return pl.pallas_call(
        _fox_prompt_kernel,
        grid=(batch, nq, nq),
        in_specs=[pl.BlockSpec((tq, w), lambda b, i, j: (b * nq + i, 0)),
                  pl.BlockSpec((tq, w), lambda b, i, j: (b * nq + jnp.minimum(j, i), 0)),
                  pl.BlockSpec((tq, w), lambda b, i, j: (b * nq + jnp.minimum(j, i), 1)),
                  pl.BlockSpec((tq, LANES), lambda b, i, j: (b * nq + i, 0)),
                  pl.BlockSpec((1, FOX_HEADS, tq), lambda b, i, j: (b, 0, jnp.minimum(j, i)))],
        out_specs=pl.BlockSpec((tq, w), lambda b, i, j: (b * nq + i, 0)),
        out_shape=jax.ShapeDtypeStruct((n, w), F32),
        scratch_shapes=[pltpu.VMEM((FOX_HEADS, tq, 1), F32),
                        pltpu.VMEM((FOX_HEADS, tq, 1), F32),
                        pltpu.VMEM((tq, w), F32)],
        compiler_params=_params("parallel", "parallel", "arbitrary"),
        name="fox_prompt",
    )(qb, rows_fox, rows_fox, c_all, ck_t)


def _stack_heads(x, n):
    return jnp.concatenate([x[:, r * HEAD_DIM:(r + 1) * HEAD_DIM] for r in range(n)], axis=0)


def _top_n_mask(imp, blk, n_sel):
    work = imp
    selm = jnp.zeros(imp.shape, F32)
    for _ in range(n_sel):
        m = jnp.max(work, axis=-1, keepdims=True)
        idx = jnp.min(jnp.where(work == m, blk, 1.0e9), axis=-1, keepdims=True)
        pick = blk == idx
        selm = jnp.where(pick, 1.0, selm)
        work = jnp.where(pick, -jnp.inf, work)
    return selm


def _importance(p_c, rows, qpos_col, blk):
    imp = p_c[0:rows]
    for r in range(1, NSA_GROUP):
        imp = imp + p_c[r * rows:(r + 1) * rows]
    cur = (qpos_col >> BLOCK_SHIFT).astype(F32)
    forced = (blk == 0.0) | (blk == cur) | (blk == cur - 1.0)
    imp = jnp.where(forced, NSA_FORCE_SCORE, imp)
    return jnp.where(blk <= cur, imp, NEG)


NSA_Q = 128
NSA_TK = 512
NSA_WKEYS = NSA_WINDOW + NSA_Q


def _nsa_prompt_kernel(qn_ref, qr_ref, kc_ref, vc_ref, ks_ref, vs_ref, kw_ref, vw_ref,
                       gate_ref, gbias_ref, cpos_ref, gkc_ref, e_ref, o_ref,
                       kcn_ref, vcm_ref):
    i = pl.program_id(2)
    t = kc_ref.shape[0]
    nb = t // NSA_BLOCK
    nbp = kcn_ref.shape[0]

    @pl.when(i == 0)
    def _():
        kcn_ref[...] = jnp.zeros_like(kcn_ref)
        vcm_ref[...] = jnp.zeros_like(vcm_ref)

        def body(jb, carry):
            rows = pl.ds(pl.multiple_of(jb * NSA_BLOCK, NSA_BLOCK), NSA_BLOCK)
            km = jnp.mean(kc_ref[rows, :] + cpos_ref[...], axis=0, keepdims=True)
            kcn_ref[pl.ds(jb, 1), :] = _head_rms(km, gkc_ref[...])
            vcm_ref[pl.ds(jb, 1), :] = jnp.mean(vc_ref[rows, :], axis=0, keepdims=True)
            return carry

        lax.fori_loop(0, nb, body, 0)

    rq = NSA_GROUP * NSA_Q
    qn = _stack_heads(qn_ref[...], NSA_GROUP)
    qr = _stack_heads(qr_ref[...], NSA_GROUP).astype(BF16)
    qpos1 = i * NSA_Q + lax.broadcasted_iota(jnp.int32, (NSA_Q, 1), 0)
    qpos = jnp.concatenate([qpos1] * NSA_GROUP, axis=0)

    blk_i = lax.broadcasted_iota(jnp.int32, (1, nbp), 1)
    s = _bdot_nt(qn, kcn_ref[...]) * SCALE
    p_c, _, _ = _softmax_rows(s, (blk_i + 1) * NSA_BLOCK - 1 <= qpos)
    o_c = _bdot(p_c, vcm_ref[...])

    blk = blk_i.astype(F32)
    imp = _importance(p_c, NSA_Q, qpos1, blk)
    selm = _top_n_mask(imp, blk, min(NSA_TOP_N, nb)).astype(BF16)

    def sel_body(c, carry):
        m_old, l_old, acc = carry
        rows = pl.ds(pl.multiple_of(c * NSA_TK, NSA_TK), NSA_TK)
        kmask = jnp.dot(selm, e_ref[c], preferred_element_type=F32)
        kmask = jnp.concatenate([kmask] * NSA_GROUP, axis=0)
        kpos = c * NSA_TK + lax.broadcasted_iota(jnp.int32, (1, NSA_TK), 1)
        mask = (kmask > 0.5) & (kpos <= qpos)
        sc = _bdot_nt(qr, ks_ref[rows, :]) * SCALE
        sc = jnp.where(mask, sc, NEG)
        m_new = jnp.maximum(m_old, jnp.max(sc, axis=-1, keepdims=True))
        p = jnp.where(mask, jnp.exp(sc - m_new), 0.0)
        alpha = jnp.exp(m_old - m_new)
        l_new = alpha * l_old + jnp.sum(p, axis=-1, keepdims=True)
        acc = alpha * acc + _bdot(p, vs_ref[rows, :])
        return m_new, l_new, acc

    n_chunks = (i * NSA_Q + NSA_Q + NSA_TK - 1) // NSA_TK
    init = (jnp.full((rq, 1), NEG, F32), jnp.zeros((rq, 1), F32), jnp.zeros((rq, HEAD_DIM), F32))
    _, l_s, acc_s = lax.fori_loop(0, n_chunks, sel_body, init)
    o_s = acc_s / jnp.maximum(l_s, 1e-30)

    k0 = jnp.clip(i * NSA_Q - NSA_WINDOW, 0, t - NSA_WKEYS)
    k0 = pl.multiple_of(k0, NSA_Q)
    wrows = pl.ds(k0, NSA_WKEYS)
    kpos = k0 + lax.broadcasted_iota(jnp.int32, (1, NSA_WKEYS), 1)
    dist = qpos - kpos
    sw = _bdot_nt(qr, kw_ref[wrows, :]) * SCALE
    p_w, _, _ = _softmax_rows(sw, (dist >= 0) & (dist < NSA_WINDOW))
    o_w = _bdot(p_w, vw_ref[wrows, :])

    gates = _sigmoid(gate_ref[...] + gbias_ref[...])
    for r in range(NSA_GROUP):
        rs = slice(r * NSA_Q, (r + 1) * NSA_Q)
        o = (gates[:, 3 * r:3 * r + 1] * o_c[rs] + gates[:, 3 * r + 1:3 * r + 2] * o_s[rs]
             + gates[:, 3 * r + 2:3 * r + 3] * o_w[rs])
        o_ref[:, r * HEAD_DIM:(r + 1) * HEAD_DIM] = o


def _expand_matrix(nbp, n_keys, tk):
    kb = (jnp.arange(n_keys, dtype=jnp.int32) // NSA_BLOCK).reshape(n_keys // tk, 1, tk)
    jb = jnp.arange(nbp, dtype=jnp.int32).reshape(1, nbp, 1)
    return (kb == jb).astype(BF16)


def _nsa_prompt(qn, qr, rows_nsa, rows_win, gates, gbias, cmp_pos, g_kc, batch):
    n = qn.shape[0]
    t = n // batch
    assert t % NSA_TK == 0 and t >= NSA_WKEYS
    nqb = t // NSA_Q
    nb = t // NSA_BLOCK
    nbp = -(-nb // LANES) * LANES
    e = _expand_matrix(nbp, t, NSA_TK)
    gw = NSA_GROUP * HEAD_DIM
    qspec = pl.BlockSpec((NSA_Q, gw), lambda b, g, i: (b * nqb + i, g))
    kv = lambda typ: pl.BlockSpec((t, HEAD_DIM), lambda b, g, i: (b, typ * NSA_KV_HEADS + g))
    return pl.pallas_call(
        _nsa_prompt_kernel,
        grid=(batch, NSA_KV_HEADS, nqb),
        in_specs=[qspec, qspec, kv(0), kv(1), kv(2), kv(3), kv(0), kv(1),
                  pl.BlockSpec((NSA_Q, LANES), lambda b, g, i: (b * nqb + i, g)),
                  pl.BlockSpec((1, LANES), lambda b, g, i: (0, g)),
                  pl.BlockSpec(cmp_pos.shape, lambda b, g, i: (0, 0)),
                  pl.BlockSpec((1, HEAD_DIM), lambda b, g, i: (0, 0)),
                  pl.BlockSpec(e.shape, lambda b, g, i: (0, 0, 0))],
        out_specs=qspec,
        out_shape=jax.ShapeDtypeStruct((n, NSA_HEADS * HEAD_DIM), F32),
        scratch_shapes=[pltpu.VMEM((nbp, HEAD_DIM), F32), pltpu.VMEM((nbp, HEAD_DIM), F32)],
        compiler_params=_params("parallel", "parallel", "arbitrary"),
        name="nsa_prompt",
    )(qn, qr, rows_nsa, rows_nsa, rows_nsa, rows_nsa, rows_win, rows_win,
      gates, gbias, cmp_pos, g_kc, e)


DIL_Q = 128
DIL_K = 2 * DIL_Q


def _dil_prompt_kernel(q_ref, k_ref, v_ref, o_ref, og_ref, lse_ref):
    t = q_ref.shape[0]
    for gi, (window, dil) in enumerate(DIL_PAIRS):
        n_str = t // dil
        n_qb = n_str // DIL_Q
        span = window // dil

        def body(it, carry, gi=gi, dil=dil, n_qb=n_qb, span=span, n_str=n_str):
            r = it // n_qb
            ib = it % n_qb
            q0 = ib * DIL_Q
            k0 = jnp.clip(q0 - DIL_Q, 0, n_str - DIL_K)
            qrows = pl.ds(r + q0 * dil, DIL_Q, stride=dil) if dil > 1 else pl.ds(q0, DIL_Q)
            krows = pl.ds(r + k0 * dil, DIL_K, stride=dil) if dil > 1 else pl.ds(k0, DIL_K)
            s = _bdot_nt(q_ref[qrows, :], k_ref[krows, :]) * SCALE
            qi = q0 + lax.broadcasted_iota(jnp.int32, (DIL_Q, 1), 0)
            kj = k0 + lax.broadcasted_iota(jnp.int32, (1, DIL_K), 1)
            d = qi - kj
            p, m, den = _softmax_rows(s, (d >= 0) & (d <= span))
            og_ref[gi, qrows, :] = _bdot(p, v_ref[krows, :])
            lse_ref[gi, qrows, :] = jnp.broadcast_to(m + jnp.log(den), (DIL_Q, LANES))
            return carry

        lax.fori_loop(0, dil * n_qb, body, 0)

    lse = lse_ref[...]
    mx = jnp.max(lse, axis=0, keepdims=True)
    w = jnp.exp(lse - mx)
    w = w / jnp.sum(w, axis=0, keepdims=True)
    o_ref[...] = jnp.sum(w * og_ref[...], axis=0)


def _dil_prompt(q, rows, batch):
    n, w = q.shape
    t = n // batch
    assert t % (DIL_K * DIL_PAIRS[-1][1]) == 0
    blk = lambda off: pl.BlockSpec((t, HEAD_DIM), lambda b, h: (b, h + off))
    return pl.pallas_call(
        _dil_prompt_kernel,
        grid=(batch, DIL_HEADS),
        in_specs=[blk(0), blk(0), blk(DIL_HEADS)],
        out_specs=blk(0),
        out_shape=jax.ShapeDtypeStruct((n, w), F32),
        scratch_shapes=[pltpu.VMEM((len(DIL_PAIRS), t, HEAD_DIM), F32),
                        pltpu.VMEM((len(DIL_PAIRS), t, LANES), F32)],
        compiler_params=_params("parallel", "parallel"),
        name="dil_prompt",
    )(q, rows, rows)


def _page_means_kernel(pt_ref, page_ref, cpos_ref, o_ref):
    del pt_ref
    half = NSA_KV_HEADS * HEAD_DIM
    cpos = jnp.concatenate([cpos_ref[...]] * NSA_KV_HEADS + [jnp.zeros((NSA_BLOCK, half), F32)],
                           axis=1)
    for j in range(page_ref.shape[2] // NSA_BLOCK):
        x = page_ref[0, 0, j * NSA_BLOCK:(j + 1) * NSA_BLOCK, :] + cpos
        o_ref[0, 0, j:j + 1, :] = jnp.mean(x, axis=0, keepdims=True)


def _page_means(cache, layer, page_table, cmp_pos):
    b, n_pages = page_table.shape
    page = cache.shape[2]
    w = 2 * NSA_KV_HEADS * HEAD_DIM
    grid_spec = pltpu.PrefetchScalarGridSpec(
        num_scalar_prefetch=1,
        grid=(b, n_pages),
        in_specs=[pl.BlockSpec((1, 1, page, w), lambda bi, p, pt: (layer, pt[bi, p], 0, 0)),
                  pl.BlockSpec(cmp_pos.shape, lambda bi, p, pt: (0, 0))],
        out_specs=pl.BlockSpec((1, 1, page // NSA_BLOCK, w), lambda bi, p, pt: (bi, p, 0, 0)),
    )
    return pl.pallas_call(
        _page_means_kernel,
        grid_spec=grid_spec,
        out_shape=jax.ShapeDtypeStruct((b, n_pages, page // NSA_BLOCK, w), F32),
        compiler_params=_params("parallel", "arbitrary"),
        name="page_means",
    )(page_table, cache, cmp_pos)


def _nsa_cmp_sample_kernel(qn_ref, means_ref, gkc_ref, oc_ref, sel_ref, *, pos0):
    tq = qn_ref.shape[0]
    nbp = means_ref.shape[1]
    qpos1 = pos0 + lax.broadcasted_iota(jnp.int32, (tq, 1), 0)
    qpos = jnp.concatenate([qpos1] * NSA_GROUP, axis=0)
    blk_i = lax.broadcasted_iota(jnp.int32, (1, nbp), 1)
    blk = blk_i.astype(F32)
    gw = NSA_GROUP * HEAD_DIM
    half = NSA_KV_HEADS * HEAD_DIM
    for g in range(NSA_KV_HEADS):
        qn = _stack_heads(qn_ref[:, g * gw:(g + 1) * gw], NSA_GROUP)
        kc = _head_rms(means_ref[0, :, g * HEAD_DIM:(g + 1) * HEAD_DIM], gkc_ref[...])
        vc = means_ref[0, :, half + g * HEAD_DIM:half + (g + 1) * HEAD_DIM]
        s = _bdot_nt(qn, kc) * SCALE
        p_c, _, _ = _softmax_rows(s, (blk_i + 1) * NSA_BLOCK - 1 <= qpos)
        o_c = _bdot(p_c, vc)
        for r in range(NSA_GROUP):
            oc_ref[:, g * gw + r * HEAD_DIM:g * gw + (r + 1) * HEAD_DIM] = o_c[r * tq:(r + 1) * tq]
        imp = _importance(p_c, tq, qpos1, blk)
        sel_ref[0, g * tq:(g + 1) * tq, :] = _top_n_mask(imp, blk, NSA_TOP_N)


def _nsa_cmp_sample(qn, means, g_kc, batch, pos0):
    n, w = qn.shape
    tq = n // batch
    nbp = means.shape[1]
    return pl.pallas_call(
        functools.partial(_nsa_cmp_sample_kernel, pos0=pos0),
        grid=(batch,),
        in_specs=[pl.BlockSpec((tq, w), lambda b: (b, 0)),
                  pl.BlockSpec((1, nbp, means.shape[2]), lambda b: (b, 0, 0)),
                  pl.BlockSpec((1, HEAD_DIM), lambda b: (0, 0))],
        out_specs=[pl.BlockSpec((tq, w), lambda b: (b, 0)),
                   pl.BlockSpec((1, NSA_KV_HEADS * tq, nbp), lambda b: (b, 0, 0))],
        out_shape=[jax.ShapeDtypeStruct((n, w), F32),
                   jax.ShapeDtypeStruct((batch, NSA_KV_HEADS * tq, nbp), F32)],
        compiler_params=_params("parallel"),
        name="nsa_cmp_sample",
    )(qn, means, g_kc)


def _block_diag_queries(q, n_kv, reps):
    parts = []
    for g in range(n_kv):
        for r in range(reps):
            hq = q[:, (g * reps + r) * HEAD_DIM:(g * reps + r + 1) * HEAD_DIM]
            parts.append(jnp.concatenate(
                [hq if gg == g else jnp.zeros_like(hq) for gg in range(n_kv)], axis=1))
    return jnp.concatenate(parts, axis=0).astype(BF16)


def _rep_rows(x, n_kv, per):
    return jnp.concatenate(
        [jnp.broadcast_to(x[g:g + 1, :], (per, x.shape[1])) for g in range(n_kv)], axis=0)


def _online_update(s, mask, v, m_ref, l_ref, acc_ref):
    if mask is not None:
        s = jnp.where(mask, s, NEG)
    m_old = m_ref[...]
    m_new = jnp.maximum(m_old, jnp.max(s, axis=-1, keepdims=True))
    pr = jnp.exp(s - m_new)
    if mask is not None:
        pr = jnp.where(mask, pr, 0.0)
    alpha = jnp.exp(m_old - m_new)
    l_ref[...] = alpha * l_ref[...] + jnp.sum(pr, axis=-1, keepdims=True)
    acc_ref[...] = alpha * acc_ref[...] + _bdot(pr, v)
    m_ref[...] = m_new


def _pad_rows(x, rows):
    return jnp.concatenate([x, jnp.zeros((rows - x.shape[0], x.shape[1]), x.dtype)], axis=0)


def _write_diag(o_ref, l_ref, acc_ref, n_kv, reps, tq):
    for g in range(n_kv):
        for r in range(reps):
            rs = slice((g * reps + r) * tq, (g * reps + r + 1) * tq)
            o = acc_ref[rs, g * HEAD_DIM:(g + 1) * HEAD_DIM] / jnp.maximum(l_ref[rs, :], 1e-30)
            o_ref[:, (g * reps + r) * HEAD_DIM:(g * reps + r + 1) * HEAD_DIM] = o


def _fox_sample_kernel(pt_ref, q_ref, page_ref, lf_ref, new_ref, lfn_ref, o_ref,
                       qbd_ref, cq_ref, carry_ref, m_ref, l_ref, acc_ref):
    del pt_ref
    p = pl.program_id(1)
    tq = q_ref.shape[0]
    page = page_ref.shape[2]
    kw = FOX_HEADS * HEAD_DIM
    lane = lax.broadcasted_iota(jnp.int32, (1, page), 1)
    pr_i = lax.broadcasted_iota(jnp.int32, (page, page), 0)
    pc_i = lax.broadcasted_iota(jnp.int32, (page, page), 1)

    @pl.when(p == 0)
    def _():
        m_ref[...] = jnp.full_like(m_ref, NEG)
        l_ref[...] = jnp.zeros_like(l_ref)
        acc_ref[...] = jnp.zeros_like(acc_ref)
        carry_ref[...] = jnp.zeros_like(carry_ref)
        qbd = _block_diag_queries(q_ref[...], FOX_HEADS, 1)
        qbd_ref[...] = qbd
        cnew = _exact_dot(lfn_ref[0], (pr_i <= pc_i).astype(BF16))
        cnew_rows = _rep_rows(cnew, FOX_HEADS, tq)
        row_t = _token_of_row(FOX_HEADS, tq)
        cq = jnp.sum(jnp.where(lane == row_t, cnew_rows, 0.0), axis=-1, keepdims=True)
        cq_ref[...] = cq
        kn = _pad_rows(new_ref[:, 0:kw], page)
        vn = _pad_rows(new_ref[:, kw:2 * kw], page)
        s = _bdot_nt(qbd, kn) * SCALE + (cq - cnew_rows)
        _online_update(s, lane <= row_t, vn, m_ref, l_ref, acc_ref)

    lf = lf_ref[0, 0]
    suffix = _exact_dot(lf, (pr_i > pc_i).astype(BF16)) + carry_ref[...]
    carry_ref[...] = carry_ref[...] + jnp.sum(lf, axis=-1, keepdims=True)
    s = _bdot_nt(qbd_ref[...], page_ref[0, 0, :, 0:kw]) * SCALE
    s = s + (cq_ref[...] + _rep_rows(suffix, FOX_HEADS, tq))
    _online_update(s, None, page_ref[0, 0, :, kw:2 * kw], m_ref, l_ref, acc_ref)

    @pl.when(p == pl.num_programs(1) - 1)
    def _():
        _write_diag(o_ref, l_ref, acc_ref, FOX_HEADS, 1, tq)


def _fox_sample(qb, cache_kv, cache_lf_t, layer, page_table, rows_new, lf_new_t):
    n, w = qb.shape
    b, n_pages = page_table.shape
    tq = n // b
    page = cache_kv.shape[2]
    rows = FOX_HEADS * tq
    pg = lambda bi, p, pt: (layer, pt[bi, n_pages - 1 - p], 0, 0)
    grid_spec = pltpu.PrefetchScalarGridSpec(
        num_scalar_prefetch=1,
        grid=(b, n_pages),
        in_specs=[pl.BlockSpec((tq, w), lambda bi, p, pt: (bi, 0)),
                  pl.BlockSpec((1, 1, page, 2 * w), pg),
                  pl.BlockSpec((1, 1, FOX_HEADS, page), pg),
                  pl.BlockSpec((tq, 2 * w), lambda bi, p, pt: (bi, 0)),
                  pl.BlockSpec((1, FOX_HEADS, page), lambda bi, p, pt: (bi, 0, 0))],
        out_specs=pl.BlockSpec((tq, w), lambda bi, p, pt: (bi, 0)),
        scratch_shapes=[pltpu.VMEM((rows, w), BF16), pltpu.VMEM((rows, 1), F32),
                        pltpu.VMEM((FOX_HEADS, 1), F32), pltpu.VMEM((rows, 1), F32),
                        pltpu.VMEM((rows, 1), F32), pltpu.VMEM((rows, w), F32)],
    )
    return pl.pallas_call(
        _fox_sample_kernel,
        grid_spec=grid_spec,
        out_shape=jax.ShapeDtypeStruct((n, w), F32),
        compiler_params=_params("parallel", "arbitrary"),
        name="fox_sample",
    )(page_table, qb, cache_kv, cache_lf_t, rows_new, lf_new_t)


def _nsa_sel_sample_kernel(pt_ref, q_ref, page_ref, new_ref, sel_ref, o_ref,
                           qbd_ref, m_ref, l_ref, acc_ref, *, pos0):
    del pt_ref
    p = pl.program_id(1)
    tq = q_ref.shape[0]
    page = page_ref.shape[2]
    kw = NSA_KV_HEADS * HEAD_DIM
    per = NSA_GROUP * tq
    nbp = sel_ref.shape[2]
    lane = lax.broadcasted_iota(jnp.int32, (1, page), 1)
    blk_r = lax.broadcasted_iota(jnp.int32, (nbp, page), 0)
    blk_of_lane = lax.broadcasted_iota(jnp.int32, (nbp, page), 1) >> BLOCK_SHIFT
    selm = sel_ref[0].astype(BF16)

    def key_mask(first_block):
        expand = (blk_r == first_block + blk_of_lane).astype(BF16)
        km = jnp.dot(selm, expand, preferred_element_type=F32)
        return jnp.concatenate(
            [km[g * tq:(g + 1) * tq] for g in range(NSA_KV_HEADS) for _ in range(NSA_GROUP)],
            axis=0) > 0.5

    @pl.when(p == 0)
    def _():
        m_ref[...] = jnp.full_like(m_ref, NEG)
        l_ref[...] = jnp.zeros_like(l_ref)
        acc_ref[...] = jnp.zeros_like(acc_ref)
        qbd = _block_diag_queries(q_ref[...], NSA_KV_HEADS, NSA_GROUP)
        qbd_ref[...] = qbd
        row_t = _token_of_row(NSA_HEADS, tq)
        kn = _pad_rows(new_ref[:, 0:kw], page)
        vn = _pad_rows(new_ref[:, kw:2 * kw], page)
        s = _bdot_nt(qbd, kn) * SCALE
        _online_update(s, key_mask(pos0 // NSA_BLOCK) & (lane <= row_t), vn, m_ref, l_ref, acc_ref)

    s = _bdot_nt(qbd_ref[...], page_ref[0, 0, :, 0:kw]) * SCALE
    _online_update(s, key_mask(p * (page // NSA_BLOCK)), page_ref[0, 0, :, kw:2 * kw],
                   m_ref, l_ref, acc_ref)

    @pl.when(p == pl.num_programs(1) - 1)
    def _():
        _write_diag(o_ref, l_ref, acc_ref, NSA_KV_HEADS, NSA_GROUP, tq)


def _nsa_sel_sample(qr, cache, layer, page_table, rows_new, sel, pos0):
    n, w = qr.shape
    b, n_pages = page_table.shape
    tq = n // b
    page = cache.shape[2]
    kvw = 2 * NSA_KV_HEADS * HEAD_DIM
    rows = NSA_HEADS * tq
    assert pos0 == n_pages * page and pos0 % NSA_BLOCK == 0 and tq <= NSA_BLOCK
    grid_spec = pltpu.PrefetchScalarGridSpec(
        num_scalar_prefetch=1,
        grid=(b, n_pages),
        in_specs=[pl.BlockSpec((tq, w), lambda bi, p, pt: (bi, 0)),
                  pl.BlockSpec((1, 1, page, kvw), lambda bi, p, pt: (layer, pt[bi, p], 0, 1)),
                  pl.BlockSpec((tq, kvw), lambda bi, p, pt: (bi, 1)),
                  pl.BlockSpec((1,) + sel.shape[1:], lambda bi, p, pt: (bi, 0, 0))],
        out_specs=pl.BlockSpec((tq, w), lambda bi, p, pt: (bi, 0)),
        scratch_shapes=[pltpu.VMEM((rows, kvw // 2), BF16), pltpu.VMEM((rows, 1), F32),
                        pltpu.VMEM((rows, 1), F32), pltpu.VMEM((rows, kvw // 2), F32)],
    )
    return pl.pallas_call(
        functools.partial(_nsa_sel_sample_kernel, pos0=pos0),
        grid_spec=grid_spec,
        out_shape=jax.ShapeDtypeStruct((n, w), F32),
        compiler_params=_params("parallel", "arbitrary"),
        name="nsa_sel_sample",
    )(page_table, qr, cache, rows_new, sel)


def _nsa_win_sample_kernel(qr_ref, win_ref, new_ref, oc_ref, os_ref, gate_ref, gbias_ref,
                           o_ref, nwin_ref, kv_ref):
    tq = qr_ref.shape[0]
    lw = win_ref.shape[1]
    nk = kv_ref.shape[0]
    kw = NSA_KV_HEADS * HEAD_DIM
    gw = NSA_GROUP * HEAD_DIM
    kv_ref[0:lw, :] = win_ref[0]
    kv_ref[lw:nk, :] = _pad_rows(new_ref[...], nk - lw)
    nwin_ref[0, 0:lw - tq, :] = win_ref[0, tq:lw, :]
    nwin_ref[0, lw - tq:lw, :] = new_ref[...]

    idx = lax.broadcasted_iota(jnp.int32, (1, nk), 1)
    row_t = _token_of_row(NSA_GROUP, tq)
    dist = lw + row_t - idx
    mask = (dist >= 0) & (dist < NSA_WINDOW) & (idx < lw + tq)
    gates = _sigmoid(gate_ref[...] + gbias_ref[...])
    for g in range(NSA_KV_HEADS):
        qr = _stack_heads(qr_ref[:, g * gw:(g + 1) * gw], NSA_GROUP)
        s = _bdot_nt(qr, kv_ref[:, g * HEAD_DIM:(g + 1) * HEAD_DIM]) * SCALE
        p_w, _, _ = _softmax_rows(s, mask)
        o_w = _bdot(p_w, kv_ref[:, kw + g * HEAD_DIM:kw + (g + 1) * HEAD_DIM])
        for r in range(NSA_GROUP):
            cs = slice(g * gw + r * HEAD_DIM, g * gw + (r + 1) * HEAD_DIM)
            gc = g * LANES + 3 * r
            o_ref[:, cs] = (gates[:, gc:gc + 1] * oc_ref[:, cs] + gates[:, gc + 1:gc + 2] * os_ref[:, cs]
                            + gates[:, gc + 2:gc + 3] * o_w[r * tq:(r + 1) * tq])


def _nsa_win_sample(qr, win, rows_win, o_c, o_s, gates, gbias, batch):
    n, w = qr.shape
    tq = n // batch
    lw, ww = win.shape[1], win.shape[2]
    nk = lw + LANES
    assert lw >= NSA_WINDOW and tq % 8 == 0 and tq <= LANES
    row = lambda a: pl.BlockSpec((tq, a.shape[1]), lambda b: (b, 0))
    return pl.pallas_call(
        _nsa_win_sample_kernel,
        grid=(batch,),
        in_specs=[row(qr), pl.BlockSpec((1, lw, ww), lambda b: (b, 0, 0)), row(rows_win),
                  row(o_c), row(o_s), row(gates), pl.BlockSpec(gbias.shape, lambda b: (0, 0))],
        out_specs=[row(qr), pl.BlockSpec((1, lw, ww), lambda b: (b, 0, 0))],
        out_shape=[jax.ShapeDtypeStruct((n, w), F32), jax.ShapeDtypeStruct(win.shape, F32)],
        scratch_shapes=[pltpu.VMEM((nk, ww), F32)],
        compiler_params=_params("parallel"),
        name="nsa_win_sample",
    )(qr, win, rows_win, o_c, o_s, gates, gbias)


def _dil_sample_kernel(q_ref, st_ref, new_ref, o_ref, nst_ref, p_ref):
    c = pl.program_id(1)
    tq = q_ref.shape[0]
    lw = st_ref.shape[1]
    n_g = len(DIL_PAIRS)
    nst_ref[0, 0:lw - tq, :] = st_ref[0, tq:lw, :]
    nst_ref[0, lw - tq:lw, :] = new_ref[...]
    new_pad = _pad_rows(new_ref[...], LANES)

    @pl.when(c < DIL_HEADS)
    def _():
        q = q_ref[...]
        s = jnp.concatenate([_bdot_nt(q, st_ref[0]), _bdot_nt(q, new_pad)], axis=1) * SCALE
        j = lax.broadcasted_iota(jnp.int32, (1, lw + LANES), 1)
        off = lw + lax.broadcasted_iota(jnp.int32, (tq, 1), 0) - j
        outs, lses = [], []
        for window, dil in DIL_PAIRS:
            p, m, den = _softmax_rows(s, (off >= 0) & (off <= window) & ((off & (dil - 1)) == 0))
            outs.append(p)
            lses.append(m + jnp.log(den))
        mx = jnp.maximum(jnp.maximum(lses[0], lses[1]), lses[2])
        ws = [jnp.exp(l - mx) for l in lses]
        tot = ws[0] + ws[1] + ws[2]
        for gi in range(n_g):
            p_ref[c, gi * tq:(gi + 1) * tq, :] = outs[gi] * (ws[gi] / tot)

    @pl.when(c >= DIL_HEADS)
    def _():
        pm = p_ref[c - DIL_HEADS]
        o = _bdot(pm[:, 0:lw], st_ref[0]) + _bdot(pm[:, lw:lw + LANES], new_pad)
        o_ref[...] = o[0:tq] + o[tq:2 * tq] + o[2 * tq:3 * tq]


def _dil_sample(q, state, rows_new, batch):
    n, w = q.shape
    tq = n // batch
    lw, sw = state.shape[1], state.shape[2]
    assert lw >= DIL_PAIRS[-1][0] and tq % 8 == 0
    nc = sw // HEAD_DIM
    return pl.pallas_call(
        _dil_sample_kernel,
        grid=(batch, nc),
        in_specs=[pl.BlockSpec((tq, HEAD_DIM), lambda b, c: (b, jnp.minimum(c, DIL_HEADS - 1))),
                  pl.BlockSpec((1, lw, HEAD_DIM), lambda b, c: (b, 0, c)),
                  pl.BlockSpec((tq, HEAD_DIM), lambda b, c: (b, c))],
        out_specs=[pl.BlockSpec((tq, HEAD_DIM), lambda b, c: (b, jnp.maximum(c - DIL_HEADS, 0))),
                   pl.BlockSpec((1, lw, HEAD_DIM), lambda b, c: (b, 0, c))],
        out_shape=[jax.ShapeDtypeStruct((n, w), F32), jax.ShapeDtypeStruct(state.shape, F32)],
        scratch_shapes=[pltpu.VMEM((DIL_HEADS, len(DIL_PAIRS) * tq, lw + LANES), F32)],
        compiler_params=_params("parallel", "arbitrary"),
        name="dil_sample",
    )(q, state, rows_new)


def _rope_tables(pos):
    half = HEAD_DIM // 2
    inv = ROPE_THETA ** (-jnp.arange(half, dtype=F32) / half)
    ang = pos.astype(F32)[:, None] * inv[None, :]
    cos, sin = jnp.cos(ang), jnp.sin(ang)
    return jnp.concatenate([cos, cos], axis=-1), jnp.concatenate([-sin, sin], axis=-1)


def _pad_lanes(a, width=LANES):
    return jnp.pad(a, ((0, 0),) * (a.ndim - 1) + ((0, width - a.shape[-1]),))


def _chunks(first, count, op="copy", gi=0):
    return [(first + c, op, gi) for c in range(count)]


NSA_W = NSA_HEADS * HEAD_DIM
NSA_KV_W = NSA_KV_HEADS * HEAD_DIM
FOX_W = FOX_HEADS * HEAD_DIM
DIL_W = DIL_HEADS * HEAD_DIM
GATES_PER_GROUP = 3 * NSA_GROUP


def _even_weights(w_in, w_out, gate_b, forget_b, nsa_g, fox_g):
    cuts = [0]
    for wd in ((NSA_W,) + (NSA_KV_W,) * 6
               + (3 * NSA_HEADS, NSA_W, FOX_W, FOX_W, FOX_W, FOX_HEADS, FOX_W)):
        cuts.append(cuts[-1] + wd)
    seg = lambda a, b: w_in[:, cuts[a]:cuts[b]]
    gate_w, f_w = seg(7, 8), seg(12, 13)
    small = [_pad_lanes(f_w)] + [_pad_lanes(gate_w[:, g * GATES_PER_GROUP:(g + 1) * GATES_PER_GROUP])
                                 for g in range(NSA_KV_HEADS)]
    gbias = jnp.concatenate(
        [_pad_lanes(gate_b[None, g * GATES_PER_GROUP:(g + 1) * GATES_PER_GROUP])
         for g in range(NSA_KV_HEADS)], axis=1)
    aux = jnp.stack([nsa_g[0], nsa_g[2], nsa_g[3], fox_g[0], fox_g[1], _pad_lanes(forget_b),
                     jnp.zeros((LANES,), F32), jnp.zeros((LANES,), F32)])
    return dict(
        w_a=seg(0, 7).astype(BF16),
        w_b=seg(8, 10).astype(BF16),
        w_c=seg(10, 12).astype(BF16),
        w_d=jnp.concatenate([seg(13, 14)] + small, axis=1).astype(BF16),
        w_out_a=w_out[:NSA_W].astype(BF16), w_out_b=w_out[NSA_W:].astype(BF16),
        aux=aux, gbias=gbias, g_kc=nsa_g[1][None, :])


def _even_projections(x, g_norm, wts, pos):
    h = _rmsnorm(x, g_norm)
    cos2, sin2 = _rope_tables(pos)
    nq, nkv = NSA_HEADS, NSA_KV_HEADS
    qn, qr, rows_nsa, rows_win = _proj(
        h, wts["w_a"], wts["aux"], cos2, sin2,
        [[_chunks(0, nq, "rms", 0)], [_chunks(0, nq, "rms_rope", 0)],
         [_chunks(nq, 2 * nkv) + _chunks(nq + 2 * nkv, nkv, "rms_rope", 1) + _chunks(nq + 3 * nkv, nkv)],
         [_chunks(nq + 4 * nkv, nkv, "rms_rope", 2) + _chunks(nq + 5 * nkv, nkv)]],
        wts["w_a"].shape[1])
    z_a, qb = _proj(h, wts["w_b"], wts["aux"], cos2, sin2,
                    [[_chunks(0, nq)], [_chunks(nq, FOX_HEADS, "rms", 3)]], wts["w_b"].shape[1])
    rows_fox, = _proj(h, wts["w_c"], wts["aux"], cos2, sin2,
                      [[_chunks(0, FOX_HEADS, "rms", 4) + _chunks(FOX_HEADS, FOX_HEADS)]],
                      wts["w_c"].shape[1])
    z_b, logf, gates = _proj(
        h, wts["w_d"], wts["aux"], cos2, sin2,
        [[_chunks(0, FOX_HEADS)], [_chunks(FOX_HEADS, 1, "logsig", 5)], [_chunks(FOX_HEADS + 1, nkv)]],
        wts["w_d"].shape[1])
    return qn, qr, rows_nsa, rows_win, z_a, qb, rows_fox, z_b, logf, gates


def _even_prompt(x, batch, g_norm, wts, cmp_pos):
    n = x.shape[0]
    t = n // batch
    pos = jnp.tile(jnp.arange(t, dtype=jnp.int32), batch)
    qn, qr, rows_nsa, rows_win, z_a, qb, rows_fox, z_b, logf, gates = _even_projections(
        x, g_norm, wts, pos)
    c_all = _cumsum_rows(logf, batch)
    ck_t = jnp.transpose(c_all[:, :FOX_HEADS].reshape(batch, t, FOX_HEADS), (0, 2, 1))
    o_b = _fox_prompt(qb, rows_fox, c_all, ck_t, batch)
    o_a = _nsa_prompt(qn, qr, rows_nsa, rows_win, gates, wts["gbias"], cmp_pos, wts["g_kc"], batch)
    y = _outproj(x, [o_a, o_b], [z_a, z_b], [wts["w_out_a"], wts["w_out_b"]])
    lw = min(NSA_WINDOW, t)
    new_win = rows_win.reshape(batch, t, 2, NSA_KV_HEADS, HEAD_DIM)[:, t - lw:]
    return y, (rows_nsa.reshape(batch, t, 4, NSA_KV_HEADS, HEAD_DIM),
               rows_fox.reshape(batch, t, 2, FOX_HEADS, HEAD_DIM),
               logf[:, :FOX_HEADS].reshape(batch, t, FOX_HEADS), new_win)


def _even_sample(x, batch, pos0, g_norm, wts, cmp_pos, layer, cache_nsa, cache_fox, cache_lf_t,
                 win_state, page_table):
    n = x.shape[0]
    tq = n // batch
    pos = jnp.tile(pos0 + jnp.arange(tq, dtype=jnp.int32), batch)
    qn, qr, rows_nsa, rows_win, z_a, qb, rows_fox, z_b, logf, gates = _even_projections(
        x, g_norm, wts, pos)
    means = _page_means(cache_nsa, layer, page_table, cmp_pos)
    n_blocks = means.shape[1] * means.shape[2]
    nbp = -(-(n_blocks + 1) // LANES) * LANES
    means = jnp.pad(means.reshape(batch, n_blocks, means.shape[3]),
                    ((0, 0), (0, nbp - n_blocks), (0, 0)))
    o_c, sel = _nsa_cmp_sample(qn, means, wts["g_kc"], batch, pos0)
    o_s = _nsa_sel_sample(qr, cache_nsa, layer, page_table, rows_nsa, sel, pos0)
    lw = win_state.shape[1]
    win = win_state.reshape(batch, lw, 2 * NSA_KV_W)
    o_a, new_win = _nsa_win_sample(qr, win, rows_win, o_c, o_s, gates, wts["gbias"], batch)
    page = cache_fox.shape[2]
    lf_new_t = _pad_lanes(
        jnp.transpose(logf[:, :FOX_HEADS].reshape(batch, tq, FOX_HEADS), (0, 2, 1)), page)
    o_b = _fox_sample(qb, cache_fox, cache_lf_t, layer, page_table, rows_fox, lf_new_t)
    y = _outproj(x, [o_a, o_b], [z_a, z_b], [wts["w_out_a"], wts["w_out_b"]])
    return y, (rows_nsa.reshape(batch, tq, 4, NSA_KV_HEADS, HEAD_DIM),
               rows_fox.reshape(batch, tq, 2, FOX_HEADS, HEAD_DIM),
               logf[:, :FOX_HEADS].reshape(batch, tq, FOX_HEADS),
               new_win.reshape(batch, lw, 2, NSA_KV_HEADS, HEAD_DIM))


def _odd_weights(w_in, w_out, qk_g):
    aux = jnp.concatenate([qk_g, jnp.zeros((6, HEAD_DIM), F32)], axis=0)
    return dict(w_q=w_in[:, :DIL_W].astype(BF16), w_kv=w_in[:, DIL_W:3 * DIL_W].astype(BF16),
                w_z=w_in[:, 3 * DIL_W:].astype(BF16), w_out=w_out.astype(BF16), aux=aux)


def _odd_projections(x, g_norm, wts, pos):
    h = _rmsnorm(x, g_norm)
    cos2, sin2 = _rope_tables(pos)
    q, = _proj(h, wts["w_q"], wts["aux"], cos2, sin2, [[_chunks(0, DIL_HEADS, "rms_rope", 0)]], DIL_W)
    rows, = _proj(h, wts["w_kv"], wts["aux"], cos2, sin2,
                  [[_chunks(0, DIL_HEADS, "rms_rope", 1), _chunks(0, DIL_HEADS)]], DIL_W)
    z, = _proj(h, wts["w_z"], wts["aux"], cos2, sin2, [[_chunks(0, DIL_HEADS)]], DIL_W)
    return q, rows, z


def _odd_prompt(x, batch, g_norm, wts):
    n = x.shape[0]
    t = n // batch
    pos = jnp.tile(jnp.arange(t, dtype=jnp.int32), batch)
    q, rows, z = _odd_projections(x, g_norm, wts, pos)
    o = _dil_prompt(q, rows, batch)
    y = _outproj(x, [o], [z], [wts["w_out"]])
    lw = min(DIL_PAIRS[-1][0], t)
    return y, rows.reshape(batch, t, 2, DIL_HEADS, HEAD_DIM)[:, t - lw:]


def _odd_sample(x, batch, pos0, g_norm, wts, state):
    n = x.shape[0]
    tq = n // batch
    pos = jnp.tile(pos0 + jnp.arange(tq, dtype=jnp.int32), batch)
    q, rows, z = _odd_projections(x, g_norm, wts, pos)
    lw = state.shape[1]
    o, new_state = _dil_sample(q, state.reshape(batch, lw, 2 * DIL_W), rows, batch)
    y = _outproj(x, [o], [z], [wts["w_out"]])
    return y, new_state.reshape(batch, lw, 2, DIL_HEADS, HEAD_DIM)


def kernel(x_prompt, x_sample, cache_nsa_kv, cache_fox_kv, cache_fox_logf, state_nsa_win_kv, state_dil_kv, page_table, norm_even, w_in_even, w_out_even, nsa_gate_bias, fox_forget_bias, nsa_cmp_pos, nsa_qk_gain, fox_qk_gain, norm_odd, w_in_odd, w_out_odd, dil_qk_gain):
    bp, t, d = x_prompt.shape
    bs, tq, _ = x_sample.shape
    n_layers, n_pool, page = cache_nsa_kv.shape[:3]
    pos0 = page_table.shape[1] * page
    depth = norm_even.shape[0] + norm_odd.shape[0]
    cache_nsa = cache_nsa_kv.reshape(n_layers, n_pool, page, 4 * NSA_KV_W)
    cache_fox = cache_fox_kv.reshape(n_layers, n_pool, page, 2 * FOX_W)
    cache_lf_t = jnp.transpose(cache_fox_logf, (0, 1, 3, 2))
    yp = x_prompt.reshape(bp * t, d)
    ys = x_sample.reshape(bs * tq, d)
    outs = [[] for _ in range(10)]
    for layer in range(depth):
        i = layer // 2
        if layer % 2 == 0:
            wts = _even_weights(w_in_even[i], w_out_even[i], nsa_gate_bias[i], fox_forget_bias[i],
                                nsa_qk_gain[i], fox_qk_gain[i])
            yp, new_p = _even_prompt(yp, bp, norm_even[i], wts, nsa_cmp_pos[i])
            ys, new_s = _even_sample(ys, bs, pos0, norm_even[i], wts, nsa_cmp_pos[i], i, cache_nsa,
                                     cache_fox, cache_lf_t, state_nsa_win_kv[i], page_table)
            for k in range(4):
                outs[2 * k].append(new_p[k])
                outs[2 * k + 1].append(new_s[k])
        else:
            wts = _odd_weights(w_in_odd[i], w_out_odd[i], dil_qk_gain[i])
            yp, buf_p = _odd_prompt(yp, bp, norm_odd[i], wts)
            ys, buf_s = _odd_sample(ys, bs, pos0, norm_odd[i], wts, state_dil_kv[i])
            outs[8].append(buf_p)
            outs[9].append(buf_s)
    return (yp.reshape(bp, t, d), ys.reshape(bs, tq, d)) + tuple(jnp.stack(o) for o in outs)
```

```python
import functools
import math

import jax
import jax.numpy as jnp
from jax import lax
from jax.experimental import pallas as pl
from jax.experimental.pallas import tpu as pltpu

F32 = jnp.float32
BF16 = jnp.bfloat16

HEAD_DIM = 128
LANES = 128
NSA_HEADS = 8
NSA_KV_HEADS = 2
NSA_GROUP = NSA_HEADS // NSA_KV_HEADS
NSA_BLOCK = 64
NSA_TOP_N = 16
NSA_WINDOW = 512
NSA_FORCE_SCORE = 1.0e4
FOX_HEADS = 8
DIL_HEADS = 16
DIL_PAIRS = ((128, 1), (512, 4), (2048, 16))
ROPE_THETA = 10000.0
EPS = 1e-6
SCALE = HEAD_DIM ** -0.5
NEG = -1.0e30
VMEM_LIMIT = 56 * 1024 * 1024


def _params(*sem):
    return pltpu.CompilerParams(dimension_semantics=sem, vmem_limit_bytes=VMEM_LIMIT)


def _bdot(a, b):
    return jnp.dot(a.astype(BF16), b.astype(BF16), preferred_element_type=F32)


def _bdot_nt(a, b):
    return lax.dot_general(a.astype(BF16), b.astype(BF16), (((1,), (1,)), ((), ())),
                           preferred_element_type=F32)


def _split3(x):
    hi = x.astype(BF16)
    r1 = x - hi.astype(F32)
    mid = r1.astype(BF16)
    lo = (r1 - mid.astype(F32)).astype(BF16)
    return hi, mid, lo


def _exact_dot(x, m):
    hi, mid, lo = _split3(x)
    return (jnp.dot(hi, m, preferred_element_type=F32)
            + jnp.dot(mid, m, preferred_element_type=F32)
            + jnp.dot(lo, m, preferred_element_type=F32))


def _exact_dot_left(m, x):
    hi, mid, lo = _split3(x)
    return (jnp.dot(m, hi, preferred_element_type=F32)
            + jnp.dot(m, mid, preferred_element_type=F32)
            + jnp.dot(m, lo, preferred_element_type=F32))


def _token_of_row(n_rep, tq):
    return jnp.concatenate([lax.broadcasted_iota(jnp.int32, (tq, 1), 0)] * n_rep, axis=0)


BLOCK_SHIFT = 6


def _sigmoid(x):
    return 1.0 / (1.0 + jnp.exp(-x))


def _log_sigmoid(x):
    return -(jnp.maximum(-x, 0.0) + jnp.log(1.0 + jnp.exp(-jnp.abs(x))))


def _head_rms(y, g):
    return y * lax.rsqrt(jnp.mean(y * y, axis=-1, keepdims=True) + EPS) * g


def _softmax_rows(s, mask):
    s = jnp.where(mask, s, -jnp.inf)
    m = jnp.max(s, axis=-1, keepdims=True)
    m = jnp.where(m > -jnp.inf, m, 0.0)
    e = jnp.exp(s - m)
    den = jnp.sum(e, axis=-1, keepdims=True)
    return e / jnp.maximum(den, 1e-30), m, den


def _rmsnorm_kernel(x_ref, g_ref, h_ref):
    x = x_ref[...]
    y = x * lax.rsqrt(jnp.mean(x * x, axis=-1, keepdims=True) + EPS)
    h_ref[...] = (y * g_ref[...]).astype(BF16)


def _rmsnorm(x, g):
    n, d = x.shape
    tm = min(n, 512)
    return pl.pallas_call(
        _rmsnorm_kernel,
        grid=(n // tm,),
        in_specs=[pl.BlockSpec((tm, d), lambda i: (i, 0)),
                  pl.BlockSpec((1, d), lambda i: (0, 0))],
        out_specs=pl.BlockSpec((tm, d), lambda i: (i, 0)),
        out_shape=jax.ShapeDtypeStruct((n, d), BF16),
        compiler_params=_params("parallel"),
        name="rmsnorm",
    )(x, g.reshape(1, d))


def _proj_kernel(h_ref, w_ref, aux_ref, cos_ref, sin_ref, *out_refs, plans, n_tiles, head_major):
    acc = jnp.dot(h_ref[...], w_ref[...], preferred_element_type=F32)
    tm = h_ref.shape[0]

    def emit(tile):
        for out_ref, plan, hm in zip(out_refs, plans, head_major):
            n_c = len(plan[tile])
            for c, (src, op, gi) in enumerate(plan[tile]):
                y = acc[:, src * LANES:(src + 1) * LANES]
                if op in ("rms", "rms_rope"):
                    y = _head_rms(y, aux_ref[gi:gi + 1, :])
                if op == "rms_rope":
                    y = y * cos_ref[...] + pltpu.roll(y, HEAD_DIM // 2, 1) * sin_ref[...]
                if op == "logsig":
                    y = _log_sigmoid(y + aux_ref[gi:gi + 1, :])
                if hm:
                    out_ref[pl.ds(tile * n_c + c, tm, stride=n_tiles * n_c), :] = y
                else:
                    out_ref[:, c * LANES:(c + 1) * LANES] = y

    if n_tiles == 1:
        emit(0)
    else:
        for t in range(n_tiles):
            pl.when(pl.program_id(0) == t)(functools.partial(emit, t))


def _proj(h, w, aux, cos2, sin2, plans, tn, head_major=None):
    n, k = h.shape
    n_tiles = w.shape[1] // tn
    tm = min(n, 256)
    head_major = head_major or [False] * len(plans)
    out_specs, out_shape = [], []
    for plan, hm in zip(plans, head_major):
        n_c = len(plan[0])
        if hm:
            assert n_tiles == 1 or n == tm
            out_specs.append(pl.BlockSpec((tm * n_tiles * n_c, LANES), lambda j, i: (i, 0)))
            out_shape.append(jax.ShapeDtypeStruct((n * n_tiles * n_c, LANES), F32))
        else:
            out_specs.append(pl.BlockSpec((tm, n_c * LANES), lambda j, i: (i, j)))
            out_shape.append(jax.ShapeDtypeStruct((n, n_tiles * n_c * LANES), F32))
    kern = functools.partial(_proj_kernel, plans=plans, n_tiles=n_tiles, head_major=head_major)
    outs = pl.pallas_call(
        kern,
        grid=(n_tiles, n // tm),
        in_specs=[pl.BlockSpec((tm, k), lambda j, i: (i, 0)),
                  pl.BlockSpec((k, tn), lambda j, i: (0, j)),
                  pl.BlockSpec(aux.shape, lambda j, i: (0, 0)),
                  pl.BlockSpec((tm, LANES), lambda j, i: (i, 0)),
                  pl.BlockSpec((tm, LANES), lambda j, i: (i, 0))],
        out_specs=out_specs,
        out_shape=out_shape,
        compiler_params=_params("arbitrary", "arbitrary"),
        name="proj",
    )(h, w, aux, cos2, sin2)
    return outs


def _outproj_kernel(*refs, n_parts):
    x_ref = refs[0]
    y_ref = refs[-1]
    acc = x_ref[...]
    for p in range(n_parts):
        o = refs[1 + p][...]
        z = refs[1 + n_parts + p][...]
        mix = o * (z * _sigmoid(z))
        acc = acc + jnp.dot(mix.astype(BF16), refs[1 + 2 * n_parts + p][...],
                            preferred_element_type=F32)
    y_ref[...] = acc


def _outproj(x, o_parts, z_parts, w_parts):
    n, d = x.shape
    tm = min(n, 256)
    n_parts = len(o_parts)
    row = lambda a: pl.BlockSpec((tm, a.shape[1]), lambda i: (i, 0))
    return pl.pallas_call(
        functools.partial(_outproj_kernel, n_parts=n_parts),
        grid=(n // tm,),
        in_specs=([row(x)] + [row(o) for o in o_parts] + [row(z) for z in z_parts]
                  + [pl.BlockSpec(w.shape, lambda i: (0, 0)) for w in w_parts]),
        out_specs=row(x),
        out_shape=jax.ShapeDtypeStruct((n, d), F32),
        compiler_params=_params("parallel"),
        name="outproj",
    )(x, *o_parts, *z_parts, *w_parts)


CUM_BLOCK = 256


def _cumsum_kernel(x_ref, tri_ref, c_ref, carry_ref):
    @pl.when(pl.program_id(1) == 0)
    def _():
        carry_ref[...] = jnp.zeros_like(carry_ref)

    c = _exact_dot_left(tri_ref[...], x_ref[...]) + carry_ref[...]
    c_ref[...] = c
    carry_ref[...] = c[CUM_BLOCK - 1:CUM_BLOCK, :]


def _cumsum_rows(x, batch):
    n, w = x.shape
    t = n // batch
    nb = t // CUM_BLOCK
    r = lax.broadcasted_iota(jnp.int32, (CUM_BLOCK, CUM_BLOCK), 0)
    c = lax.broadcasted_iota(jnp.int32, (CUM_BLOCK, CUM_BLOCK), 1)
    tri = (c <= r).astype(BF16)
    return pl.pallas_call(
        _cumsum_kernel,
        grid=(batch, nb),
        in_specs=[pl.BlockSpec((CUM_BLOCK, w), lambda b, j: (b * nb + j, 0)),
                  pl.BlockSpec((CUM_BLOCK, CUM_BLOCK), lambda b, j: (0, 0))],
        out_specs=pl.BlockSpec((CUM_BLOCK, w), lambda b, j: (b * nb + j, 0)),
        out_shape=jax.ShapeDtypeStruct((n, w), F32),
        scratch_shapes=[pltpu.VMEM((1, w), F32)],
        compiler_params=_params("parallel", "arbitrary"),
        name="logf_cumsum",
    )(x, tri)


FOX_TILE = 512


def _fox_prompt_kernel(q_ref, k_ref, v_ref, cq_ref, ck_ref, o_ref, m_ref, l_ref, acc_ref):
    i = pl.program_id(1)
    j = pl.program_id(2)
    tq = q_ref.shape[0]
    tk = k_ref.shape[0]

    @pl.when(j == 0)
    def _():
        m_ref[...] = jnp.full_like(m_ref, NEG)
        l_ref[...] = jnp.zeros_like(l_ref)
        acc_ref[...] = jnp.zeros_like(acc_ref)

    @pl.when(j <= i)
    def _():
        qpos = i * tq + lax.broadcasted_iota(jnp.int32, (tq, tk), 0)
        kpos = j * tk + lax.broadcasted_iota(jnp.int32, (tq, tk), 1)
        mask = qpos >= kpos
        for h in range(FOX_HEADS):
            sl = slice(h * HEAD_DIM, (h + 1) * HEAD_DIM)
            s = _bdot_nt(q_ref[:, sl], k_ref[:, sl]) * SCALE
            s = s + (cq_ref[:, h:h + 1] - ck_ref[0, h:h + 1, :])
            s = jnp.where(mask, s, NEG)
            m_old = m_ref[h]
            m_new = jnp.maximum(m_old, jnp.max(s, axis=-1, keepdims=True))
            p = jnp.where(mask, jnp.exp(s - m_new), 0.0)
            alpha = jnp.exp(m_old - m_new)
            l_ref[h] = alpha * l_ref[h] + jnp.sum(p, axis=-1, keepdims=True)
            acc_ref[:, sl] = alpha * acc_ref[:, sl] + _bdot(p, v_ref[:, sl])
            m_ref[h] = m_new

    @pl.when(j == i)
    def _():
        for h in range(FOX_HEADS):
            sl = slice(h * HEAD_DIM, (h + 1) * HEAD_DIM)
            o_ref[:, sl] = acc_ref[:, sl] / jnp.maximum(l_ref[h], 1e-30)


def _fox_prompt(qb, rows_fox, c_all, ck_t, batch):
    n, w = qb.shape
    t = n // batch
    tq = min(FOX_TILE, t)
    nq = t // tq
    return pl.pallas_call(
        _fox_prompt_kernel,
        grid=(batch, nq, nq),
        in_specs=[pl.BlockSpec((tq, w), lambda b, i, j: (b * nq + i, 0)),
                  pl.BlockSpec((tq, w), lambda b, i, j: (b * nq + jnp.minimum(j, i), 0)),
                  pl.BlockSpec((tq, w), lambda b, i, j: (b * nq + jnp.minimum(j, i), 1)),
                  pl.BlockSpec((tq, LANES), lambda b, i, j: (b * nq + i, 0)),
                  pl.BlockSpec((1, FOX_HEADS, tq), lambda b, i, j: (b, 0, jnp.minimum(j, i)))],
        out_specs=pl.BlockSpec((tq, w), lambda b, i, j: (b * nq + i, 0)),
        out_shape=jax.ShapeDtypeStruct((n, w), F32),
        scratch_shapes=[pltpu.VMEM((FOX_HEADS, tq, 1), F32),
                        pltpu.VMEM((FOX_HEADS, tq, 1), F32),
                        pltpu.VMEM((tq, w), F32)],
        compiler_params=_params("parallel", "parallel", "arbitrary"),
        name="fox_prompt",
    )(qb, rows_fox, rows_fox, c_all, ck_t)


def _stack_heads(x, n):
    return jnp.concatenate([x[:, r * HEAD_DIM:(r + 1) * HEAD_DIM] for r in range(n)], axis=0)


def _top_n_mask(imp, blk, n_sel):
    work = imp
    selm = jnp.zeros(imp.shape, F32)
    for _ in range(n_sel):
        m = jnp.max(work, axis=-1, keepdims=True)
        idx = jnp.min(jnp.where(work == m, blk, 1.0e9), axis=-1, keepdims=True)
        pick = blk == idx
        selm = jnp.where(pick, 1.0, selm)
        work = jnp.where(pick, -jnp.inf, work)
    return selm


def _importance(p_c, rows, qpos_col, blk):
    imp = p_c[0:rows]
    for r in range(1, NSA_GROUP):
        imp = imp + p_c[r * rows:(r + 1) * rows]
    cur = (qpos_col >> BLOCK_SHIFT).astype(F32)
    forced = (blk == 0.0) | (blk == cur) | (blk == cur - 1.0)
    imp = jnp.where(forced, NSA_FORCE_SCORE, imp)
    return jnp.where(blk <= cur, imp, NEG)


NSA_Q = 128
NSA_TK = 512
NSA_WKEYS = NSA_WINDOW + NSA_Q


def _nsa_prompt_kernel(qn_ref, qr_ref, kc_ref, vc_ref, ks_ref, vs_ref, kw_ref, vw_ref,
                       gate_ref, gbias_ref, cpos_ref, gkc_ref, e_ref, o_ref,
                       kcn_ref, vcm_ref):
    i = pl.program_id(2)
    t = kc_ref.shape[0]
    nb = t // NSA_BLOCK
    nbp = kcn_ref.shape[0]

    @pl.when(i == 0)
    def _():
        kcn_ref[...] = jnp.zeros_like(kcn_ref)
        vcm_ref[...] = jnp.zeros_like(vcm_ref)

        def body(jb, carry):
            rows = pl.ds(pl.multiple_of(jb * NSA_BLOCK, NSA_BLOCK), NSA_BLOCK)
            km = jnp.mean(kc_ref[rows, :] + cpos_ref[...], axis=0, keepdims=True)
            kcn_ref[pl.ds(jb, 1), :] = _head_rms(km, gkc_ref[...])
            vcm_ref[pl.ds(jb, 1), :] = jnp.mean(vc_ref[rows, :], axis=0, keepdims=True)
            return carry

        lax.fori_loop(0, nb, body, 0)

    rq = NSA_GROUP * NSA_Q
    qn = _stack_heads(qn_ref[...], NSA_GROUP)
    qr = _stack_heads(qr_ref[...], NSA_GROUP).astype(BF16)
    qpos1 = i * NSA_Q + lax.broadcasted_iota(jnp.int32, (NSA_Q, 1), 0)
    qpos = jnp.concatenate([qpos1] * NSA_GROUP, axis=0)

    blk_i = lax.broadcasted_iota(jnp.int32, (1, nbp), 1)
    s = _bdot_nt(qn, kcn_ref[...]) * SCALE
    p_c, _, _ = _softmax_rows(s, (blk_i + 1) * NSA_BLOCK - 1 <= qpos)
    o_c = _bdot(p_c, vcm_ref[...])

    blk = blk_i.astype(F32)
    imp = _importance(p_c, NSA_Q, qpos1, blk)
    selm = _top_n_mask(imp, blk, min(NSA_TOP_N, nb)).astype(BF16)

    def sel_body(c, carry):
        m_old, l_old, acc = carry
        rows = pl.ds(pl.multiple_of(c * NSA_TK, NSA_TK), NSA_TK)
        kmask = jnp.dot(selm, e_ref[c], preferred_element_type=F32)
        kmask = jnp.concatenate([kmask] * NSA_GROUP, axis=0)
        kpos = c * NSA_TK + lax.broadcasted_iota(jnp.int32, (1, NSA_TK), 1)
        mask = (kmask > 0.5) & (kpos <= qpos)
        sc = _bdot_nt(qr, ks_ref[rows, :]) * SCALE
        sc = jnp.where(mask, sc, NEG)
        m_new = jnp.maximum(m_old, jnp.max(sc, axis=-1, keepdims=True))
        p = jnp.where(mask, jnp.exp(sc - m_new), 0.0)
        alpha = jnp.exp(m_old - m_new)
        l_new = alpha * l_old + jnp.sum(p, axis=-1, keepdims=True)
        acc = alpha * acc + _bdot(p, vs_ref[rows, :])
        return m_new, l_new, acc

    n_chunks = (i * NSA_Q + NSA_Q + NSA_TK - 1) // NSA_TK
    init = (jnp.full((rq, 1), NEG, F32), jnp.zeros((rq, 1), F32), jnp.zeros((rq, HEAD_DIM), F32))
    _, l_s, acc_s = lax.fori_loop(0, n_chunks, sel_body, init)
    o_s = acc_s / jnp.maximum(l_s, 1e-30)

    k0 = jnp.clip(i * NSA_Q - NSA_WINDOW, 0, t - NSA_WKEYS)
    k0 = pl.multiple_of(k0, NSA_Q)
    wrows = pl.ds(k0, NSA_WKEYS)
    kpos = k0 + lax.broadcasted_iota(jnp.int32, (1, NSA_WKEYS), 1)
    dist = qpos - kpos
    sw = _bdot_nt(qr, kw_ref[wrows, :]) * SCALE
    p_w, _, _ = _softmax_rows(sw, (dist >= 0) & (dist < NSA_WINDOW))
    o_w = _bdot(p_w, vw_ref[wrows, :])

    gates = _sigmoid(gate_ref[...] + gbias_ref[...])
    for r in range(NSA_GROUP):
        rs = slice(r * NSA_Q, (r + 1) * NSA_Q)
        o = (gates[:, 3 * r:3 * r + 1] * o_c[rs] + gates[:, 3 * r + 1:3 * r + 2] * o_s[rs]
             + gates[:, 3 * r + 2:3 * r + 3] * o_w[rs])
        o_ref[:, r * HEAD_DIM:(r + 1) * HEAD_DIM] = o


def _expand_matrix(nbp, n_keys, tk):
    kb = (jnp.arange(n_keys, dtype=jnp.int32) // NSA_BLOCK).reshape(n_keys // tk, 1, tk)
    jb = jnp.arange(nbp, dtype=jnp.int32).reshape(1, nbp, 1)
    return (kb == jb).astype(BF16)


def _nsa_prompt(qn, qr, rows_nsa, rows_win, gates, gbias, cmp_pos, g_kc, batch):
    n = qn.shape[0]
    t = n // batch
    assert t % NSA_TK == 0 and t >= NSA_WKEYS
    nqb = t // NSA_Q
    nb = t // NSA_BLOCK
    nbp = -(-nb // LANES) * LANES
    e = _expand_matrix(nbp, t, NSA_TK)
    gw = NSA_GROUP * HEAD_DIM
    qspec = pl.BlockSpec((NSA_Q, gw), lambda b, g, i: (b * nqb + i, g))
    kv = lambda typ: pl.BlockSpec((t, HEAD_DIM), lambda b, g, i: (b, typ * NSA_KV_HEADS + g))
    return pl.pallas_call(
        _nsa_prompt_kernel,
        grid=(batch, NSA_KV_HEADS, nqb),
        in_specs=[qspec, qspec, kv(0), kv(1), kv(2), kv(3), kv(0), kv(1),
                  pl.BlockSpec((NSA_Q, LANES), lambda b, g, i: (b * nqb + i, g)),
                  pl.BlockSpec((1, LANES), lambda b, g, i: (0, g)),
                  pl.BlockSpec(cmp_pos.shape, lambda b, g, i: (0, 0)),
                  pl.BlockSpec((1, HEAD_DIM), lambda b, g, i: (0, 0)),
                  pl.BlockSpec(e.shape, lambda b, g, i: (0, 0, 0))],
        out_specs=qspec,
        out_shape=jax.ShapeDtypeStruct((n, NSA_HEADS * HEAD_DIM), F32),
        scratch_shapes=[pltpu.VMEM((nbp, HEAD_DIM), F32), pltpu.VMEM((nbp, HEAD_DIM), F32)],
        compiler_params=_params("parallel", "parallel", "arbitrary"),
        name="nsa_prompt",
    )(qn, qr, rows_nsa, rows_nsa, rows_nsa, rows_nsa, rows_win, rows_win,
      gates, gbias, cmp_pos, g_kc, e)


DIL_Q = 128
DIL_K = 2 * DIL_Q
DIL_UNROLL = 4


def _dil_prompt_kernel(q_ref, k_ref, v_ref, o_ref, og_ref, lse_ref):
    t = q_ref.shape[0]
    for gi, (window, dil) in enumerate(DIL_PAIRS):
        n_str = t // dil
        n_qb = n_str // DIL_Q
        span = window // dil

        def body(it, carry, gi=gi, dil=dil, n_qb=n_qb, span=span, n_str=n_str):
            r = it // n_qb
            ib = it % n_qb
            q0 = ib * DIL_Q
            k0 = jnp.clip(q0 - DIL_Q, 0, n_str - DIL_K)
            qrows = pl.ds(r + q0 * dil, DIL_Q, stride=dil) if dil > 1 else pl.ds(q0, DIL_Q)
            krows = pl.ds(r + k0 * dil, DIL_K, stride=dil) if dil > 1 else pl.ds(k0, DIL_K)
            s = _bdot_nt(q_ref[qrows, :], k_ref[krows, :]) * SCALE
            qi = q0 + lax.broadcasted_iota(jnp.int32, (DIL_Q, 1), 0)
            kj = k0 + lax.broadcasted_iota(jnp.int32, (1, DIL_K), 1)
            d = qi - kj
            p, m, den = _softmax_rows(s, (d >= 0) & (d <= span))
            og_ref[gi, qrows, :] = _bdot(p, v_ref[krows, :])
            lse_ref[gi, qrows, :] = jnp.broadcast_to(m + jnp.log(den), (DIL_Q, LANES))
            return carry

        lax.fori_loop(0, dil * n_qb, body, 0, unroll=DIL_UNROLL)

    lse = lse_ref[...]
    mx = jnp.max(lse, axis=0, keepdims=True)
    w = jnp.exp(lse - mx)
    w = w / jnp.sum(w, axis=0, keepdims=True)
    o_ref[...] = jnp.sum(w * og_ref[...], axis=0)


def _dil_prompt(q, rows, batch):
    n, w = q.shape
    t = n // batch
    assert t % (DIL_K * DIL_PAIRS[-1][1]) == 0
    blk = lambda off: pl.BlockSpec((t, HEAD_DIM), lambda b, h: (b, h + off))
    return pl.pallas_call(
        _dil_prompt_kernel,
        grid=(batch, DIL_HEADS),
        in_specs=[blk(0), blk(0), blk(DIL_HEADS)],
        out_specs=blk(0),
        out_shape=jax.ShapeDtypeStruct((n, w), F32),
        scratch_shapes=[pltpu.VMEM((len(DIL_PAIRS), t, HEAD_DIM), F32),
                        pltpu.VMEM((len(DIL_PAIRS), t, LANES), F32)],
        compiler_params=_params("parallel", "parallel"),
        name="dil_prompt",
    )(q, rows, rows)


NSA_ROWS = 4 * NSA_KV_HEADS
MEANS_PAGES = 8


def _page_rows(page_ref, typ, g, pos0, n_pos):
    return page_ref[0, 0, pl.ds(pos0 * NSA_ROWS + typ * NSA_KV_HEADS + g, n_pos, stride=NSA_ROWS), :]


def _page_means_kernel(pt_ref, *refs, n_pg):
    del pt_ref
    page_refs, cpos_ref, o_ref = refs[:n_pg], refs[n_pg], refs[n_pg + 1]
    n_blk = page_refs[0].shape[2] // (NSA_ROWS * NSA_BLOCK)
    for k in range(n_pg):
        for j in range(n_blk):
            parts = []
            for typ in range(2):
                for g in range(NSA_KV_HEADS):
                    x = _page_rows(page_refs[k], typ, g, j * NSA_BLOCK, NSA_BLOCK)
                    if typ == 0:
                        x = x + cpos_ref[...]
                    parts.append(jnp.mean(x, axis=0, keepdims=True))
            o_ref[0, k * n_blk + j:k * n_blk + j + 1, :] = jnp.concatenate(parts, axis=1)


def _page_means(cache, layer, page_table, cmp_pos):
    b, n_pages = page_table.shape
    rows = cache.shape[2]
    n_blk = rows // (NSA_ROWS * NSA_BLOCK)
    n_pg = math.gcd(MEANS_PAGES, n_pages)
    w = 2 * NSA_KV_HEADS * HEAD_DIM
    page_spec = lambda k: pl.BlockSpec(
        (1, 1, rows, LANES), lambda bi, p, pt: (layer, pt[bi, p * n_pg + k], 0, 0))
    grid_spec = pltpu.PrefetchScalarGridSpec(
        num_scalar_prefetch=1,
        grid=(b, n_pages // n_pg),
        in_specs=[page_spec(k) for k in range(n_pg)]
        + [pl.BlockSpec(cmp_pos.shape, lambda bi, p, pt: (0, 0))],
        out_specs=pl.BlockSpec((1, n_pg * n_blk, w), lambda bi, p, pt: (bi, p, 0)),
    )
    return pl.pallas_call(
        functools.partial(_page_means_kernel, n_pg=n_pg),
        grid_spec=grid_spec,
        out_shape=jax.ShapeDtypeStruct((b, n_pages * n_blk, w), F32),
        compiler_params=_params("parallel", "arbitrary"),
        name="page_means",
    )(page_table, *([cache] * n_pg), cmp_pos)


def _nsa_cmp_sample_kernel(qn_ref, means_ref, gkc_ref, oc_ref, sel_ref, *, pos0):
    tq = qn_ref.shape[0]
    nbp = means_ref.shape[1]
    qpos1 = pos0 + lax.broadcasted_iota(jnp.int32, (tq, 1), 0)
    qpos = jnp.concatenate([qpos1] * NSA_GROUP, axis=0)
    blk_i = lax.broadcasted_iota(jnp.int32, (1, nbp), 1)
    blk = blk_i.astype(F32)
    gw = NSA_GROUP * HEAD_DIM
    half = NSA_KV_HEADS * HEAD_DIM
    for g in range(NSA_KV_HEADS):
        qn = _stack_heads(qn_ref[:, g * gw:(g + 1) * gw], NSA_GROUP)
        kc = _head_rms(means_ref[0, :, g * HEAD_DIM:(g + 1) * HEAD_DIM], gkc_ref[...])
        vc = means_ref[0, :, half + g * HEAD_DIM:half + (g + 1) * HEAD_DIM]
        s = _bdot_nt(qn, kc) * SCALE
        p_c, _, _ = _softmax_rows(s, (blk_i + 1) * NSA_BLOCK - 1 <= qpos)
        o_c = _bdot(p_c, vc)
        for r in range(NSA_GROUP):
            oc_ref[:, g * gw + r * HEAD_DIM:g * gw + (r + 1) * HEAD_DIM] = o_c[r * tq:(r + 1) * tq]
        imp = _importance(p_c, tq, qpos1, blk)
        sel_ref[0, g * tq:(g + 1) * tq, :] = _top_n_mask(imp, blk, NSA_TOP_N)


def _nsa_cmp_sample(qn, means, g_kc, batch, pos0):
    n, w = qn.shape
    tq = n // batch
    nbp = means.shape[1]
    return pl.pallas_call(
        functools.partial(_nsa_cmp_sample_kernel, pos0=pos0),
        grid=(batch,),
        in_specs=[pl.BlockSpec((tq, w), lambda b: (b, 0)),
                  pl.BlockSpec((1, nbp, means.shape[2]), lambda b: (b, 0, 0)),
                  pl.BlockSpec((1, HEAD_DIM), lambda b: (0, 0))],
        out_specs=[pl.BlockSpec((tq, w), lambda b: (b, 0)),
                   pl.BlockSpec((1, NSA_KV_HEADS * tq, nbp), lambda b: (b, 0, 0))],
        out_shape=[jax.ShapeDtypeStruct((n, w), F32),
                   jax.ShapeDtypeStruct((batch, NSA_KV_HEADS * tq, nbp), F32)],
        compiler_params=_params("parallel"),
        name="nsa_cmp_sample",
    )(qn, means, g_kc)


def _block_diag_queries(q, n_kv, reps):
    parts = []
    for g in range(n_kv):
        for r in range(reps):
            hq = q[:, (g * reps + r) * HEAD_DIM:(g * reps + r + 1) * HEAD_DIM]
            parts.append(jnp.concatenate(
                [hq if gg == g else jnp.zeros_like(hq) for gg in range(n_kv)], axis=1))
    return jnp.concatenate(parts, axis=0).astype(BF16)


def _rep_rows(x, n_kv, per):
    return jnp.concatenate(
        [jnp.broadcast_to(x[g:g + 1, :], (per, x.shape[1])) for g in range(n_kv)], axis=0)


def _online_update(s, mask, v, m_ref, l_ref, acc_ref):
    if mask is not None:
        s = jnp.where(mask, s, NEG)
    m_old = m_ref[...]
    m_new = jnp.maximum(m_old, jnp.max(s, axis=-1, keepdims=True))
    pr = jnp.exp(s - m_new)
    if mask is not None:
        pr = jnp.where(mask, pr, 0.0)
    alpha = jnp.exp(m_old - m_new)
    l_ref[...] = alpha * l_ref[...] + jnp.sum(pr, axis=-1, keepdims=True)
    acc_ref[...] = alpha * acc_ref[...] + _bdot(pr, v)
    m_ref[...] = m_new


def _pad_rows(x, rows):
    return jnp.concatenate([x, jnp.zeros((rows - x.shape[0], x.shape[1]), x.dtype)], axis=0)


def _write_diag(o_ref, l_ref, acc_ref, n_kv, reps, tq):
    for g in range(n_kv):
        for r in range(reps):
            rs = slice((g * reps + r) * tq, (g * reps + r + 1) * tq)
            o = acc_ref[rs, g * HEAD_DIM:(g + 1) * HEAD_DIM] / jnp.maximum(l_ref[rs, :], 1e-30)
            o_ref[:, (g * reps + r) * HEAD_DIM:(g * reps + r + 1) * HEAD_DIM] = o


FOX_PAGES = 4
HEAD_SHIFT = 3


def _fox_sample_kernel(pt_ref, q_ref, *refs, n_pg):
    del pt_ref
    page_refs, lf_refs = refs[:n_pg], refs[n_pg:2 * n_pg]
    new_ref, lfn_ref, o_ref, cq_ref, carry_ref, m_ref, l_ref, acc_ref = refs[2 * n_pg:]
    p = pl.program_id(1)
    tq = q_ref.shape[0]
    page = page_refs[0].shape[2]
    rows = FOX_HEADS * tq
    row_h = jnp.concatenate([jnp.full((tq, 1), h, jnp.int32) for h in range(FOX_HEADS)], axis=0)
    q_all = _stack_heads(q_ref[...], FOX_HEADS).astype(BF16)
    pr_i = lax.broadcasted_iota(jnp.int32, (LANES, LANES), 0)
    pc_i = lax.broadcasted_iota(jnp.int32, (LANES, LANES), 1)

    @pl.when(p == 0)
    def _():
        carry_ref[...] = jnp.zeros_like(carry_ref)
        lane = lax.broadcasted_iota(jnp.int32, (1, LANES), 1)
        cnew = _exact_dot(lfn_ref[0], (pr_i <= pc_i).astype(BF16))
        cnew_rows = _rep_rows(cnew, FOX_HEADS, tq)
        row_t = _token_of_row(FOX_HEADS, tq)
        cq = jnp.sum(jnp.where(lane == row_t, cnew_rows, 0.0), axis=-1, keepdims=True)
        cq_ref[...] = cq
        kw = FOX_HEADS * HEAD_DIM
        qbd = _block_diag_queries(q_ref[...], FOX_HEADS, 1)
        kn = _pad_rows(new_ref[:, 0:kw], LANES)
        vn = _pad_rows(new_ref[:, kw:2 * kw], LANES)
        mask = lane <= row_t
        s = jnp.where(mask, _bdot_nt(qbd, kn) * SCALE + (cq - cnew_rows), NEG)
        m = jnp.max(s, axis=-1, keepdims=True)
        pr = jnp.where(mask, jnp.exp(s - m), 0.0)
        o_full = _bdot(pr, vn)
        m_ref[...] = m
        l_ref[...] = jnp.sum(pr, axis=-1, keepdims=True)
        acc_ref[...] = jnp.concatenate(
            [o_full[h * tq:(h + 1) * tq, h * HEAD_DIM:(h + 1) * HEAD_DIM] for h in range(FOX_HEADS)],
            axis=0)

    same_head = (pr_i & (FOX_HEADS - 1)) == (pc_i & (FOX_HEADS - 1))
    later_in_row = (same_head & ((pr_i >> HEAD_SHIFT) > (pc_i >> HEAD_SHIFT))).astype(BF16)
    row_total = same_head.astype(BF16)
    n_r = page * FOX_HEADS // LANES
    lane_k = lax.broadcasted_iota(jnp.int32, (1, n_pg * page * FOX_HEADS), 1)
    head_ok = (lane_k & (FOX_HEADS - 1)) == row_h
    cq = cq_ref[...]
    carry = carry_ref[...]
    s_parts, v_parts = [], []
    for k in range(n_pg):
        lf = lf_refs[k][0, 0]
        within = _exact_dot(lf, later_in_row)
        tot = _exact_dot(lf, row_total)
        run = carry
        later = [None] * n_r
        for r in reversed(range(n_r)):
            later[r] = run
            run = run + tot[r:r + 1, :]
        carry = run
        suffix = within + jnp.concatenate(later, axis=0)
        xk = page_refs[k][0, 0, :, 0].reshape(page * FOX_HEADS, HEAD_DIM)
        s_all = _bdot_nt(q_all, xk) * SCALE
        s_parts += [s_all[:, r * LANES:(r + 1) * LANES] + (cq + suffix[r:r + 1, :]) for r in range(n_r)]
        v_parts.append(page_refs[k][0, 0, :, 1].reshape(page * FOX_HEADS, HEAD_DIM))
    carry_ref[...] = carry
    s = jnp.concatenate(s_parts, axis=1)
    _online_update(s, head_ok, jnp.concatenate(v_parts, axis=0), m_ref, l_ref, acc_ref)

    @pl.when(p == pl.num_programs(1) - 1)
    def _():
        o = acc_ref[...] / jnp.maximum(l_ref[...], 1e-30)
        for h in range(FOX_HEADS):
            o_ref[:, h * HEAD_DIM:(h + 1) * HEAD_DIM] = o[h * tq:(h + 1) * tq]


def _fox_sample(qb, cache_kv, cache_lf, layer, page_table, rows_new, lf_new_t):
    n, w = qb.shape
    b, n_pages = page_table.shape
    tq = n // b
    page = cache_kv.shape[2]
    rows = FOX_HEADS * tq
    n_pg = math.gcd(FOX_PAGES, n_pages)
    assert tq <= LANES and (page * FOX_HEADS) % LANES == 0
    pg6 = lambda k: (lambda bi, p, pt: (layer, pt[bi, n_pages - 1 - (p * n_pg + k)], 0, 0, 0, 0))
    pg4 = lambda k: (lambda bi, p, pt: (layer, pt[bi, n_pages - 1 - (p * n_pg + k)], 0, 0))
    grid_spec = pltpu.PrefetchScalarGridSpec(
        num_scalar_prefetch=1,
        grid=(b, n_pages // n_pg),
        in_specs=[pl.BlockSpec((tq, w), lambda bi, p, pt: (bi, 0))]
        + [pl.BlockSpec((1, 1) + cache_kv.shape[2:], pg6(k)) for k in range(n_pg)]
        + [pl.BlockSpec((1, 1) + cache_lf.shape[2:], pg4(k)) for k in range(n_pg)]
        + [pl.BlockSpec((tq, 2 * w), lambda bi, p, pt: (bi, 0)),
           pl.BlockSpec((1, FOX_HEADS, LANES), lambda bi, p, pt: (bi, 0, 0))],
        out_specs=pl.BlockSpec((tq, w), lambda bi, p, pt: (bi, 0)),
        scratch_shapes=[pltpu.VMEM((rows, 1), F32), pltpu.VMEM((1, LANES), F32),
                        pltpu.VMEM((rows, 1), F32), pltpu.VMEM((rows, 1), F32),
                        pltpu.VMEM((rows, HEAD_DIM), F32)],
    )
    return pl.pallas_call(
        functools.partial(_fox_sample_kernel, n_pg=n_pg),
        grid_spec=grid_spec,
        out_shape=jax.ShapeDtypeStruct((n, w), F32),
        compiler_params=_params("parallel", "arbitrary"),
        name="fox_sample",
    )(page_table, qb, *([cache_kv] * n_pg), *([cache_lf] * n_pg), rows_new, lf_new_t)


SEL_PAGES = 4


def _nsa_sel_sample_kernel(pt_ref, q_ref, *refs, n_pg, pos0):
    del pt_ref
    page_refs = refs[:n_pg]
    new_ref, sel_ref, o_ref, qbd_ref, m_ref, l_ref, acc_ref = refs[n_pg:]
    p = pl.program_id(1)
    tq = q_ref.shape[0]
    page = page_refs[0].shape[2] // NSA_ROWS
    kw = NSA_KV_HEADS * HEAD_DIM
    nbp = sel_ref.shape[2]
    selm = sel_ref[0].astype(BF16)

    def key_mask(first_block, n_keys):
        blk_r = lax.broadcasted_iota(jnp.int32, (nbp, n_keys), 0)
        blk_of_lane = lax.broadcasted_iota(jnp.int32, (nbp, n_keys), 1) >> BLOCK_SHIFT
        expand = (blk_r == first_block + blk_of_lane).astype(BF16)
        km = jnp.dot(selm, expand, preferred_element_type=F32)
        return jnp.concatenate(
            [km[g * tq:(g + 1) * tq] for g in range(NSA_KV_HEADS) for _ in range(NSA_GROUP)],
            axis=0) > 0.5

    @pl.when(p == 0)
    def _():
        m_ref[...] = jnp.full_like(m_ref, NEG)
        l_ref[...] = jnp.zeros_like(l_ref)
        acc_ref[...] = jnp.zeros_like(acc_ref)
        qbd = _block_diag_queries(q_ref[...], NSA_KV_HEADS, NSA_GROUP)
        qbd_ref[...] = qbd
        row_t = _token_of_row(NSA_HEADS, tq)
        lane = lax.broadcasted_iota(jnp.int32, (1, LANES), 1)
        kn = _pad_rows(new_ref[:, 0:kw], LANES)
        vn = _pad_rows(new_ref[:, kw:2 * kw], LANES)
        s = _bdot_nt(qbd, kn) * SCALE
        _online_update(s, key_mask(pos0 // NSA_BLOCK, LANES) & (lane <= row_t), vn,
                       m_ref, l_ref, acc_ref)

    def slab(typ):
        return jnp.concatenate(
            [jnp.concatenate([_page_rows(page_refs[k], typ, g, 0, page) for g in range(NSA_KV_HEADS)],
                             axis=1) for k in range(n_pg)], axis=0)

    s = _bdot_nt(qbd_ref[...], slab(2)) * SCALE
    _online_update(s, key_mask(p * (n_pg * page // NSA_BLOCK), n_pg * page), slab(3),
                   m_ref, l_ref, acc_ref)

    @pl.when(p == pl.num_programs(1) - 1)
    def _():
        _write_diag(o_ref, l_ref, acc_ref, NSA_KV_HEADS, NSA_GROUP, tq)


def _nsa_sel_sample(qr, cache, layer, page_table, rows_new, sel, pos0):
    n, w = qr.shape
    b, n_pages = page_table.shape
    tq = n // b
    rows_pp = cache.shape[2]
    page = rows_pp // NSA_ROWS
    kvw = 2 * NSA_KV_HEADS * HEAD_DIM
    rows = NSA_HEADS * tq
    n_pg = math.gcd(SEL_PAGES, n_pages)
    assert pos0 == n_pages * page and pos0 % NSA_BLOCK == 0 and tq <= NSA_BLOCK
    page_spec = lambda k: pl.BlockSpec(
        (1, 1, rows_pp, LANES), lambda bi, p, pt: (layer, pt[bi, p * n_pg + k], 0, 0))
    grid_spec = pltpu.PrefetchScalarGridSpec(
        num_scalar_prefetch=1,
        grid=(b, n_pages // n_pg),
        in_specs=[pl.BlockSpec((tq, w), lambda bi, p, pt: (bi, 0))]
        + [page_spec(k) for k in range(n_pg)]
        + [pl.BlockSpec((tq, kvw), lambda bi, p, pt: (bi, 1)),
           pl.BlockSpec((1,) + sel.shape[1:], lambda bi, p, pt: (bi, 0, 0))],
        out_specs=pl.BlockSpec((tq, w), lambda bi, p, pt: (bi, 0)),
        scratch_shapes=[pltpu.VMEM((rows, kvw // 2), BF16), pltpu.VMEM((rows, 1), F32),
                        pltpu.VMEM((rows, 1), F32), pltpu.VMEM((rows, kvw // 2), F32)],
    )
    return pl.pallas_call(
        functools.partial(_nsa_sel_sample_kernel, n_pg=n_pg, pos0=pos0),
        grid_spec=grid_spec,
        out_shape=jax.ShapeDtypeStruct((n, w), F32),
        compiler_params=_params("parallel", "arbitrary"),
        name="nsa_sel_sample",
    )(page_table, qr, *([cache] * n_pg), rows_new, sel)


def _nsa_win_sample_kernel(qr_ref, win_ref, new_ref, oc_ref, os_ref, gate_ref, gbias_ref,
                           o_ref, nwin_ref, kv_ref):
    tq = qr_ref.shape[0]
    lw = win_ref.shape[1]
    nk = kv_ref.shape[0]
    kw = NSA_KV_HEADS * HEAD_DIM
    gw = NSA_GROUP * HEAD_DIM
    kv_ref[0:lw, :] = win_ref[0]
    kv_ref[lw:nk, :] = _pad_rows(new_ref[...], nk - lw)
    nwin_ref[0, 0:lw - tq, :] = win_ref[0, tq:lw, :]
    nwin_ref[0, lw - tq:lw, :] = new_ref[...]

    idx = lax.broadcasted_iota(jnp.int32, (1, nk), 1)
    row_t = _token_of_row(NSA_GROUP, tq)
    dist = lw + row_t - idx
    mask = (dist >= 0) & (dist < NSA_WINDOW) & (idx < lw + tq)
    gates = _sigmoid(gate_ref[...] + gbias_ref[...])
    for g in range(NSA_KV_HEADS):
        qr = _stack_heads(qr_ref[:, g * gw:(g + 1) * gw], NSA_GROUP)
        s = _bdot_nt(qr, kv_ref[:, g * HEAD_DIM:(g + 1) * HEAD_DIM]) * SCALE
        p_w, _, _ = _softmax_rows(s, mask)
        o_w = _bdot(p_w, kv_ref[:, kw + g * HEAD_DIM:kw + (g + 1) * HEAD_DIM])
        for r in range(NSA_GROUP):
            cs = slice(g * gw + r * HEAD_DIM, g * gw + (r + 1) * HEAD_DIM)
            gc = g * LANES + 3 * r
            o_ref[:, cs] = (gates[:, gc:gc + 1] * oc_ref[:, cs] + gates[:, gc + 1:gc + 2] * os_ref[:, cs]
                            + gates[:, gc + 2:gc + 3] * o_w[r * tq:(r + 1) * tq])


def _nsa_win_sample(qr, win, rows_win, o_c, o_s, gates, gbias, batch):
    n, w = qr.shape
    tq = n // batch
    lw, ww = win.shape[1], win.shape[2]
    nk = lw + LANES
    assert lw >= NSA_WINDOW and tq % 8 == 0 and tq <= LANES
    row = lambda a: pl.BlockSpec((tq, a.shape[1]), lambda b: (b, 0))
    return pl.pallas_call(
        _nsa_win_sample_kernel,
        grid=(batch,),
        in_specs=[row(qr), pl.BlockSpec((1, lw, ww), lambda b: (b, 0, 0)), row(rows_win),
                  row(o_c), row(o_s), row(gates), pl.BlockSpec(gbias.shape, lambda b: (0, 0))],
        out_specs=[row(qr), pl.BlockSpec((1, lw, ww), lambda b: (b, 0, 0))],
        out_shape=[jax.ShapeDtypeStruct((n, w), F32), jax.ShapeDtypeStruct(win.shape, F32)],
        scratch_shapes=[pltpu.VMEM((nk, ww), F32)],
        compiler_params=_params("parallel"),
        name="nsa_win_sample",
    )(qr, win, rows_win, o_c, o_s, gates, gbias)


DIL_CHUNK = 256
DIL_HEAD_SHIFT = 4


def _dil_sample_kernel(q_ref, st_ref, new_ref, o_ref, m_ref, l_ref, acc_ref, *, lw):
    c = pl.program_id(1)
    n_chunks = pl.num_programs(1) - 1
    tq = q_ref.shape[0]
    rows = DIL_HEADS * tq
    n_g = len(DIL_PAIRS)
    q_all = _stack_heads(q_ref[...], DIL_HEADS).astype(BF16)
    row_t = _token_of_row(DIL_HEADS, tq)
    row_h = jnp.concatenate([jnp.full((tq, 1), h, jnp.int32) for h in range(DIL_HEADS)], axis=0)

    @pl.when(c == 0)
    def _():
        m_ref[...] = jnp.full_like(m_ref, NEG)
        l_ref[...] = jnp.zeros_like(l_ref)
        acc_ref[...] = jnp.zeros_like(acc_ref)

    def process(xk, xv, j0):
        n_l = xk.shape[0]
        s = _bdot_nt(q_all, xk) * SCALE
        lane = lax.broadcasted_iota(jnp.int32, (1, n_l), 1)
        head_ok = (lane & (DIL_HEADS - 1)) == row_h
        off = lw + row_t - (j0 + (lane >> DIL_HEAD_SHIFT))
        probs = []
        for gi, (window, dil) in enumerate(DIL_PAIRS):
            mask = head_ok & (off >= 0) & (off <= window) & ((off & (dil - 1)) == 0)
            sm = jnp.where(mask, s, NEG)
            m_old = m_ref[gi]
            m_new = jnp.maximum(m_old, jnp.max(sm, axis=-1, keepdims=True))
            pr = jnp.where(mask, jnp.exp(sm - m_new), 0.0)
            alpha = jnp.exp(m_old - m_new)
            l_ref[gi] = alpha * l_ref[gi] + jnp.sum(pr, axis=-1, keepdims=True)
            acc_ref[gi] = alpha * acc_ref[gi]
            m_ref[gi] = m_new
            probs.append(pr)
        pv = _bdot(jnp.concatenate(probs, axis=0), xv)
        for gi in range(n_g):
            acc_ref[gi] = acc_ref[gi] + pv[gi * rows:(gi + 1) * rows]

    @pl.when(c < n_chunks)
    def _():
        n_pos = st_ref.shape[2]
        process(st_ref[0, 0, :, 0].reshape(n_pos * DIL_HEADS, HEAD_DIM),
                st_ref[0, 0, :, 1].reshape(n_pos * DIL_HEADS, HEAD_DIM), c * n_pos)

    @pl.when(c == n_chunks)
    def _():
        process(new_ref[0, :, 0].reshape(tq * DIL_HEADS, HEAD_DIM),
                new_ref[0, :, 1].reshape(tq * DIL_HEADS, HEAD_DIM), lw)
        lses = [m_ref[gi] + jnp.log(l_ref[gi]) for gi in range(n_g)]
        mx = jnp.maximum(jnp.maximum(lses[0], lses[1]), lses[2])
        ws = [jnp.exp(l - mx) for l in lses]
        tot = ws[0] + ws[1] + ws[2]
        o = sum((ws[gi] / tot) * (acc_ref[gi] / l_ref[gi]) for gi in range(n_g))
        for h in range(DIL_HEADS):
            o_ref[:, h * HEAD_DIM:(h + 1) * HEAD_DIM] = o[h * tq:(h + 1) * tq]


def _dil_sample(q, state, layer, rows_new, batch):
    n, w = q.shape
    tq = n // batch
    lw = state.shape[2]
    assert lw >= DIL_PAIRS[-1][0] and lw % DIL_CHUNK == 0
    n_chunks = lw // DIL_CHUNK
    rows = DIL_HEADS * tq
    n_g = len(DIL_PAIRS)
    return pl.pallas_call(
        functools.partial(_dil_sample_kernel, lw=lw),
        grid=(batch, n_chunks + 1),
        in_specs=[pl.BlockSpec((tq, w), lambda b, c: (b, 0)),
                  pl.BlockSpec((1, 1, DIL_CHUNK) + state.shape[3:],
                               lambda b, c: (layer, b, jnp.minimum(c, n_chunks - 1), 0, 0, 0)),
                  pl.BlockSpec((1,) + rows_new.shape[1:], lambda b, c: (b, 0, 0, 0, 0))],
        out_specs=pl.BlockSpec((tq, w), lambda b, c: (b, 0)),
        out_shape=jax.ShapeDtypeStruct((n, w), F32),
        scratch_shapes=[pltpu.VMEM((n_g, rows, 1), F32), pltpu.VMEM((n_g, rows, 1), F32),
                        pltpu.VMEM((n_g, rows, HEAD_DIM), F32)],
        compiler_params=_params("parallel", "arbitrary"),
        name="dil_sample",
    )(q, state, rows_new)


def _state_shift_kernel(*refs):
    st_ref, new_ref, out_ref = refs[-3:]
    lw, tq = st_ref.shape[0], new_ref.shape[0]
    out_ref[0:lw - tq] = st_ref[tq:lw]
    out_ref[lw - tq:lw] = new_ref[...]


def _state_shift(state, layer, rows_new, prev_out):
    n_l, b, lw, n_s = state.shape[:4]
    tq = rows_new.shape[1]
    tile = state.shape[4:]
    st_spec = lambda: pl.BlockSpec((None, None, lw, None) + tile, lambda bi, s: (layer, bi, 0, s, 0, 0))
    in_specs = [st_spec(), pl.BlockSpec((None, tq, None) + tile, lambda bi, s: (bi, 0, s, 0, 0))]
    args = [state, rows_new]
    aliases = {}
    if prev_out is not None:
        in_specs = [pl.BlockSpec(memory_space=pl.ANY)] + in_specs
        args = [prev_out] + args
        aliases = {0: 0}
    return pl.pallas_call(
        _state_shift_kernel,
        grid=(b, n_s),
        in_specs=in_specs,
        out_specs=st_spec(),
        out_shape=jax.ShapeDtypeStruct(state.shape, F32),
        input_output_aliases=aliases,
        compiler_params=_params("parallel", "arbitrary"),
        name="state_shift",
    )(*args)


def _rope_tables(pos):
    half = HEAD_DIM // 2
    inv = ROPE_THETA ** (-jnp.arange(half, dtype=F32) / half)
    ang = pos.astype(F32)[:, None] * inv[None, :]
    cos, sin = jnp.cos(ang), jnp.sin(ang)
    return jnp.concatenate([cos, cos], axis=-1), jnp.concatenate([-sin, sin], axis=-1)


def _pad_lanes(a, width=LANES):
    return jnp.pad(a, ((0, 0),) * (a.ndim - 1) + ((0, width - a.shape[-1]),))


def _chunks(first, count, op="copy", gi=0):
    return [(first + c, op, gi) for c in range(count)]


NSA_W = NSA_HEADS * HEAD_DIM
NSA_KV_W = NSA_KV_HEADS * HEAD_DIM
FOX_W = FOX_HEADS * HEAD_DIM
DIL_W = DIL_HEADS * HEAD_DIM
GATES_PER_GROUP = 3 * NSA_GROUP


def _even_weights(w_in, w_out, gate_b, forget_b, nsa_g, fox_g):
    cuts = [0]
    for wd in ((NSA_W,) + (NSA_KV_W,) * 6
               + (3 * NSA_HEADS, NSA_W, FOX_W, FOX_W, FOX_W, FOX_HEADS, FOX_W)):
        cuts.append(cuts[-1] + wd)
    seg = lambda a, b: w_in[:, cuts[a]:cuts[b]]
    gate_w, f_w = seg(7, 8), seg(12, 13)
    small = [_pad_lanes(f_w)] + [_pad_lanes(gate_w[:, g * GATES_PER_GROUP:(g + 1) * GATES_PER_GROUP])
                                 for g in range(NSA_KV_HEADS)]
    gbias = jnp.concatenate(
        [_pad_lanes(gate_b[None, g * GATES_PER_GROUP:(g + 1) * GATES_PER_GROUP])
         for g in range(NSA_KV_HEADS)], axis=1)
    aux = jnp.stack([nsa_g[0], nsa_g[2], nsa_g[3], fox_g[0], fox_g[1], _pad_lanes(forget_b),
                     jnp.zeros((LANES,), F32), jnp.zeros((LANES,), F32)])
    return dict(
        w_a=seg(0, 7).astype(BF16),
        w_b=seg(8, 10).astype(BF16),
        w_c=seg(10, 12).astype(BF16),
        w_d=jnp.concatenate([seg(13, 14)] + small, axis=1).astype(BF16),
        w_out_a=w_out[:NSA_W].astype(BF16), w_out_b=w_out[NSA_W:].astype(BF16),
        aux=aux, gbias=gbias, g_kc=nsa_g[1][None, :])


def _even_projections(x, g_norm, wts, pos):
    h = _rmsnorm(x, g_norm)
    cos2, sin2 = _rope_tables(pos)
    nq, nkv = NSA_HEADS, NSA_KV_HEADS
    qn, qr, rows_nsa, rows_win = _proj(
        h, wts["w_a"], wts["aux"], cos2, sin2,
        [[_chunks(0, nq, "rms", 0)], [_chunks(0, nq, "rms_rope", 0)],
         [_chunks(nq, 2 * nkv) + _chunks(nq + 2 * nkv, nkv, "rms_rope", 1) + _chunks(nq + 3 * nkv, nkv)],
         [_chunks(nq + 4 * nkv, nkv, "rms_rope", 2) + _chunks(nq + 5 * nkv, nkv)]],
        wts["w_a"].shape[1])
    z_a, qb = _proj(h, wts["w_b"], wts["aux"], cos2, sin2,
                    [[_chunks(0, nq)], [_chunks(nq, FOX_HEADS, "rms", 3)]], wts["w_b"].shape[1])
    rows_fox, = _proj(h, wts["w_c"], wts["aux"], cos2, sin2,
                      [[_chunks(0, FOX_HEADS, "rms", 4) + _chunks(FOX_HEADS, FOX_HEADS)]],
                      wts["w_c"].shape[1])
    z_b, logf, gates = _proj(
        h, wts["w_d"], wts["aux"], cos2, sin2,
        [[_chunks(0, FOX_HEADS)], [_chunks(FOX_HEADS, 1, "logsig", 5)], [_chunks(FOX_HEADS + 1, nkv)]],
        wts["w_d"].shape[1])
    return qn, qr, rows_nsa, rows_win, z_a, qb, rows_fox, z_b, logf, gates


def _even_prompt(x, batch, g_norm, wts, cmp_pos):
    n = x.shape[0]
    t = n // batch
    pos = jnp.tile(jnp.arange(t, dtype=jnp.int32), batch)
    qn, qr, rows_nsa, rows_win, z_a, qb, rows_fox, z_b, logf, gates = _even_projections(
        x, g_norm, wts, pos)
    c_all = _cumsum_rows(logf, batch)
    ck_t = jnp.transpose(c_all[:, :FOX_HEADS].reshape(batch, t, FOX_HEADS), (0, 2, 1))
    o_b = _fox_prompt(qb, rows_fox, c_all, ck_t, batch)
    o_a = _nsa_prompt(qn, qr, rows_nsa, rows_win, gates, wts["gbias"], cmp_pos, wts["g_kc"], batch)
    y = _outproj(x, [o_a, o_b], [z_a, z_b], [wts["w_out_a"], wts["w_out_b"]])
    lw = min(NSA_WINDOW, t)
    new_win = rows_win.reshape(batch, t, 2, NSA_KV_HEADS, HEAD_DIM)[:, t - lw:]
    return y, (rows_nsa.reshape(batch, t, 4, NSA_KV_HEADS, HEAD_DIM),
               rows_fox.reshape(batch, t, 2, FOX_HEADS, HEAD_DIM),
               logf[:, :FOX_HEADS].reshape(batch, t, FOX_HEADS), new_win)


def _even_sample(x, batch, pos0, g_norm, wts, cmp_pos, layer, cache_nsa, cache_fox, cache_lf,
                 win_state, page_table):
    n = x.shape[0]
    tq = n // batch
    pos = jnp.tile(pos0 + jnp.arange(tq, dtype=jnp.int32), batch)
    qn, qr, rows_nsa, rows_win, z_a, qb, rows_fox, z_b, logf, gates = _even_projections(
        x, g_norm, wts, pos)
    means = _page_means(cache_nsa, layer, page_table, cmp_pos)
    n_blocks = means.shape[1]
    nbp = -(-(n_blocks + 1) // LANES) * LANES
    means = jnp.pad(means, ((0, 0), (0, nbp - n_blocks), (0, 0)))
    o_c, sel = _nsa_cmp_sample(qn, means, wts["g_kc"], batch, pos0)
    o_s = _nsa_sel_sample(qr, cache_nsa, layer, page_table, rows_nsa, sel, pos0)
    lw = win_state.shape[1]
    win = win_state.reshape(batch, lw, 2 * NSA_KV_W)
    o_a, new_win = _nsa_win_sample(qr, win, rows_win, o_c, o_s, gates, wts["gbias"], batch)
    lf_new_t = _pad_lanes(
        jnp.transpose(logf[:, :FOX_HEADS].reshape(batch, tq, FOX_HEADS), (0, 2, 1)))
    o_b = _fox_sample(qb, cache_fox, cache_lf, layer, page_table, rows_fox, lf_new_t)
    y = _outproj(x, [o_a, o_b], [z_a, z_b], [wts["w_out_a"], wts["w_out_b"]])
    return y, (rows_nsa.reshape(batch, tq, 4, NSA_KV_HEADS, HEAD_DIM),
               rows_fox.reshape(batch, tq, 2, FOX_HEADS, HEAD_DIM),
               logf[:, :FOX_HEADS].reshape(batch, tq, FOX_HEADS),
               new_win.reshape(batch, lw, 2, NSA_KV_HEADS, HEAD_DIM))


def _odd_weights(w_in, w_out, qk_g):
    aux = jnp.concatenate([qk_g, jnp.zeros((6, HEAD_DIM), F32)], axis=0)
    return dict(w_q=w_in[:, :DIL_W].astype(BF16), w_kv=w_in[:, DIL_W:3 * DIL_W].astype(BF16),
                w_z=w_in[:, 3 * DIL_W:].astype(BF16), w_out=w_out.astype(BF16), aux=aux)


def _odd_projections(x, g_norm, wts, pos, head_major_rows=False):
    h = _rmsnorm(x, g_norm)
    cos2, sin2 = _rope_tables(pos)
    q, = _proj(h, wts["w_q"], wts["aux"], cos2, sin2, [[_chunks(0, DIL_HEADS, "rms_rope", 0)]], DIL_W)
    rows, = _proj(h, wts["w_kv"], wts["aux"], cos2, sin2,
                  [[_chunks(0, DIL_HEADS, "rms_rope", 1), _chunks(0, DIL_HEADS)]], DIL_W,
                  head_major=[head_major_rows])
    z, = _proj(h, wts["w_z"], wts["aux"], cos2, sin2, [[_chunks(0, DIL_HEADS)]], DIL_W)
    return q, rows, z


def _odd_prompt(x, batch, g_norm, wts):
    n = x.shape[0]
    t = n // batch
    pos = jnp.tile(jnp.arange(t, dtype=jnp.int32), batch)
    q, rows, z = _odd_projections(x, g_norm, wts, pos)
    o = _dil_prompt(q, rows, batch)
    y = _outproj(x, [o], [z], [wts["w_out"]])
    lw = min(DIL_PAIRS[-1][0], t)
    return y, rows.reshape(batch, t, 2, DIL_HEADS, HEAD_DIM)[:, t - lw:]


STATE_TILE = 8


def _odd_sample(x, batch, pos0, g_norm, wts, state_all, layer, prev_out):
    n = x.shape[0]
    tq = n // batch
    pos = jnp.tile(pos0 + jnp.arange(tq, dtype=jnp.int32), batch)
    q, rows, z = _odd_projections(x, g_norm, wts, pos, head_major_rows=True)
    n_l, _, lw = state_all.shape[:3]
    o = _dil_sample(q, state_all, layer, rows.reshape(batch, tq, 2, DIL_HEADS, HEAD_DIM), batch)
    y = _outproj(x, [o], [z], [wts["w_out"]])
    n_s = 2 * DIL_HEADS // STATE_TILE
    new_all = _state_shift(state_all.reshape(n_l, batch, lw, n_s, STATE_TILE, HEAD_DIM), layer,
                           rows.reshape(batch, tq, n_s, STATE_TILE, HEAD_DIM), prev_out)
    return y, new_all


def kernel(x_prompt, x_sample, cache_nsa_kv, cache_fox_kv, cache_fox_logf, state_nsa_win_kv, state_dil_kv, page_table, norm_even, w_in_even, w_out_even, nsa_gate_bias, fox_forget_bias, nsa_cmp_pos, nsa_qk_gain, fox_qk_gain, norm_odd, w_in_odd, w_out_odd, dil_qk_gain):
    bp, t, d = x_prompt.shape
    bs, tq, _ = x_sample.shape
    n_layers, n_pool, page = cache_nsa_kv.shape[:3]
    pos0 = page_table.shape[1] * page
    depth = norm_even.shape[0] + norm_odd.shape[0]
    cache_nsa = cache_nsa_kv.reshape(n_layers, n_pool, page * NSA_ROWS, HEAD_DIM)
    cache_lf = cache_fox_logf.reshape(n_layers, n_pool, page * FOX_HEADS // LANES, LANES)
    yp = x_prompt.reshape(bp * t, d)
    ys = x_sample.reshape(bs * tq, d)
    outs = [[] for _ in range(9)]
    dil_s = None
    for layer in range(depth):
        i = layer // 2
        if layer % 2 == 0:
            wts = _even_weights(w_in_even[i], w_out_even[i], nsa_gate_bias[i], fox_forget_bias[i],
                                nsa_qk_gain[i], fox_qk_gain[i])
            yp, new_p = _even_prompt(yp, bp, norm_even[i], wts, nsa_cmp_pos[i])
            ys, new_s = _even_sample(ys, bs, pos0, norm_even[i], wts, nsa_cmp_pos[i], i, cache_nsa,
                                     cache_fox_kv, cache_lf, state_nsa_win_kv[i], page_table)
            for k in range(4):
                outs[2 * k].append(new_p[k])
                outs[2 * k + 1].append(new_s[k])
        else:
            wts = _odd_weights(w_in_odd[i], w_out_odd[i], dil_qk_gain[i])
            yp, buf_p = _odd_prompt(yp, bp, norm_odd[i], wts)
            ys, dil_s = _odd_sample(ys, bs, pos0, norm_odd[i], wts, state_dil_kv, i, dil_s)
            outs[8].append(buf_p)
    return ((yp.reshape(bp, t, d), ys.reshape(bs, tq, d)) + tuple(jnp.stack(o) for o in outs)
            + (dil_s.reshape(state_dil_kv.shape),))
```

```python
import functools
import math

import jax
import jax.numpy as jnp
from jax import lax
from jax.experimental import pallas as pl
from jax.experimental.pallas import tpu as pltpu

F32 = jnp.float32
BF16 = jnp.bfloat16

HEAD_DIM = 128
LANES = 128
NSA_HEADS = 8
NSA_KV_HEADS = 2
NSA_GROUP = NSA_HEADS // NSA_KV_HEADS
NSA_BLOCK = 64
NSA_TOP_N = 16
NSA_WINDOW = 512
NSA_FORCE_SCORE = 1.0e4
FOX_HEADS = 8
DIL_HEADS = 16
DIL_PAIRS = ((128, 1), (512, 4), (2048, 16))
ROPE_THETA = 10000.0
EPS = 1e-6
SCALE = HEAD_DIM ** -0.5
NEG = -1.0e30
LOG2E = math.log2(math.e)
VMEM_LIMIT = 56 * 1024 * 1024
MATMUL_ROWS = 512


def _params(*sem):
    return pltpu.CompilerParams(dimension_semantics=sem, vmem_limit_bytes=VMEM_LIMIT)


def _bdot(a, b):
    return jnp.dot(a.astype(BF16), b.astype(BF16), preferred_element_type=F32)


def _bdot_nt(a, b):
    return lax.dot_general(a.astype(BF16), b.astype(BF16), (((1,), (1,)), ((), ())),
                           preferred_element_type=F32)


def _split3(x):
    hi = x.astype(BF16)
    r1 = x - hi.astype(F32)
    mid = r1.astype(BF16)
    lo = (r1 - mid.astype(F32)).astype(BF16)
    return hi, mid, lo


def _exact_dot(x, m):
    hi, mid, lo = _split3(x)
    return (jnp.dot(hi, m, preferred_element_type=F32)
            + jnp.dot(mid, m, preferred_element_type=F32)
            + jnp.dot(lo, m, preferred_element_type=F32))


def _exact_dot_left(m, x):
    hi, mid, lo = _split3(x)
    return (jnp.dot(m, hi, preferred_element_type=F32)
            + jnp.dot(m, mid, preferred_element_type=F32)
            + jnp.dot(m, lo, preferred_element_type=F32))


def _token_of_row(n_rep, tq):
    return jnp.concatenate([lax.broadcasted_iota(jnp.int32, (tq, 1), 0)] * n_rep, axis=0)


BLOCK_SHIFT = 6


def _sigmoid(x):
    return 1.0 / (1.0 + jnp.exp(-x))


def _log_sigmoid(x):
    return -(jnp.maximum(-x, 0.0) + jnp.log(1.0 + jnp.exp(-jnp.abs(x))))


def _head_rms(y, g):
    return y * lax.rsqrt(jnp.mean(y * y, axis=-1, keepdims=True) + EPS) * g


def _softmax_rows(s, mask):
    s = jnp.where(mask, s, -jnp.inf)
    m = jnp.max(s, axis=-1, keepdims=True)
    m = jnp.where(m > -jnp.inf, m, 0.0)
    e = jnp.exp(s - m)
    den = jnp.sum(e, axis=-1, keepdims=True)
    return e / jnp.maximum(den, 1e-30), m, den


def _rmsnorm_kernel(x_ref, g_ref, h_ref):
    x = x_ref[...]
    y = x * lax.rsqrt(jnp.mean(x * x, axis=-1, keepdims=True) + EPS)
    h_ref[...] = (y * g_ref[...]).astype(BF16)


def _rmsnorm(x, g):
    n, d = x.shape
    tm = min(n, 512)
    return pl.pallas_call(
        _rmsnorm_kernel,
        grid=(n // tm,),
        in_specs=[pl.BlockSpec((tm, d), lambda i: (i, 0)),
                  pl.BlockSpec((1, d), lambda i: (0, 0))],
        out_specs=pl.BlockSpec((tm, d), lambda i: (i, 0)),
        out_shape=jax.ShapeDtypeStruct((n, d), BF16),
        compiler_params=_params("parallel"),
        name="rmsnorm",
    )(x, g.reshape(1, d))


def _proj_kernel(h_ref, w_ref, aux_ref, cos_ref, sin_ref, *out_refs, plans, n_tiles, head_major):
    acc = jnp.dot(h_ref[...], w_ref[...], preferred_element_type=F32)
    tm = h_ref.shape[0]

    def emit(tile):
        for out_ref, plan, hm in zip(out_refs, plans, head_major):
            n_c = len(plan[tile])
            for c, (src, op, gi) in enumerate(plan[tile]):
                y = acc[:, src * LANES:(src + 1) * LANES]
                if op in ("rms", "rms_rope"):
                    y = _head_rms(y, aux_ref[gi:gi + 1, :])
                if op == "rms_rope":
                    y = y * cos_ref[...] + pltpu.roll(y, HEAD_DIM // 2, 1) * sin_ref[...]
                if op == "logsig":
                    y = _log_sigmoid(y + aux_ref[gi:gi + 1, :])
                if hm:
                    out_ref[pl.ds(tile * n_c + c, tm, stride=n_tiles * n_c), :] = y
                else:
                    out_ref[:, c * LANES:(c + 1) * LANES] = y

    if n_tiles == 1:
        emit(0)
    else:
        for t in range(n_tiles):
            pl.when(pl.program_id(0) == t)(functools.partial(emit, t))


def _proj(h, w, aux, cos2, sin2, plans, tn, head_major=None):
    n, k = h.shape
    n_tiles = w.shape[1] // tn
    tm = min(n, MATMUL_ROWS)
    head_major = head_major or [False] * len(plans)
    out_specs, out_shape = [], []
    for plan, hm in zip(plans, head_major):
        n_c = len(plan[0])
        if hm:
            assert n_tiles == 1 or n == tm
            out_specs.append(pl.BlockSpec((tm * n_tiles * n_c, LANES), lambda j, i: (i, 0)))
            out_shape.append(jax.ShapeDtypeStruct((n * n_tiles * n_c, LANES), F32))
        else:
            out_specs.append(pl.BlockSpec((tm, n_c * LANES), lambda j, i: (i, j)))
            out_shape.append(jax.ShapeDtypeStruct((n, n_tiles * n_c * LANES), F32))
    kern = functools.partial(_proj_kernel, plans=plans, n_tiles=n_tiles, head_major=head_major)
    outs = pl.pallas_call(
        kern,
        grid=(n_tiles, n // tm),
        in_specs=[pl.BlockSpec((tm, k), lambda j, i: (i, 0)),
                  pl.BlockSpec((k, tn), lambda j, i: (0, j), pipeline_mode=pl.Buffered(1)),
                  pl.BlockSpec(aux.shape, lambda j, i: (0, 0)),
                  pl.BlockSpec((tm, LANES), lambda j, i: (i, 0)),
                  pl.BlockSpec((tm, LANES), lambda j, i: (i, 0))],
        out_specs=out_specs,
        out_shape=out_shape,
        compiler_params=_params("arbitrary", "arbitrary"),
        name="proj",
    )(h, w, aux, cos2, sin2)
    return outs


def _outproj_kernel(*refs, n_parts):
    x_ref = refs[0]
    y_ref = refs[-1]
    acc = x_ref[...]
    for p in range(n_parts):
        o = refs[1 + p][...]
        z = refs[1 + n_parts + p][...]
        mix = o * (z * _sigmoid(z))
        acc = acc + jnp.dot(mix.astype(BF16), refs[1 + 2 * n_parts + p][...],
                            preferred_element_type=F32)
    y_ref[...] = acc


def _outproj(x, o_parts, z_parts, w_parts):
    n, d = x.shape
    tm = min(n, MATMUL_ROWS)
    n_parts = len(o_parts)
    row = lambda a: pl.BlockSpec((tm, a.shape[1]), lambda i: (i, 0))
    return pl.pallas_call(
        functools.partial(_outproj_kernel, n_parts=n_parts),
        grid=(n // tm,),
        in_specs=([row(x)] + [row(o) for o in o_parts] + [row(z) for z in z_parts]
                  + [pl.BlockSpec(w.shape, lambda i: (0, 0), pipeline_mode=pl.Buffered(1))
                     for w in w_parts]),
        out_specs=row(x),
        out_shape=jax.ShapeDtypeStruct((n, d), F32),
        compiler_params=_params("parallel"),
        name="outproj",
    )(x, *o_parts, *z_parts, *w_parts)


CUM_BLOCK = 256


def _cumsum_kernel(x_ref, tri_ref, c_ref, carry_ref):
    @pl.when(pl.program_id(1) == 0)
    def _():
        carry_ref[...] = jnp.zeros_like(carry_ref)

    c = _exact_dot_left(tri_ref[...], x_ref[...]) + carry_ref[...]
    c_ref[...] = c
    carry_ref[...] = c[CUM_BLOCK - 1:CUM_BLOCK, :]


def _cumsum_rows(x, batch):
    n, w = x.shape
    t = n // batch
    nb = t // CUM_BLOCK
    r = lax.broadcasted_iota(jnp.int32, (CUM_BLOCK, CUM_BLOCK), 0)
    c = lax.broadcasted_iota(jnp.int32, (CUM_BLOCK, CUM_BLOCK), 1)
    tri = (c <= r).astype(BF16)
    return pl.pallas_call(
        _cumsum_kernel,
        grid=(batch, nb),
        in_specs=[pl.BlockSpec((CUM_BLOCK, w), lambda b, j: (b * nb + j, 0)),
                  pl.BlockSpec((CUM_BLOCK, CUM_BLOCK), lambda b, j: (0, 0))],
        out_specs=pl.BlockSpec((CUM_BLOCK, w), lambda b, j: (b * nb + j, 0)),
        out_shape=jax.ShapeDtypeStruct((n, w), F32),
        scratch_shapes=[pltpu.VMEM((1, w), F32)],
        compiler_params=_params("parallel", "arbitrary"),
        name="logf_cumsum",
    )(x, tri)


FOX_TILE = 512


def _lanes(col):
    return jnp.broadcast_to(col, (col.shape[0], LANES))


def _fox_prompt_kernel(q_ref, k_ref, v_ref, cq_ref, ck_ref, o_ref, m_ref, l_ref, acc_ref, cq2_ref):
    i = pl.program_id(1)
    j = pl.program_id(2)
    tq = q_ref.shape[0]
    tk = k_ref.shape[0]

    @pl.when(j == 0)
    def _():
        m_ref[...] = jnp.full_like(m_ref, NEG)
        l_ref[...] = jnp.zeros_like(l_ref)
        acc_ref[...] = jnp.zeros_like(acc_ref)
        for h in range(FOX_HEADS):
            cq2_ref[h] = _lanes(cq_ref[:, h:h + 1] * LOG2E)

    def step(causal):
        if causal:
            mask = (lax.broadcasted_iota(jnp.int32, (tq, tk), 0)
                    >= lax.broadcasted_iota(jnp.int32, (tq, tk), 1))
        for h in range(FOX_HEADS):
            sl = slice(h * HEAD_DIM, (h + 1) * HEAD_DIM)
            u = _bdot_nt(q_ref[:, sl], k_ref[:, sl]) * (SCALE * LOG2E) - ck_ref[0, h:h + 1, :] * LOG2E
            if causal:
                u = jnp.where(mask, u, NEG)
            cq2 = cq2_ref[h]
            m_old = m_ref[h]
            m_new = jnp.maximum(m_old, _lanes(jnp.max(u, axis=-1, keepdims=True)) + cq2)
            p = jnp.exp2(u - jnp.tile(m_new - cq2, (1, tk // LANES)))
            alpha = jnp.exp2(m_old - m_new)
            l_ref[h] = alpha * l_ref[h] + _lanes(jnp.sum(p, axis=-1, keepdims=True))
            acc_ref[:, sl] = alpha * acc_ref[:, sl] + _bdot(p, v_ref[:, sl])
            m_ref[h] = m_new

    pl.when(j < i)(functools.partial(step, False))
    pl.when(j == i)(functools.partial(step, True))

    @pl.when(j == i)
    def _():
        for h in range(FOX_HEADS):
            sl = slice(h * HEAD_DIM, (h + 1) * HEAD_DIM)
            o_ref[:, sl] = acc_ref[:, sl] / jnp.maximum(l_ref[h], 1e-30)


def _fox_prompt(qb, rows_fox, c_all, ck_t, batch):
    n, w = qb.shape
    t = n // batch
    tq = min(FOX_TILE, t)
    nq = t // tq
    return pl.pallas_call(
        _fox_prompt_kernel,
        grid=(batch, nq, nq),
        in_specs=[pl.BlockSpec((tq, w), lambda b, i, j: (b * nq + i, 0)),
                  pl.BlockSpec((tq, w), lambda b, i, j: (b * nq + jnp.minimum(j, i), 0)),
                  pl.BlockSpec((tq, w), lambda b, i, j: (b * nq + jnp.minimum(j, i), 1)),
                  pl.BlockSpec((tq, LANES), lambda b, i, j: (b * nq + i, 0)),
                  pl.BlockSpec((1, FOX_HEADS, tq), lambda b, i, j: (b, 0, jnp.minimum(j, i)))],
        out_specs=pl.BlockSpec((tq, w), lambda b, i, j: (b * nq + i, 0)),
        out_shape=jax.ShapeDtypeStruct((n, w), F32),
        scratch_shapes=[pltpu.VMEM((FOX_HEADS, tq, LANES), F32),
                        pltpu.VMEM((FOX_HEADS, tq, LANES), F32),
                        pltpu.VMEM((tq, w), F32),
                        pltpu.VMEM((FOX_HEADS, tq, LANES), F32)],
        compiler_params=_params("parallel", "parallel", "arbitrary"),
        name="fox_prompt",
    )(qb, rows_fox, rows_fox, c_all, ck_t)


def _stack_heads(x, n):
    return jnp.concatenate([x[:, r * HEAD_DIM:(r + 1) * HEAD_DIM] for r in range(n)], axis=0)


def _top_n_mask(imp, blk, n_sel):
    work = imp
    selm = jnp.zeros(imp.shape, F32)
    for _ in range(n_sel):
        m = jnp.max(work, axis=-1, keepdims=True)
        idx = jnp.min(jnp.where(work == m, blk, 1.0e9), axis=-1, keepdims=True)
        pick = blk == idx
        selm = jnp.where(pick, 1.0, selm)
        work = jnp.where(pick, -jnp.inf, work)
    return selm


def _importance(p_c, rows, qpos_col, blk):
    imp = p_c[0:rows]
    for r in range(1, NSA_GROUP):
        imp = imp + p_c[r * rows:(r + 1) * rows]
    cur = (qpos_col >> BLOCK_SHIFT).astype(F32)
    forced = (blk == 0.0) | (blk == cur) | (blk == cur - 1.0)
    imp = jnp.where(forced, NSA_FORCE_SCORE, imp)
    return jnp.where(blk <= cur, imp, NEG)


NSA_Q = 128
NSA_TK = 512
NSA_WKEYS = NSA_WINDOW + NSA_Q


def _nsa_prompt_kernel(qn_ref, qr_ref, kc_ref, vc_ref, ks_ref, vs_ref, kw_ref, vw_ref,
                       gate_ref, gbias_ref, cpos_ref, gkc_ref, e_ref, o_ref,
                       kcn_ref, vcm_ref):
    i = pl.program_id(2)
    t = kc_ref.shape[0]
    nb = t // NSA_BLOCK
    nbp = kcn_ref.shape[0]

    @pl.when(i == 0)
    def _():
        kcn_ref[...] = jnp.zeros_like(kcn_ref)
        vcm_ref[...] = jnp.zeros_like(vcm_ref)

        def body(jb, carry):
            rows = pl.ds(pl.multiple_of(jb * NSA_BLOCK, NSA_BLOCK), NSA_BLOCK)
            km = jnp.mean(kc_ref[rows, :] + cpos_ref[...], axis=0, keepdims=True)
            kcn_ref[pl.ds(jb, 1), :] = _head_rms(km, gkc_ref[...])
            vcm_ref[pl.ds(jb, 1), :] = jnp.mean(vc_ref[rows, :], axis=0, keepdims=True)
            return carry

        lax.fori_loop(0, nb, body, 0)

    rq = NSA_GROUP * NSA_Q
    qn = _stack_heads(qn_ref[...], NSA_GROUP)
    qr = _stack_heads(qr_ref[...], NSA_GROUP).astype(BF16)
    qpos1 = i * NSA_Q + lax.broadcasted_iota(jnp.int32, (NSA_Q, 1), 0)
    qpos = jnp.concatenate([qpos1] * NSA_GROUP, axis=0)

    blk_i = lax.broadcasted_iota(jnp.int32, (1, nbp), 1)
    s = _bdot_nt(qn, kcn_ref[...]) * SCALE
    p_c, _, _ = _softmax_rows(s, (blk_i + 1) * NSA_BLOCK - 1 <= qpos)
    o_c = _bdot(p_c, vcm_ref[...])

    blk = blk_i.astype(F32)
    imp = _importance(p_c, NSA_Q, qpos1, blk)
    selm = _top_n_mask(imp, blk, min(NSA_TOP_N, nb)).astype(BF16)

    def sel_chunk(c, carry, causal):
        m_old, l_old, acc = carry
        rows = pl.ds(pl.multiple_of(c * NSA_TK, NSA_TK), NSA_TK)
        kmask = jnp.dot(selm, e_ref[c], preferred_element_type=F32)
        bias = (kmask - 1.0) * (-NEG)
        if causal:
            kpos = c * NSA_TK + lax.broadcasted_iota(jnp.int32, (1, NSA_TK), 1)
            bias = jnp.where(kpos <= qpos1, bias, NEG)
        sc = _bdot_nt(qr, ks_ref[rows, :]) * SCALE + jnp.concatenate([bias] * NSA_GROUP, axis=0)
        m_new = jnp.maximum(m_old, _lanes(jnp.max(sc, axis=-1, keepdims=True)))
        p = jnp.exp(sc - jnp.tile(m_new, (1, NSA_TK // LANES)))
        alpha = jnp.exp(m_old - m_new)
        l_new = alpha * l_old + _lanes(jnp.sum(p, axis=-1, keepdims=True))
        acc = alpha * acc + _bdot(p, vs_ref[rows, :])
        return m_new, l_new, acc

    last = (i * NSA_Q) // NSA_TK
    init = (jnp.full((rq, LANES), NEG, F32), jnp.zeros((rq, LANES), F32),
            jnp.zeros((rq, HEAD_DIM), F32))
    carry = lax.fori_loop(0, last, functools.partial(sel_chunk, causal=False), init)
    _, l_s, acc_s = sel_chunk(last, carry, True)
    o_s = acc_s / jnp.maximum(l_s, 1e-30)

    k0 = jnp.clip(i * NSA_Q - NSA_WINDOW, 0, t - NSA_WKEYS)
    k0 = pl.multiple_of(k0, NSA_Q)
    wrows = pl.ds(k0, NSA_WKEYS)
    kpos = k0 + lax.broadcasted_iota(jnp.int32, (1, NSA_WKEYS), 1)
    dist = qpos - kpos
    sw = _bdot_nt(qr, kw_ref[wrows, :]) * SCALE
    p_w, _, _ = _softmax_rows(sw, (dist >= 0) & (dist < NSA_WINDOW))
    o_w = _bdot(p_w, vw_ref[wrows, :])

    gates = _sigmoid(gate_ref[...] + gbias_ref[...])
    for r in range(NSA_GROUP):
        rs = slice(r * NSA_Q, (r + 1) * NSA_Q)
        o = (gates[:, 3 * r:3 * r + 1] * o_c[rs] + gates[:, 3 * r + 1:3 * r + 2] * o_s[rs]
             + gates[:, 3 * r + 2:3 * r + 3] * o_w[rs])
        o_ref[:, r * HEAD_DIM:(r + 1) * HEAD_DIM] = o


def _expand_matrix(nbp, n_keys, tk):
    kb = (jnp.arange(n_keys, dtype=jnp.int32) // NSA_BLOCK).reshape(n_keys // tk, 1, tk)
    jb = jnp.arange(nbp, dtype=jnp.int32).reshape(1, nbp, 1)
    return (kb == jb).astype(BF16)


def _nsa_prompt(qn, qr, rows_nsa, rows_win, gates, gbias, cmp_pos, g_kc, batch):
    n = qn.shape[0]
    t = n // batch
    assert t % NSA_TK == 0 and t >= NSA_WKEYS
    nqb = t // NSA_Q
    nb = t // NSA_BLOCK
    nbp = -(-nb // LANES) * LANES
    e = _expand_matrix(nbp, t, NSA_TK)
    gw = NSA_GROUP * HEAD_DIM
    qspec = pl.BlockSpec((NSA_Q, gw), lambda b, g, i: (b * nqb + i, g))
    kv = lambda typ: pl.BlockSpec((t, HEAD_DIM), lambda b, g, i: (b, typ * NSA_KV_HEADS + g))
    return pl.pallas_call(
        _nsa_prompt_kernel,
        grid=(batch, NSA_KV_HEADS, nqb),
        in_specs=[qspec, qspec, kv(0), kv(1), kv(2), kv(3), kv(0), kv(1),
                  pl.BlockSpec((NSA_Q, LANES), lambda b, g, i: (b * nqb + i, g)),
                  pl.BlockSpec((1, LANES), lambda b, g, i: (0, g)),
                  pl.BlockSpec(cmp_pos.shape, lambda b, g, i: (0, 0)),
                  pl.BlockSpec((1, HEAD_DIM), lambda b, g, i: (0, 0)),
                  pl.BlockSpec(e.shape, lambda b, g, i: (0, 0, 0))],
        out_specs=qspec,
        out_shape=jax.ShapeDtypeStruct((n, NSA_HEADS * HEAD_DIM), F32),
        scratch_shapes=[pltpu.VMEM((nbp, HEAD_DIM), F32), pltpu.VMEM((nbp, HEAD_DIM), F32)],
        compiler_params=_params("parallel", "parallel", "arbitrary"),
        name="nsa_prompt",
    )(qn, qr, rows_nsa, rows_nsa, rows_nsa, rows_nsa, rows_win, rows_win,
      gates, gbias, cmp_pos, g_kc, e)


DIL_Q = 128
DIL_K = 2 * DIL_Q
DIL_UNROLL = 4


def _dil_prompt_kernel(q_ref, k_ref, v_ref, o_ref, og_ref, lse_ref):
    t = q_ref.shape[0]
    for gi, (window, dil) in enumerate(DIL_PAIRS):
        n_str = t // dil
        n_qb = n_str // DIL_Q
        span = window // dil

        def body(it, carry, gi=gi, dil=dil, n_qb=n_qb, span=span, n_str=n_str):
            r = it // n_qb
            ib = it % n_qb
            q0 = ib * DIL_Q
            k0 = jnp.clip(q0 - DIL_Q, 0, n_str - DIL_K)
            qrows = pl.ds(r + q0 * dil, DIL_Q, stride=dil) if dil > 1 else pl.ds(q0, DIL_Q)
            krows = pl.ds(r + k0 * dil, DIL_K, stride=dil) if dil > 1 else pl.ds(k0, DIL_K)
            s = _bdot_nt(q_ref[qrows, :], k_ref[krows, :]) * SCALE
            qi = q0 + lax.broadcasted_iota(jnp.int32, (DIL_Q, 1), 0)
            kj = k0 + lax.broadcasted_iota(jnp.int32, (1, DIL_K), 1)
            d = qi - kj
            p, m, den = _softmax_rows(s, (d >= 0) & (d <= span))
            og_ref[gi, qrows, :] = _bdot(p, v_ref[krows, :])
            lse_ref[gi, qrows, :] = jnp.broadcast_to(m + jnp.log(den), (DIL_Q, LANES))
            return carry

        lax.fori_loop(0, dil * n_qb, body, 0, unroll=DIL_UNROLL)

    lse = lse_ref[...]
    mx = jnp.max(lse, axis=0, keepdims=True)
    w = jnp.exp(lse - mx)
    w = w / jnp.sum(w, axis=0, keepdims=True)
    o_ref[...] = jnp.sum(w * og_ref[...], axis=0)


def _dil_prompt(q, rows, batch):
    n, w = q.shape
    t = n // batch
    assert t % (DIL_K * DIL_PAIRS[-1][1]) == 0
    blk = lambda off: pl.BlockSpec((t, HEAD_DIM), lambda b, h: (b, h + off))
    return pl.pallas_call(
        _dil_prompt_kernel,
        grid=(batch, DIL_HEADS),
        in_specs=[blk(0), blk(0), blk(DIL_HEADS)],
        out_specs=blk(0),
        out_shape=jax.ShapeDtypeStruct((n, w), F32),
        scratch_shapes=[pltpu.VMEM((len(DIL_PAIRS), t, HEAD_DIM), F32),
                        pltpu.VMEM((len(DIL_PAIRS), t, LANES), F32)],
        compiler_params=_params("parallel", "parallel"),
        name="dil_prompt",
    )(q, rows, rows)


NSA_ROWS = 4 * NSA_KV_HEADS
MEANS_PAGES = 16


def _page_rows(page_ref, typ, g, pos0, n_pos):
    return page_ref[0, 0, pl.ds(pos0 * NSA_ROWS + typ * NSA_KV_HEADS + g, n_pos, stride=NSA_ROWS), :]


def _page_means_kernel(pt_ref, *refs, n_pg):
    del pt_ref
    page_refs, cpos_ref, o_ref = refs[:n_pg], refs[n_pg], refs[n_pg + 1]
    n_blk = page_refs[0].shape[2] // (NSA_ROWS * NSA_BLOCK)
    for k in range(n_pg):
        for j in range(n_blk):
            parts = []
            for typ in range(2):
                for g in range(NSA_KV_HEADS):
                    x = _page_rows(page_refs[k], typ, g, j * NSA_BLOCK, NSA_BLOCK)
                    if typ == 0:
                        x = x + cpos_ref[...]
                    parts.append(jnp.mean(x, axis=0, keepdims=True))
            o_ref[0, k * n_blk + j:k * n_blk + j + 1, :] = jnp.concatenate(parts, axis=1)


def _page_means(cache, layer, page_table, cmp_pos):
    b, n_pages = page_table.shape
    rows = cache.shape[2]
    n_blk = rows // (NSA_ROWS * NSA_BLOCK)
    n_pg = math.gcd(MEANS_PAGES, n_pages)
    w = 2 * NSA_KV_HEADS * HEAD_DIM
    page_spec = lambda k: pl.BlockSpec(
        (1, 1, rows, LANES), lambda bi, p, pt: (layer, pt[bi, p * n_pg + k], 0, 0))
    grid_spec = pltpu.PrefetchScalarGridSpec(
        num_scalar_prefetch=1,
        grid=(b, n_pages // n_pg),
        in_specs=[page_spec(k) for k in range(n_pg)]
        + [pl.BlockSpec(cmp_pos.shape, lambda bi, p, pt: (0, 0))],
        out_specs=pl.BlockSpec((1, n_pg * n_blk, w), lambda bi, p, pt: (bi, p, 0)),
    )
    return pl.pallas_call(
        functools.partial(_page_means_kernel, n_pg=n_pg),
        grid_spec=grid_spec,
        out_shape=jax.ShapeDtypeStruct((b, n_pages * n_blk, w), F32),
        compiler_params=_params("parallel", "arbitrary"),
        name="page_means",
    )(page_table, *([cache] * n_pg), cmp_pos)


def _nsa_cmp_sample_kernel(qn_ref, means_ref, gkc_ref, oc_ref, sel_ref, *, pos0):
    tq = qn_ref.shape[0]
    nbp = means_ref.shape[1]
    qpos1 = pos0 + lax.broadcasted_iota(jnp.int32, (tq, 1), 0)
    qpos = jnp.concatenate([qpos1] * NSA_GROUP, axis=0)
    blk_i = lax.broadcasted_iota(jnp.int32, (1, nbp), 1)
    blk = blk_i.astype(F32)
    gw = NSA_GROUP * HEAD_DIM
    half = NSA_KV_HEADS * HEAD_DIM
    for g in range(NSA_KV_HEADS):
        qn = _stack_heads(qn_ref[:, g * gw:(g + 1) * gw], NSA_GROUP)
        kc = _head_rms(means_ref[0, :, g * HEAD_DIM:(g + 1) * HEAD_DIM], gkc_ref[...])
        vc = means_ref[0, :, half + g * HEAD_DIM:half + (g + 1) * HEAD_DIM]
        s = _bdot_nt(qn, kc) * SCALE
        p_c, _, _ = _softmax_rows(s, (blk_i + 1) * NSA_BLOCK - 1 <= qpos)
        o_c = _bdot(p_c, vc)
        for r in range(NSA_GROUP):
            oc_ref[:, g * gw + r * HEAD_DIM:g * gw + (r + 1) * HEAD_DIM] = o_c[r * tq:(r + 1) * tq]
        imp = _importance(p_c, tq, qpos1, blk)
        sel_ref[0, g * tq:(g + 1) * tq, :] = _top_n_mask(imp, blk, NSA_TOP_N)


def _nsa_cmp_sample(qn, means, g_kc, batch, pos0):
    n, w = qn.shape
    tq = n // batch
    nbp = means.shape[1]
    return pl.pallas_call(
        functools.partial(_nsa_cmp_sample_kernel, pos0=pos0),
        grid=(batch,),
        in_specs=[pl.BlockSpec((tq, w), lambda b: (b, 0)),
                  pl.BlockSpec((1, nbp, means.shape[2]), lambda b: (b, 0, 0)),
                  pl.BlockSpec((1, HEAD_DIM), lambda b: (0, 0))],
        out_specs=[pl.BlockSpec((tq, w), lambda b: (b, 0)),
                   pl.BlockSpec((1, NSA_KV_HEADS * tq, nbp), lambda b: (b, 0, 0))],
        out_shape=[jax.ShapeDtypeStruct((n, w), F32),
                   jax.ShapeDtypeStruct((batch, NSA_KV_HEADS * tq, nbp), F32)],
        compiler_params=_params("parallel"),
        name="nsa_cmp_sample",
    )(qn, means, g_kc)


def _block_diag_queries(q, n_kv, reps):
    parts = []
    for g in range(n_kv):
        for r in range(reps):
            hq = q[:, (g * reps + r) * HEAD_DIM:(g * reps + r + 1) * HEAD_DIM]
            parts.append(jnp.concatenate(
                [hq if gg == g else jnp.zeros_like(hq) for gg in range(n_kv)], axis=1))
    return jnp.concatenate(parts, axis=0).astype(BF16)


def _rep_rows(x, n_kv, per):
    return jnp.concatenate(
        [jnp.broadcast_to(x[g:g + 1, :], (per, x.shape[1])) for g in range(n_kv)], axis=0)


def _online_update(s, mask, v, m_ref, l_ref, acc_ref):
    if mask is not None:
        s = jnp.where(mask, s, NEG)
    m_old = m_ref[...]
    m_new = jnp.maximum(m_old, jnp.max(s, axis=-1, keepdims=True))
    pr = jnp.exp(s - m_new)
    if mask is not None:
        pr = jnp.where(mask, pr, 0.0)
    alpha = jnp.exp(m_old - m_new)
    l_ref[...] = alpha * l_ref[...] + jnp.sum(pr, axis=-1, keepdims=True)
    acc_ref[...] = alpha * acc_ref[...] + _bdot(pr, v)
    m_ref[...] = m_new


def _pad_rows(x, rows):
    return jnp.concatenate([x, jnp.zeros((rows - x.shape[0], x.shape[1]), x.dtype)], axis=0)


def _write_diag(o_ref, l_ref, acc_ref, n_kv, reps, tq):
    for g in range(n_kv):
        for r in range(reps):
            rs = slice((g * reps + r) * tq, (g * reps + r + 1) * tq)
            o = acc_ref[rs, g * HEAD_DIM:(g + 1) * HEAD_DIM] / jnp.maximum(l_ref[rs, :], 1e-30)
            o_ref[:, (g * reps + r) * HEAD_DIM:(g * reps + r + 1) * HEAD_DIM] = o


FOX_PAGES = 8
HEAD_SHIFT = 3


def _fox_sample_kernel(pt_ref, q_ref, *refs, n_pg):
    del pt_ref
    page_refs, lf_refs = refs[:n_pg], refs[n_pg:2 * n_pg]
    new_ref, lfn_ref, o_ref, cq_ref, carry_ref, m_ref, l_ref, acc_ref = refs[2 * n_pg:]
    p = pl.program_id(1)
    tq = q_ref.shape[0]
    page = page_refs[0].shape[2]
    rows = FOX_HEADS * tq
    row_h = jnp.concatenate([jnp.full((tq, 1), h, jnp.int32) for h in range(FOX_HEADS)], axis=0)
    q_all = _stack_heads(q_ref[...], FOX_HEADS).astype(BF16)
    pr_i = lax.broadcasted_iota(jnp.int32, (LANES, LANES), 0)
    pc_i = lax.broadcasted_iota(jnp.int32, (LANES, LANES), 1)

    @pl.when(p == 0)
    def _():
        carry_ref[...] = jnp.zeros_like(carry_ref)
        lane = lax.broadcasted_iota(jnp.int32, (1, LANES), 1)
        cnew = _exact_dot(lfn_ref[0], (pr_i <= pc_i).astype(BF16))
        cnew_rows = _rep_rows(cnew, FOX_HEADS, tq)
        row_t = _token_of_row(FOX_HEADS, tq)
        cq = jnp.sum(jnp.where(lane == row_t, cnew_rows, 0.0), axis=-1, keepdims=True)
        cq_ref[...] = cq
        kw = FOX_HEADS * HEAD_DIM
        qbd = _block_diag_queries(q_ref[...], FOX_HEADS, 1)
        kn = _pad_rows(new_ref[:, 0:kw], LANES)
        vn = _pad_rows(new_ref[:, kw:2 * kw], LANES)
        mask = lane <= row_t
        s = jnp.where(mask, _bdot_nt(qbd, kn) * SCALE + (cq - cnew_rows), NEG)
        m = jnp.max(s, axis=-1, keepdims=True)
        pr = jnp.where(mask, jnp.exp(s - m), 0.0)
        o_full = _bdot(pr, vn)
        m_ref[...] = m
        l_ref[...] = jnp.sum(pr, axis=-1, keepdims=True)
        acc_ref[...] = jnp.concatenate(
            [o_full[h * tq:(h + 1) * tq, h * HEAD_DIM:(h + 1) * HEAD_DIM] for h in range(FOX_HEADS)],
            axis=0)

    same_head = (pr_i & (FOX_HEADS - 1)) == (pc_i & (FOX_HEADS - 1))
    later_in_row = (same_head & ((pr_i >> HEAD_SHIFT) > (pc_i >> HEAD_SHIFT))).astype(BF16)
    row_total = same_head.astype(BF16)
    n_r = page * FOX_HEADS // LANES
    lane_k = lax.broadcasted_iota(jnp.int32, (1, n_pg * page * FOX_HEADS), 1)
    head_ok = (lane_k & (FOX_HEADS - 1)) == row_h
    cq = cq_ref[...]
    carry = carry_ref[...]
    s_parts, v_parts = [], []
    for k in range(n_pg):
        lf = lf_refs[k][0, 0]
        within = _exact_dot(lf, later_in_row)
        tot = _exact_dot(lf, row_total)
        run = carry
        later = [None] * n_r
        for r in reversed(range(n_r)):
            later[r] = run
            run = run + tot[r:r + 1, :]
        carry = run
        suffix = within + jnp.concatenate(later, axis=0)
        xk = page_refs[k][0, 0, :, 0].reshape(page * FOX_HEADS, HEAD_DIM)
        s_all = _bdot_nt(q_all, xk) * SCALE
        s_parts += [s_all[:, r * LANES:(r + 1) * LANES] + (cq + suffix[r:r + 1, :]) for r in range(n_r)]
        v_parts.append(page_refs[k][0, 0, :, 1].reshape(page * FOX_HEADS, HEAD_DIM))
    carry_ref[...] = carry
    s = jnp.concatenate(s_parts, axis=1)
    _online_update(s, head_ok, jnp.concatenate(v_parts, axis=0), m_ref, l_ref, acc_ref)

    @pl.when(p == pl.num_programs(1) - 1)
    def _():
        o = acc_ref[...] / jnp.maximum(l_ref[...], 1e-30)
        for h in range(FOX_HEADS):
            o_ref[:, h * HEAD_DIM:(h + 1) * HEAD_DIM] = o[h * tq:(h + 1) * tq]


def _fox_sample(qb, cache_kv, cache_lf, layer, page_table, rows_new, lf_new_t):
    n, w = qb.shape
    b, n_pages = page_table.shape
    tq = n // b
    page = cache_kv.shape[2]
    rows = FOX_HEADS * tq
    n_pg = math.gcd(FOX_PAGES, n_pages)
    assert tq <= LANES and (page * FOX_HEADS) % LANES == 0
    pg6 = lambda k: (lambda bi, p, pt: (layer, pt[bi, n_pages - 1 - (p * n_pg + k)], 0, 0, 0, 0))
    pg4 = lambda k: (lambda bi, p, pt: (layer, pt[bi, n_pages - 1 - (p * n_pg + k)], 0, 0))
    grid_spec = pltpu.PrefetchScalarGridSpec(
        num_scalar_prefetch=1,
        grid=(b, n_pages // n_pg),
        in_specs=[pl.BlockSpec((tq, w), lambda bi, p, pt: (bi, 0))]
        + [pl.BlockSpec((1, 1) + cache_kv.shape[2:], pg6(k)) for k in range(n_pg)]
        + [pl.BlockSpec((1, 1) + cache_lf.shape[2:], pg4(k)) for k in range(n_pg)]
        + [pl.BlockSpec((tq, 2 * w), lambda bi, p, pt: (bi, 0)),
           pl.BlockSpec((1, FOX_HEADS, LANES), lambda bi, p, pt: (bi, 0, 0))],
        out_specs=pl.BlockSpec((tq, w), lambda bi, p, pt: (bi, 0)),
        scratch_shapes=[pltpu.VMEM((rows, 1), F32), pltpu.VMEM((1, LANES), F32),
                        pltpu.VMEM((rows, 1), F32), pltpu.VMEM((rows, 1), F32),
                        pltpu.VMEM((rows, HEAD_DIM), F32)],
    )
    return pl.pallas_call(
        functools.partial(_fox_sample_kernel, n_pg=n_pg),
        grid_spec=grid_spec,
        out_shape=jax.ShapeDtypeStruct((n, w), F32),
        compiler_params=_params("parallel", "arbitrary"),
        name="fox_sample",
    )(page_table, qb, *([cache_kv] * n_pg), *([cache_lf] * n_pg), rows_new, lf_new_t)


SEL_PAGES = 8


def _nsa_sel_sample_kernel(pt_ref, q_ref, *refs, n_pg, pos0):
    del pt_ref
    page_refs = refs[:n_pg]
    new_ref, sel_ref, o_ref, qbd_ref, m_ref, l_ref, acc_ref = refs[n_pg:]
    p = pl.program_id(1)
    tq = q_ref.shape[0]
    page = page_refs[0].shape[2] // NSA_ROWS
    kw = NSA_KV_HEADS * HEAD_DIM
    nbp = sel_ref.shape[2]
    selm = sel_ref[0].astype(BF16)

    def key_mask(first_block, n_keys):
        blk_r = lax.broadcasted_iota(jnp.int32, (nbp, n_keys), 0)
        blk_of_lane = lax.broadcasted_iota(jnp.int32, (nbp, n_keys), 1) >> BLOCK_SHIFT
        expand = (blk_r == first_block + blk_of_lane).astype(BF16)
        km = jnp.dot(selm, expand, preferred_element_type=F32)
        return jnp.concatenate(
            [km[g * tq:(g + 1) * tq] for g in range(NSA_KV_HEADS) for _ in range(NSA_GROUP)],
            axis=0) > 0.5

    @pl.when(p == 0)
    def _():
        m_ref[...] = jnp.full_like(m_ref, NEG)
        l_ref[...] = jnp.zeros_like(l_ref)
        acc_ref[...] = jnp.zeros_like(acc_ref)
        qbd = _block_diag_queries(q_ref[...], NSA_KV_HEADS, NSA_GROUP)
        qbd_ref[...] = qbd
        row_t = _token_of_row(NSA_HEADS, tq)
        lane = lax.broadcasted_iota(jnp.int32, (1, LANES), 1)
        kn = _pad_rows(new_ref[:, 0:kw], LANES)
        vn = _pad_rows(new_ref[:, kw:2 * kw], LANES)
        s = _bdot_nt(qbd, kn) * SCALE
        _online_update(s, key_mask(pos0 // NSA_BLOCK, LANES) & (lane <= row_t), vn,
                       m_ref, l_ref, acc_ref)

    def slab(typ):
        return jnp.concatenate(
            [jnp.concatenate([_page_rows(page_refs[k], typ, g, 0, page) for g in range(NSA_KV_HEADS)],
                             axis=1) for k in range(n_pg)], axis=0)

    s = _bdot_nt(qbd_ref[...], slab(2)) * SCALE
    _online_update(s, key_mask(p * (n_pg * page // NSA_BLOCK), n_pg * page), slab(3),
                   m_ref, l_ref, acc_ref)

    @pl.when(p == pl.num_programs(1) - 1)
    def _():
        _write_diag(o_ref, l_ref, acc_ref, NSA_KV_HEADS, NSA_GROUP, tq)


def _nsa_sel_sample(qr, cache, layer, page_table, rows_new, sel, pos0):
    n, w = qr.shape
    b, n_pages = page_table.shape
    tq = n // b
    rows_pp = cache.shape[2]
    page = rows_pp // NSA_ROWS
    kvw = 2 * NSA_KV_HEADS * HEAD_DIM
    rows = NSA_HEADS * tq
    n_pg = math.gcd(SEL_PAGES, n_pages)
    assert pos0 == n_pages * page and pos0 % NSA_BLOCK == 0 and tq <= NSA_BLOCK
    page_spec = lambda k: pl.BlockSpec(
        (1, 1, rows_pp, LANES), lambda bi, p, pt: (layer, pt[bi, p * n_pg + k], 0, 0))
    grid_spec = pltpu.PrefetchScalarGridSpec(
        num_scalar_prefetch=1,
        grid=(b, n_pages // n_pg),
        in_specs=[pl.BlockSpec((tq, w), lambda bi, p, pt: (bi, 0))]
        + [page_spec(k) for k in range(n_pg)]
        + [pl.BlockSpec((tq, kvw), lambda bi, p, pt: (bi, 1)),
           pl.BlockSpec((1,) + sel.shape[1:], lambda bi, p, pt: (bi, 0, 0))],
        out_specs=pl.BlockSpec((tq, w), lambda bi, p, pt: (bi, 0)),
        scratch_shapes=[pltpu.VMEM((rows, kvw // 2), BF16), pltpu.VMEM((rows, 1), F32),
                        pltpu.VMEM((rows, 1), F32), pltpu.VMEM((rows, kvw // 2), F32)],
    )
    return pl.pallas_call(
        functools.partial(_nsa_sel_sample_kernel, n_pg=n_pg, pos0=pos0),
        grid_spec=grid_spec,
        out_shape=jax.ShapeDtypeStruct((n, w), F32),
        compiler_params=_params("parallel", "arbitrary"),
        name="nsa_sel_sample",
    )(page_table, qr, *([cache] * n_pg), rows_new, sel)


def _nsa_win_sample_kernel(qr_ref, win_ref, new_ref, oc_ref, os_ref, gate_ref, gbias_ref,
                           o_ref, nwin_ref, kv_ref):
    tq = qr_ref.shape[0]
    lw = win_ref.shape[1]
    nk = kv_ref.shape[0]
    kw = NSA_KV_HEADS * HEAD_DIM
    gw = NSA_GROUP * HEAD_DIM
    kv_ref[0:lw, :] = win_ref[0]
    kv_ref[lw:nk, :] = _pad_rows(new_ref[...], nk - lw)
    nwin_ref[0, 0:lw - tq, :] = win_ref[0, tq:lw, :]
    nwin_ref[0, lw - tq:lw, :] = new_ref[...]

    idx = lax.broadcasted_iota(jnp.int32, (1, nk), 1)
    row_t = _token_of_row(NSA_GROUP, tq)
    dist = lw + row_t - idx
    mask = (dist >= 0) & (dist < NSA_WINDOW) & (idx < lw + tq)
    gates = _sigmoid(gate_ref[...] + gbias_ref[...])
    for g in range(NSA_KV_HEADS):
        qr = _stack_heads(qr_ref[:, g * gw:(g + 1) * gw], NSA_GROUP)
        s = _bdot_nt(qr, kv_ref[:, g * HEAD_DIM:(g + 1) * HEAD_DIM]) * SCALE
        p_w, _, _ = _softmax_rows(s, mask)
        o_w = _bdot(p_w, kv_ref[:, kw + g * HEAD_DIM:kw + (g + 1) * HEAD_DIM])
        for r in range(NSA_GROUP):
            cs = slice(g * gw + r * HEAD_DIM, g * gw + (r + 1) * HEAD_DIM)
            gc = g * LANES + 3 * r
            o_ref[:, cs] = (gates[:, gc:gc + 1] * oc_ref[:, cs] + gates[:, gc + 1:gc + 2] * os_ref[:, cs]
                            + gates[:, gc + 2:gc + 3] * o_w[r * tq:(r + 1) * tq])


def _nsa_win_sample(qr, win, rows_win, o_c, o_s, gates, gbias, batch):
    n, w = qr.shape
    tq = n // batch
    lw, ww = win.shape[1], win.shape[2]
    nk = lw + LANES
    assert lw >= NSA_WINDOW and tq % 8 == 0 and tq <= LANES
    row = lambda a: pl.BlockSpec((tq, a.shape[1]), lambda b: (b, 0))
    return pl.pallas_call(
        _nsa_win_sample_kernel,
        grid=(batch,),
        in_specs=[row(qr), pl.BlockSpec((1, lw, ww), lambda b: (b, 0, 0)), row(rows_win),
                  row(o_c), row(o_s), row(gates), pl.BlockSpec(gbias.shape, lambda b: (0, 0))],
        out_specs=[row(qr), pl.BlockSpec((1, lw, ww), lambda b: (b, 0, 0))],
        out_shape=[jax.ShapeDtypeStruct((n, w), F32), jax.ShapeDtypeStruct(win.shape, F32)],
        scratch_shapes=[pltpu.VMEM((nk, ww), F32)],
        compiler_params=_params("parallel"),
        name="nsa_win_sample",
    )(qr, win, rows_win, o_c, o_s, gates, gbias)


DIL_CHUNK = 256
DIL_HEAD_SHIFT = 4


def _dil_sample_kernel(q_ref, st_ref, new_ref, o_ref, m_ref, l_ref, acc_ref, *, lw):
    c = pl.program_id(1)
    n_chunks = pl.num_programs(1) - 1
    tq = q_ref.shape[0]
    rows = DIL_HEADS * tq
    n_g = len(DIL_PAIRS)
    q_all = _stack_heads(q_ref[...], DIL_HEADS).astype(BF16)
    row_t = _token_of_row(DIL_HEADS, tq)
    row_h = jnp.concatenate([jnp.full((tq, 1), h, jnp.int32) for h in range(DIL_HEADS)], axis=0)

    @pl.when(c == 0)
    def _():
        m_ref[...] = jnp.full_like(m_ref, NEG)
        l_ref[...] = jnp.zeros_like(l_ref)
        acc_ref[...] = jnp.zeros_like(acc_ref)

    def process(k_rows, v_rows, j0, first_chunk):
        n_l = k_rows.shape[0] * DIL_HEADS
        xk = k_rows[...].reshape(n_l, HEAD_DIM)
        s = _bdot_nt(q_all, xk) * SCALE
        lane = lax.broadcasted_iota(jnp.int32, (1, n_l), 1)
        head_ok = (lane & (DIL_HEADS - 1)) == row_h
        off = lw + row_t - (j0 + (lane >> DIL_HEAD_SHIFT))

        def group(gi, window, dil):
            mask = head_ok & (off >= 0) & (off <= window) & ((off & (dil - 1)) == 0)
            sm = jnp.where(mask, s, NEG)
            m_old = m_ref[gi]
            m_new = jnp.maximum(m_old, jnp.max(sm, axis=-1, keepdims=True))
            pr = jnp.where(mask, jnp.exp(sm - m_new), 0.0)
            alpha = jnp.exp(m_old - m_new)
            l_ref[gi] = alpha * l_ref[gi] + jnp.sum(pr, axis=-1, keepdims=True)
            acc_ref[gi] = alpha * acc_ref[gi] + _bdot(pr, v_rows[...].reshape(n_l, HEAD_DIM))
            m_ref[gi] = m_new

        for gi, (window, dil) in enumerate(DIL_PAIRS):
            pl.when(c >= first_chunk(window))(functools.partial(group, gi, window, dil))

    @pl.when(c < n_chunks)
    def _():
        n_pos = st_ref.shape[2]
        process(st_ref.at[0, 0, :, 0], st_ref.at[0, 0, :, 1], c * n_pos,
                lambda window: max(lw - window, 0) // n_pos)

    @pl.when(c == n_chunks)
    def _():
        process(new_ref.at[0, :, 0], new_ref.at[0, :, 1], lw, lambda window: 0)
        lses = [m_ref[gi] + jnp.log(l_ref[gi]) for gi in range(n_g)]
        mx = jnp.maximum(jnp.maximum(lses[0], lses[1]), lses[2])
        ws = [jnp.exp(l - mx) for l in lses]
        tot = ws[0] + ws[1] + ws[2]
        o = sum((ws[gi] / tot) * (acc_ref[gi] / l_ref[gi]) for gi in range(n_g))
        for h in range(DIL_HEADS):
            o_ref[:, h * HEAD_DIM:(h + 1) * HEAD_DIM] = o[h * tq:(h + 1) * tq]


def _dil_sample(q, state, layer, rows_new, batch):
    n, w = q.shape
    tq = n // batch
    lw = state.shape[2]
    assert lw >= DIL_PAIRS[-1][0] and lw % DIL_CHUNK == 0
    n_chunks = lw // DIL_CHUNK
    rows = DIL_HEADS * tq
    n_g = len(DIL_PAIRS)
    return pl.pallas_call(
        functools.partial(_dil_sample_kernel, lw=lw),
        grid=(batch, n_chunks + 1),
        in_specs=[pl.BlockSpec((tq, w), lambda b, c: (b, 0)),
                  pl.BlockSpec((1, 1, DIL_CHUNK) + state.shape[3:],
                               lambda b, c: (layer, b, jnp.minimum(c, n_chunks - 1), 0, 0, 0)),
                  pl.BlockSpec((1,) + rows_new.shape[1:], lambda b, c: (b, 0, 0, 0, 0))],
        out_specs=pl.BlockSpec((tq, w), lambda b, c: (b, 0)),
        out_shape=jax.ShapeDtypeStruct((n, w), F32),
        scratch_shapes=[pltpu.VMEM((n_g, rows, 1), F32), pltpu.VMEM((n_g, rows, 1), F32),
                        pltpu.VMEM((n_g, rows, HEAD_DIM), F32)],
        compiler_params=_params("parallel", "arbitrary"),
        name="dil_sample",
    )(q, state, rows_new)


def _state_shift_kernel(*refs):
    st_ref, new_ref, out_ref = refs[-3:]
    lw, tq = st_ref.shape[0], new_ref.shape[0]
    out_ref[0:lw - tq] = st_ref[tq:lw]
    out_ref[lw - tq:lw] = new_ref[...]


def _state_shift(state, layer, rows_new, prev_out):
    n_l, b, lw, n_s = state.shape[:4]
    tq = rows_new.shape[1]
    tile = state.shape[4:]
    st_spec = lambda: pl.BlockSpec((None, None, lw, None) + tile, lambda bi, s: (layer, bi, 0, s, 0, 0))
    in_specs = [st_spec(), pl.BlockSpec((None, tq, None) + tile, lambda bi, s: (bi, 0, s, 0, 0))]
    args = [state, rows_new]
    aliases = {}
    if prev_out is not None:
        in_specs = [pl.BlockSpec(memory_space=pl.ANY)] + in_specs
        args = [prev_out] + args
        aliases = {0: 0}
    return pl.pallas_call(
        _state_shift_kernel,
        grid=(b, n_s),
        in_specs=in_specs,
        out_specs=st_spec(),
        out_shape=jax.ShapeDtypeStruct(state.shape, F32),
        input_output_aliases=aliases,
        compiler_params=_params("parallel", "arbitrary"),
        name="state_shift",
    )(*args)


def _rope_tables(pos):
    half = HEAD_DIM // 2
    inv = ROPE_THETA ** (-jnp.arange(half, dtype=F32) / half)
    ang = pos.astype(F32)[:, None] * inv[None, :]
    cos, sin = jnp.cos(ang), jnp.sin(ang)
    return jnp.concatenate([cos, cos], axis=-1), jnp.concatenate([-sin, sin], axis=-1)


def _pad_lanes(a, width=LANES):
    return jnp.pad(a, ((0, 0),) * (a.ndim - 1) + ((0, width - a.shape[-1]),))


def _chunks(first, count, op="copy", gi=0):
    return [(first + c, op, gi) for c in range(count)]


NSA_W = NSA_HEADS * HEAD_DIM
NSA_KV_W = NSA_KV_HEADS * HEAD_DIM
FOX_W = FOX_HEADS * HEAD_DIM
DIL_W = DIL_HEADS * HEAD_DIM
GATES_PER_GROUP = 3 * NSA_GROUP


def _even_weights(w_in, w_out, gate_b, forget_b, nsa_g, fox_g):
    cuts = [0]
    for wd in ((NSA_W,) + (NSA_KV_W,) * 6
               + (3 * NSA_HEADS, NSA_W, FOX_W, FOX_W, FOX_W, FOX_HEADS, FOX_W)):
        cuts.append(cuts[-1] + wd)
    seg = lambda a, b: w_in[:, cuts[a]:cuts[b]]
    gate_w, f_w = seg(7, 8), seg(12, 13)
    small = [_pad_lanes(f_w)] + [_pad_lanes(gate_w[:, g * GATES_PER_GROUP:(g + 1) * GATES_PER_GROUP])
                                 for g in range(NSA_KV_HEADS)]
    gbias = jnp.concatenate(
        [_pad_lanes(gate_b[None, g * GATES_PER_GROUP:(g + 1) * GATES_PER_GROUP])
         for g in range(NSA_KV_HEADS)], axis=1)
    aux = jnp.stack([nsa_g[0], nsa_g[2], nsa_g[3], fox_g[0], fox_g[1], _pad_lanes(forget_b),
                     jnp.zeros((LANES,), F32), jnp.zeros((LANES,), F32)])
    return dict(
        w_a=seg(0, 7).astype(BF16),
        w_b=seg(8, 10).astype(BF16),
        w_c=seg(10, 12).astype(BF16),
        w_d=jnp.concatenate([seg(13, 14)] + small, axis=1).astype(BF16),
        w_out_a=w_out[:NSA_W].astype(BF16), w_out_b=w_out[NSA_W:].astype(BF16),
        aux=aux, gbias=gbias, g_kc=nsa_g[1][None, :])


def _even_projections(x, g_norm, wts, pos):
    h = _rmsnorm(x, g_norm)
    cos2, sin2 = _rope_tables(pos)
    nq, nkv = NSA_HEADS, NSA_KV_HEADS
    qn, qr, rows_nsa, rows_win = _proj(
        h, wts["w_a"], wts["aux"], cos2, sin2,
        [[_chunks(0, nq, "rms", 0)], [_chunks(0, nq, "rms_rope", 0)],
         [_chunks(nq, 2 * nkv) + _chunks(nq + 2 * nkv, nkv, "rms_rope", 1) + _chunks(nq + 3 * nkv, nkv)],
         [_chunks(nq + 4 * nkv, nkv, "rms_rope", 2) + _chunks(nq + 5 * nkv, nkv)]],
        wts["w_a"].shape[1])
    z_a, qb = _proj(h, wts["w_b"], wts["aux"], cos2, sin2,
                    [[_chunks(0, nq)], [_chunks(nq, FOX_HEADS, "rms", 3)]], wts["w_b"].shape[1])
    rows_fox, = _proj(h, wts["w_c"], wts["aux"], cos2, sin2,
                      [[_chunks(0, FOX_HEADS, "rms", 4) + _chunks(FOX_HEADS, FOX_HEADS)]],
                      wts["w_c"].shape[1])
    z_b, logf, gates = _proj(
        h, wts["w_d"], wts["aux"], cos2, sin2,
        [[_chunks(0, FOX_HEADS)], [_chunks(FOX_HEADS, 1, "logsig", 5)], [_chunks(FOX_HEADS + 1, nkv)]],
        wts["w_d"].shape[1])
    return qn, qr, rows_nsa, rows_win, z_a, qb, rows_fox, z_b, logf, gates


def _even_prompt(x, batch, g_norm, wts, cmp_pos):
    n = x.shape[0]
    t = n // batch
    pos = jnp.tile(jnp.arange(t, dtype=jnp.int32), batch)
    qn, qr, rows_nsa, rows_win, z_a, qb, rows_fox, z_b, logf, gates = _even_projections(
        x, g_norm, wts, pos)
    c_all = _cumsum_rows(logf, batch)
    ck_t = jnp.transpose(c_all[:, :FOX_HEADS].reshape(batch, t, FOX_HEADS), (0, 2, 1))
    o_b = _fox_prompt(qb, rows_fox, c_all, ck_t, batch)
    o_a = _nsa_prompt(qn, qr, rows_nsa, rows_win, gates, wts["gbias"], cmp_pos, wts["g_kc"], batch)
    y = _outproj(x, [o_a, o_b], [z_a, z_b], [wts["w_out_a"], wts["w_out_b"]])
    lw = min(NSA_WINDOW, t)
    new_win = rows_win.reshape(batch, t, 2, NSA_KV_HEADS, HEAD_DIM)[:, t - lw:]
    return y, (rows_nsa.reshape(batch, t, 4, NSA_KV_HEADS, HEAD_DIM),
               rows_fox.reshape(batch, t, 2, FOX_HEADS, HEAD_DIM),
               logf[:, :FOX_HEADS].reshape(batch, t, FOX_HEADS), new_win)


def _even_sample(x, batch, pos0, g_norm, wts, cmp_pos, layer, cache_nsa, cache_fox, cache_lf,
                 win_state, page_table):
    n = x.shape[0]
    tq = n // batch
    pos = jnp.tile(pos0 + jnp.arange(tq, dtype=jnp.int32), batch)
    qn, qr, rows_nsa, rows_win, z_a, qb, rows_fox, z_b, logf, gates = _even_projections(
        x, g_norm, wts, pos)
    means = _page_means(cache_nsa, layer, page_table, cmp_pos)
    n_blocks = means.shape[1]
    nbp = -(-(n_blocks + 1) // LANES) * LANES
    means = jnp.pad(means, ((0, 0), (0, nbp - n_blocks), (0, 0)))
    o_c, sel = _nsa_cmp_sample(qn, means, wts["g_kc"], batch, pos0)
    o_s = _nsa_sel_sample(qr, cache_nsa, layer, page_table, rows_nsa, sel, pos0)
    lw = win_state.shape[1]
    win = win_state.reshape(batch, lw, 2 * NSA_KV_W)
    o_a, new_win = _nsa_win_sample(qr, win, rows_win, o_c, o_s, gates, wts["gbias"], batch)
    lf_new_t = _pad_lanes(
        jnp.transpose(logf[:, :FOX_HEADS].reshape(batch, tq, FOX_HEADS), (0, 2, 1)))
    o_b = _fox_sample(qb, cache_fox, cache_lf, layer, page_table, rows_fox, lf_new_t)
    y = _outproj(x, [o_a, o_b], [z_a, z_b], [wts["w_out_a"], wts["w_out_b"]])
    return y, (rows_nsa.reshape(batch, tq, 4, NSA_KV_HEADS, HEAD_DIM),
               rows_fox.reshape(batch, tq, 2, FOX_HEADS, HEAD_DIM),
               logf[:, :FOX_HEADS].reshape(batch, tq, FOX_HEADS),
               new_win.reshape(batch, lw, 2, NSA_KV_HEADS, HEAD_DIM))


def _odd_weights(w_in, w_out, qk_g):
    aux = jnp.concatenate([qk_g, jnp.zeros((6, HEAD_DIM), F32)], axis=0)
    return dict(w_q=w_in[:, :DIL_W].astype(BF16), w_kv=w_in[:, DIL_W:3 * DIL_W].astype(BF16),
                w_z=w_in[:, 3 * DIL_W:].astype(BF16), w_out=w_out.astype(BF16), aux=aux)


def _odd_projections(x, g_norm, wts, pos, head_major_rows=False):
    h = _rmsnorm(x, g_norm)
    cos2, sin2 = _rope_tables(pos)
    q, = _proj(h, wts["w_q"], wts["aux"], cos2, sin2, [[_chunks(0, DIL_HEADS, "rms_rope", 0)]], DIL_W)
    rows, = _proj(h, wts["w_kv"], wts["aux"], cos2, sin2,
                  [[_chunks(0, DIL_HEADS, "rms_rope", 1), _chunks(0, DIL_HEADS)]], DIL_W,
                  head_major=[head_major_rows])
    z, = _proj(h, wts["w_z"], wts["aux"], cos2, sin2, [[_chunks(0, DIL_HEADS)]], DIL_W)
    return q, rows, z


def _odd_prompt(x, batch, g_norm, wts):
    n = x.shape[0]
    t = n // batch
    pos = jnp.tile(jnp.arange(t, dtype=jnp.int32), batch)
    q, rows, z = _odd_projections(x, g_norm, wts, pos)
    o = _dil_prompt(q, rows, batch)
    y = _outproj(x, [o], [z], [wts["w_out"]])
    lw = min(DIL_PAIRS[-1][0], t)
    return y, rows.reshape(batch, t, 2, DIL_HEADS, HEAD_DIM)[:, t - lw:]


STATE_TILE = 8


def _odd_sample(x, batch, pos0, g_norm, wts, state_all, layer, prev_out):
    n = x.shape[0]
    tq = n // batch
    pos = jnp.tile(pos0 + jnp.arange(tq, dtype=jnp.int32), batch)
    q, rows, z = _odd_projections(x, g_norm, wts, pos, head_major_rows=True)
    n_l, _, lw = state_all.shape[:3]
    o = _dil_sample(q, state_all, layer, rows.reshape(batch, tq, 2, DIL_HEADS, HEAD_DIM), batch)
    y = _outproj(x, [o], [z], [wts["w_out"]])
    n_s = 2 * DIL_HEADS // STATE_TILE
    new_all = _state_shift(state_all.reshape(n_l, batch, lw, n_s, STATE_TILE, HEAD_DIM), layer,
                           rows.reshape(batch, tq, n_s, STATE_TILE, HEAD_DIM), prev_out)
    return y, new_all


def kernel(x_prompt, x_sample, cache_nsa_kv, cache_fox_kv, cache_fox_logf, state_nsa_win_kv, state_dil_kv, page_table, norm_even, w_in_even, w_out_even, nsa_gate_bias, fox_forget_bias, nsa_cmp_pos, nsa_qk_gain, fox_qk_gain, norm_odd, w_in_odd, w_out_odd, dil_qk_gain):
    bp, t, d = x_prompt.shape
    bs, tq, _ = x_sample.shape
    n_layers, n_pool, page = cache_nsa_kv.shape[:3]
    pos0 = page_table.shape[1] * page
    depth = norm_even.shape[0] + norm_odd.shape[0]
    cache_nsa = cache_nsa_kv.reshape(n_layers, n_pool, page * NSA_ROWS, HEAD_DIM)
    cache_lf = cache_fox_logf.reshape(n_layers, n_pool, page * FOX_HEADS // LANES, LANES)
    yp = x_prompt.reshape(bp * t, d)
    ys = x_sample.reshape(bs * tq, d)
    outs = [[] for _ in range(9)]
    dil_s = None
    for layer in range(depth):
        i = layer // 2
        if layer % 2 == 0:
            wts = _even_weights(w_in_even[i], w_out_even[i], nsa_gate_bias[i], fox_forget_bias[i],
                                nsa_qk_gain[i], fox_qk_gain[i])
            yp, new_p = _even_prompt(yp, bp, norm_even[i], wts, nsa_cmp_pos[i])
            ys, new_s = _even_sample(ys, bs, pos0, norm_even[i], wts, nsa_cmp_pos[i], i, cache_nsa,
                                     cache_fox_kv, cache_lf, state_nsa_win_kv[i], page_table)
            for k in range(4):
                outs[2 * k].append(new_p[k])
                outs[2 * k + 1].append(new_s[k])
        else:
            wts = _odd_weights(w_in_odd[i], w_out_odd[i], dil_qk_gain[i])
            yp, buf_p = _odd_prompt(yp, bp, norm_odd[i], wts)
            ys, dil_s = _odd_sample(ys, bs, pos0, norm_odd[i], wts, state_dil_kv, i, dil_s)
            outs[8].append(buf_p)
    return ((yp.reshape(bp, t, d), ys.reshape(bs, tq, d)) + tuple(jnp.stack(o) for o in outs)
            + (dil_s.reshape(state_dil_kv.shape),))
```

```python
import functools
import math

import jax
import jax.numpy as jnp
from jax import lax
from jax.experimental import pallas as pl
from jax.experimental.pallas import tpu as pltpu

F32 = jnp.float32
BF16 = jnp.bfloat16

HEAD_DIM = 128
LANES = 128
NSA_HEADS = 8
NSA_KV_HEADS = 2
NSA_GROUP = NSA_HEADS // NSA_KV_HEADS
NSA_BLOCK = 64
NSA_TOP_N = 16
NSA_WINDOW = 512
NSA_FORCE_SCORE = 1.0e4
FOX_HEADS = 8
DIL_HEADS = 16
DIL_PAIRS = ((128, 1), (512, 4), (2048, 16))
ROPE_THETA = 10000.0
EPS = 1e-6
SCALE = HEAD_DIM ** -0.5
NEG = -1.0e30
LOG2E = math.log2(math.e)
VMEM_LIMIT = 56 * 1024 * 1024
MATMUL_ROWS = 512


def _params(*sem):
    return pltpu.CompilerParams(dimension_semantics=sem, vmem_limit_bytes=VMEM_LIMIT)


def _bdot(a, b):
    return jnp.dot(a.astype(BF16), b.astype(BF16), preferred_element_type=F32)


def _bdot_nt(a, b):
    return lax.dot_general(a.astype(BF16), b.astype(BF16), (((1,), (1,)), ((), ())),
                           preferred_element_type=F32)


def _split3(x):
    hi = x.astype(BF16)
    r1 = x - hi.astype(F32)
    mid = r1.astype(BF16)
    lo = (r1 - mid.astype(F32)).astype(BF16)
    return hi, mid, lo


def _exact_dot(x, m):
    hi, mid, lo = _split3(x)
    return (jnp.dot(hi, m, preferred_element_type=F32)
            + jnp.dot(mid, m, preferred_element_type=F32)
            + jnp.dot(lo, m, preferred_element_type=F32))


def _exact_dot_left(m, x):
    hi, mid, lo = _split3(x)
    return (jnp.dot(m, hi, preferred_element_type=F32)
            + jnp.dot(m, mid, preferred_element_type=F32)
            + jnp.dot(m, lo, preferred_element_type=F32))


def _token_of_row(n_rep, tq):
    return jnp.concatenate([lax.broadcasted_iota(jnp.int32, (tq, 1), 0)] * n_rep, axis=0)


BLOCK_SHIFT = 6


def _sigmoid(x):
    return 1.0 / (1.0 + jnp.exp(-x))


def _log_sigmoid(x):
    return -(jnp.maximum(-x, 0.0) + jnp.log(1.0 + jnp.exp(-jnp.abs(x))))


def _head_rms(y, g):
    return y * lax.rsqrt(jnp.mean(y * y, axis=-1, keepdims=True) + EPS) * g


def _softmax_rows(s, mask):
    s = jnp.where(mask, s, -jnp.inf)
    m = jnp.max(s, axis=-1, keepdims=True)
    m = jnp.where(m > -jnp.inf, m, 0.0)
    e = jnp.exp(s - m)
    den = jnp.sum(e, axis=-1, keepdims=True)
    return e / jnp.maximum(den, 1e-30), m, den


def _rmsnorm_kernel(x_ref, g_ref, h_ref):
    x = x_ref[...]
    y = x * lax.rsqrt(jnp.mean(x * x, axis=-1, keepdims=True) + EPS)
    h_ref[...] = (y * g_ref[...]).astype(BF16)


def _rmsnorm(x, g):
    n, d = x.shape
    tm = min(n, 512)
    return pl.pallas_call(
        _rmsnorm_kernel,
        grid=(n // tm,),
        in_specs=[pl.BlockSpec((tm, d), lambda i: (i, 0)),
                  pl.BlockSpec((1, d), lambda i: (0, 0))],
        out_specs=pl.BlockSpec((tm, d), lambda i: (i, 0)),
        out_shape=jax.ShapeDtypeStruct((n, d), BF16),
        compiler_params=_params("parallel"),
        name="rmsnorm",
    )(x, g.reshape(1, d))


def _proj_kernel(h_ref, w_ref, aux_ref, cos_ref, sin_ref, *out_refs, plans, n_tiles, head_major):
    acc = jnp.dot(h_ref[...], w_ref[...], preferred_element_type=F32)
    tm = h_ref.shape[0]

    def emit(tile):
        for out_ref, plan, hm in zip(out_refs, plans, head_major):
            n_c = len(plan[tile])
            for c, (src, op, gi) in enumerate(plan[tile]):
                y = acc[:, src * LANES:(src + 1) * LANES]
                if op in ("rms", "rms_rope"):
                    y = _head_rms(y, aux_ref[gi:gi + 1, :])
                if op == "rms_rope":
                    y = y * cos_ref[...] + pltpu.roll(y, HEAD_DIM // 2, 1) * sin_ref[...]
                if op == "logsig":
                    y = _log_sigmoid(y + aux_ref[gi:gi + 1, :])
                if hm:
                    out_ref[pl.ds(tile * n_c + c, tm, stride=n_tiles * n_c), :] = y
                else:
                    out_ref[:, c * LANES:(c + 1) * LANES] = y

    if n_tiles == 1:
        emit(0)
    else:
        for t in range(n_tiles):
            pl.when(pl.program_id(0) == t)(functools.partial(emit, t))


def _proj(h, w, aux, cos2, sin2, plans, tn, head_major=None):
    n, k = h.shape
    n_tiles = w.shape[1] // tn
    tm = min(n, MATMUL_ROWS)
    head_major = head_major or [False] * len(plans)
    out_specs, out_shape = [], []
    for plan, hm in zip(plans, head_major):
        n_c = len(plan[0])
        if hm:
            assert n_tiles == 1 or n == tm
            out_specs.append(pl.BlockSpec((tm * n_tiles * n_c, LANES), lambda j, i: (i, 0)))
            out_shape.append(jax.ShapeDtypeStruct((n * n_tiles * n_c, LANES), F32))
        else:
            out_specs.append(pl.BlockSpec((tm, n_c * LANES), lambda j, i: (i, j)))
            out_shape.append(jax.ShapeDtypeStruct((n, n_tiles * n_c * LANES), F32))
    kern = functools.partial(_proj_kernel, plans=plans, n_tiles=n_tiles, head_major=head_major)
    outs = pl.pallas_call(
        kern,
        grid=(n_tiles, n // tm),
        in_specs=[pl.BlockSpec((tm, k), lambda j, i: (i, 0)),
                  pl.BlockSpec((k, tn), lambda j, i: (0, j), pipeline_mode=pl.Buffered(1)),
                  pl.BlockSpec(aux.shape, lambda j, i: (0, 0)),
                  pl.BlockSpec((tm, LANES), lambda j, i: (i, 0)),
                  pl.BlockSpec((tm, LANES), lambda j, i: (i, 0))],
        out_specs=out_specs,
        out_shape=out_shape,
        compiler_params=_params("arbitrary", "arbitrary"),
        name="proj",
    )(h, w, aux, cos2, sin2)
    return outs


def _outproj_kernel(*refs, n_parts):
    x_ref = refs[0]
    y_ref = refs[-1]
    acc = x_ref[...]
    for p in range(n_parts):
        o = refs[1 + p][...]
        z = refs[1 + n_parts + p][...]
        mix = o * (z * _sigmoid(z))
        acc = acc + jnp.dot(mix.astype(BF16), refs[1 + 2 * n_parts + p][...],
                            preferred_element_type=F32)
    y_ref[...] = acc


def _outproj(x, o_parts, z_parts, w_parts):
    n, d = x.shape
    tm = min(n, MATMUL_ROWS)
    n_parts = len(o_parts)
    row = lambda a: pl.BlockSpec((tm, a.shape[1]), lambda i: (i, 0))
    return pl.pallas_call(
        functools.partial(_outproj_kernel, n_parts=n_parts),
        grid=(n // tm,),
        in_specs=([row(x)] + [row(o) for o in o_parts] + [row(z) for z in z_parts]
                  + [pl.BlockSpec(w.shape, lambda i: (0, 0), pipeline_mode=pl.Buffered(1))
                     for w in w_parts]),
        out_specs=row(x),
        out_shape=jax.ShapeDtypeStruct((n, d), F32),
        compiler_params=_params("parallel"),
        name="outproj",
    )(x, *o_parts, *z_parts, *w_parts)


CUM_BLOCK = 256


def _cumsum_kernel(x_ref, tri_ref, c_ref, carry_ref):
    @pl.when(pl.program_id(1) == 0)
    def _():
        carry_ref[...] = jnp.zeros_like(carry_ref)

    c = _exact_dot_left(tri_ref[...], x_ref[...]) + carry_ref[...]
    c_ref[...] = c
    carry_ref[...] = c[CUM_BLOCK - 1:CUM_BLOCK, :]


def _cumsum_rows(x, batch):
    n, w = x.shape
    t = n // batch
    nb = t // CUM_BLOCK
    r = lax.broadcasted_iota(jnp.int32, (CUM_BLOCK, CUM_BLOCK), 0)
    c = lax.broadcasted_iota(jnp.int32, (CUM_BLOCK, CUM_BLOCK), 1)
    tri = (c <= r).astype(BF16)
    return pl.pallas_call(
        _cumsum_kernel,
        grid=(batch, nb),
        in_specs=[pl.BlockSpec((CUM_BLOCK, w), lambda b, j: (b * nb + j, 0)),
                  pl.BlockSpec((CUM_BLOCK, CUM_BLOCK), lambda b, j: (0, 0))],
        out_specs=pl.BlockSpec((CUM_BLOCK, w), lambda b, j: (b * nb + j, 0)),
        out_shape=jax.ShapeDtypeStruct((n, w), F32),
        scratch_shapes=[pltpu.VMEM((1, w), F32)],
        compiler_params=_params("parallel", "arbitrary"),
        name="logf_cumsum",
    )(x, tri)


FOX_TILE = 512


def _lanes(col):
    return jnp.broadcast_to(col, (col.shape[0], LANES))


def _fox_prompt_kernel(q_ref, k_ref, v_ref, cq_ref, ck_ref, o_ref, m_ref, l_ref, acc_ref, cq2_ref):
    i = pl.program_id(1)
    j = pl.program_id(2)
    tq = q_ref.shape[0]
    tk = k_ref.shape[0]

    @pl.when(j == 0)
    def _():
        m_ref[...] = jnp.full_like(m_ref, NEG)
        l_ref[...] = jnp.zeros_like(l_ref)
        acc_ref[...] = jnp.zeros_like(acc_ref)
        for h in range(FOX_HEADS):
            cq2_ref[h] = _lanes(cq_ref[:, h:h + 1] * LOG2E)

    def step(causal):
        if causal:
            mask = (lax.broadcasted_iota(jnp.int32, (tq, tk), 0)
                    >= lax.broadcasted_iota(jnp.int32, (tq, tk), 1))
        for h in range(FOX_HEADS):
            sl = slice(h * HEAD_DIM, (h + 1) * HEAD_DIM)
            u = _bdot_nt(q_ref[:, sl], k_ref[:, sl]) * (SCALE * LOG2E) - ck_ref[0, h:h + 1, :] * LOG2E
            if causal:
                u = jnp.where(mask, u, NEG)
            cq2 = cq2_ref[h]
            m_old = m_ref[h]
            m_new = jnp.maximum(m_old, _lanes(jnp.max(u, axis=-1, keepdims=True)) + cq2)
            p = jnp.exp2(u - jnp.tile(m_new - cq2, (1, tk // LANES)))
            alpha = jnp.exp2(m_old - m_new)
            l_ref[h] = alpha * l_ref[h] + _lanes(jnp.sum(p, axis=-1, keepdims=True))
            acc_ref[:, sl] = alpha * acc_ref[:, sl] + _bdot(p, v_ref[:, sl])
            m_ref[h] = m_new

    pl.when(j < i)(functools.partial(step, False))
    pl.when(j == i)(functools.partial(step, True))

    @pl.when(j == i)
    def _():
        for h in range(FOX_HEADS):
            sl = slice(h * HEAD_DIM, (h + 1) * HEAD_DIM)
            o_ref[:, sl] = acc_ref[:, sl] / jnp.maximum(l_ref[h], 1e-30)


def _fox_prompt(qb, rows_fox, c_all, ck_t, batch):
    n, w = qb.shape
    t = n // batch
    tq = min(FOX_TILE, t)
    nq = t // tq
    return pl.pallas_call(
        _fox_prompt_kernel,
        grid=(batch, nq, nq),
        in_specs=[pl.BlockSpec((tq, w), lambda b, i, j: (b * nq + i, 0)),
                  pl.BlockSpec((tq, w), lambda b, i, j: (b * nq + jnp.minimum(j, i), 0)),
                  pl.BlockSpec((tq, w), lambda b, i, j: (b * nq + jnp.minimum(j, i), 1)),
                  pl.BlockSpec((tq, LANES), lambda b, i, j: (b * nq + i, 0)),
                  pl.BlockSpec((1, FOX_HEADS, tq), lambda b, i, j: (b, 0, jnp.minimum(j, i)))],
        out_specs=pl.BlockSpec((tq, w), lambda b, i, j: (b * nq + i, 0)),
        out_shape=jax.ShapeDtypeStruct((n, w), F32),
        scratch_shapes=[pltpu.VMEM((FOX_HEADS, tq, LANES), F32),
                        pltpu.VMEM((FOX_HEADS, tq, LANES), F32),
                        pltpu.VMEM((tq, w), F32),
                        pltpu.VMEM((FOX_HEADS, tq, LANES), F32)],
        compiler_params=_params("parallel", "parallel", "arbitrary"),
        name="fox_prompt",
    )(qb, rows_fox, rows_fox, c_all, ck_t)


def _stack_heads(x, n):
    return jnp.concatenate([x[:, r * HEAD_DIM:(r + 1) * HEAD_DIM] for r in range(n)], axis=0)


def _top_n_mask(imp, blk, n_sel):
    work = imp
    selm = jnp.zeros(imp.shape, F32)
    for _ in range(n_sel):
        m = jnp.max(work, axis=-1, keepdims=True)
        idx = jnp.min(jnp.where(work == m, blk, 1.0e9), axis=-1, keepdims=True)
        pick = blk == idx
        selm = jnp.where(pick, 1.0, selm)
        work = jnp.where(pick, -jnp.inf, work)
    return selm


def _top_n_mask_t(imp, n_sel):
    work = imp.T
    blk = lax.broadcasted_iota(jnp.int32, work.shape, 0).astype(F32)
    selm = jnp.zeros(work.shape, F32)
    for _ in range(n_sel):
        m = jnp.max(work, axis=0, keepdims=True)
        idx = jnp.min(jnp.where(work == m, blk, 1.0e9), axis=0, keepdims=True)
        pick = blk == idx
        selm = jnp.where(pick, 1.0, selm)
        work = jnp.where(pick, -jnp.inf, work)
    return selm.T


def _importance(p_c, rows, qpos_col, blk):
    imp = p_c[0:rows]
    for r in range(1, NSA_GROUP):
        imp = imp + p_c[r * rows:(r + 1) * rows]
    cur = (qpos_col >> BLOCK_SHIFT).astype(F32)
    forced = (blk == 0.0) | (blk == cur) | (blk == cur - 1.0)
    imp = jnp.where(forced, NSA_FORCE_SCORE, imp)
    return jnp.where(blk <= cur, imp, NEG)


NSA_Q = 128
NSA_TK = 512
NSA_WKEYS = NSA_WINDOW + NSA_Q


def _nsa_prompt_kernel(qn_ref, qr_ref, kc_ref, vc_ref, ks_ref, vs_ref, kw_ref, vw_ref,
                       gate_ref, gbias_ref, cpos_ref, gkc_ref, e_ref, o_ref,
                       kcn_ref, vcm_ref):
    i = pl.program_id(2)
    t = kc_ref.shape[0]
    nb = t // NSA_BLOCK
    nbp = kcn_ref.shape[0]

    @pl.when(i == 0)
    def _():
        kcn_ref[...] = jnp.zeros_like(kcn_ref)
        vcm_ref[...] = jnp.zeros_like(vcm_ref)

        def body(jb, carry):
            rows = pl.ds(pl.multiple_of(jb * NSA_BLOCK, NSA_BLOCK), NSA_BLOCK)
            km = jnp.mean(kc_ref[rows, :] + cpos_ref[...], axis=0, keepdims=True)
            kcn_ref[pl.ds(jb, 1), :] = _head_rms(km, gkc_ref[...])
            vcm_ref[pl.ds(jb, 1), :] = jnp.mean(vc_ref[rows, :], axis=0, keepdims=True)
            return carry

        lax.fori_loop(0, nb, body, 0)

    rq = NSA_GROUP * NSA_Q
    qn = _stack_heads(qn_ref[...], NSA_GROUP)
    qr = _stack_heads(qr_ref[...], NSA_GROUP).astype(BF16)
    qpos1 = i * NSA_Q + lax.broadcasted_iota(jnp.int32, (NSA_Q, 1), 0)
    qpos = jnp.concatenate([qpos1] * NSA_GROUP, axis=0)

    blk_i = lax.broadcasted_iota(jnp.int32, (1, nbp), 1)
    s = _bdot_nt(qn, kcn_ref[...]) * SCALE
    p_c, _, _ = _softmax_rows(s, (blk_i + 1) * NSA_BLOCK - 1 <= qpos)
    o_c = _bdot(p_c, vcm_ref[...])

    blk = blk_i.astype(F32)
    imp = _importance(p_c, NSA_Q, qpos1, blk)
    selm = _top_n_mask_t(imp, min(NSA_TOP_N, nb)).astype(BF16)

    def sel_chunk(c, carry, causal):
        m_old, l_old, acc = carry
        rows = pl.ds(pl.multiple_of(c * NSA_TK, NSA_TK), NSA_TK)
        kmask = jnp.dot(selm, e_ref[c], preferred_element_type=F32)
        bias = (kmask - 1.0) * (-NEG)
        if causal:
            kpos = c * NSA_TK + lax.broadcasted_iota(jnp.int32, (1, NSA_TK), 1)
            bias = jnp.where(kpos <= qpos1, bias, NEG)
        sc = _bdot_nt(qr, ks_ref[rows, :]) * SCALE + jnp.concatenate([bias] * NSA_GROUP, axis=0)
        m_new = jnp.maximum(m_old, _lanes(jnp.max(sc, axis=-1, keepdims=True)))
        p = jnp.exp(sc - jnp.tile(m_new, (1, NSA_TK // LANES)))
        alpha = jnp.exp(m_old - m_new)
        l_new = alpha * l_old + _lanes(jnp.sum(p, axis=-1, keepdims=True))
        acc = alpha * acc + _bdot(p, vs_ref[rows, :])
        return m_new, l_new, acc

    last = (i * NSA_Q) // NSA_TK
    init = (jnp.full((rq, LANES), NEG, F32), jnp.zeros((rq, LANES), F32),
            jnp.zeros((rq, HEAD_DIM), F32))
    carry = lax.fori_loop(0, last, functools.partial(sel_chunk, causal=False), init)
    _, l_s, acc_s = sel_chunk(last, carry, True)
    o_s = acc_s / jnp.maximum(l_s, 1e-30)

    k0 = jnp.clip(i * NSA_Q - NSA_WINDOW, 0, t - NSA_WKEYS)
    k0 = pl.multiple_of(k0, NSA_Q)
    wrows = pl.ds(k0, NSA_WKEYS)
    kpos = k0 + lax.broadcasted_iota(jnp.int32, (1, NSA_WKEYS), 1)
    dist = qpos - kpos
    sw = jnp.where((dist >= 0) & (dist < NSA_WINDOW), _bdot_nt(qr, kw_ref[wrows, :]) * SCALE, NEG)
    m_w = _lanes(jnp.max(sw, axis=-1, keepdims=True))
    e_w = jnp.exp(sw - jnp.tile(m_w, (1, NSA_WKEYS // LANES)))
    o_w = _bdot(e_w, vw_ref[wrows, :]) / _lanes(jnp.sum(e_w, axis=-1, keepdims=True))

    gates = _sigmoid(gate_ref[...] + gbias_ref[...])
    for r in range(NSA_GROUP):
        rs = slice(r * NSA_Q, (r + 1) * NSA_Q)
        o = (gates[:, 3 * r:3 * r + 1] * o_c[rs] + gates[:, 3 * r + 1:3 * r + 2] * o_s[rs]
             + gates[:, 3 * r + 2:3 * r + 3] * o_w[rs])
        o_ref[:, r * HEAD_DIM:(r + 1) * HEAD_DIM] = o


def _expand_matrix(nbp, n_keys, tk):
    kb = (jnp.arange(n_keys, dtype=jnp.int32) // NSA_BLOCK).reshape(n_keys // tk, 1, tk)
    jb = jnp.arange(nbp, dtype=jnp.int32).reshape(1, nbp, 1)
    return (kb == jb).astype(BF16)


def _nsa_prompt(qn, qr, rows_nsa, rows_win, gates, gbias, cmp_pos, g_kc, batch):
    n = qn.shape[0]
    t = n // batch
    assert t % NSA_TK == 0 and t >= NSA_WKEYS
    nqb = t // NSA_Q
    nb = t // NSA_BLOCK
    nbp = -(-nb // LANES) * LANES
    e = _expand_matrix(nbp, t, NSA_TK)
    gw = NSA_GROUP * HEAD_DIM
    qspec = pl.BlockSpec((NSA_Q, gw), lambda b, g, i: (b * nqb + i, g))
    kv = lambda typ: pl.BlockSpec((t, HEAD_DIM), lambda b, g, i: (b, typ * NSA_KV_HEADS + g))
    return pl.pallas_call(
        _nsa_prompt_kernel,
        grid=(batch, NSA_KV_HEADS, nqb),
        in_specs=[qspec, qspec, kv(0), kv(1), kv(2), kv(3), kv(0), kv(1),
                  pl.BlockSpec((NSA_Q, LANES), lambda b, g, i: (b * nqb + i, g)),
                  pl.BlockSpec((1, LANES), lambda b, g, i: (0, g)),
                  pl.BlockSpec(cmp_pos.shape, lambda b, g, i: (0, 0)),
                  pl.BlockSpec((1, HEAD_DIM), lambda b, g, i: (0, 0)),
                  pl.BlockSpec(e.shape, lambda b, g, i: (0, 0, 0))],
        out_specs=qspec,
        out_shape=jax.ShapeDtypeStruct((n, NSA_HEADS * HEAD_DIM), F32),
        scratch_shapes=[pltpu.VMEM((nbp, HEAD_DIM), F32), pltpu.VMEM((nbp, HEAD_DIM), F32)],
        compiler_params=_params("parallel", "parallel", "arbitrary"),
        name="nsa_prompt",
    )(qn, qr, rows_nsa, rows_nsa, rows_nsa, rows_nsa, rows_win, rows_win,
      gates, gbias, cmp_pos, g_kc, e)


DIL_Q = 128
DIL_K = 2 * DIL_Q
DIL_UNROLL = 4


def _dil_prompt_kernel(q_ref, k_ref, v_ref, o_ref, og_ref, lse_ref):
    t = q_ref.shape[0]
    for gi, (window, dil) in enumerate(DIL_PAIRS):
        n_str = t // dil
        n_qb = n_str // DIL_Q
        span = window // dil

        def body(it, carry, gi=gi, dil=dil, n_qb=n_qb, span=span, n_str=n_str):
            r = it // n_qb
            ib = it % n_qb
            q0 = ib * DIL_Q
            k0 = jnp.clip(q0 - DIL_Q, 0, n_str - DIL_K)
            qrows = pl.ds(r + q0 * dil, DIL_Q, stride=dil) if dil > 1 else pl.ds(q0, DIL_Q)
            krows = pl.ds(r + k0 * dil, DIL_K, stride=dil) if dil > 1 else pl.ds(k0, DIL_K)
            s = _bdot_nt(q_ref[qrows, :], k_ref[krows, :]) * SCALE
            qi = q0 + lax.broadcasted_iota(jnp.int32, (DIL_Q, 1), 0)
            kj = k0 + lax.broadcasted_iota(jnp.int32, (1, DIL_K), 1)
            d = qi - kj
            s = jnp.where((d >= 0) & (d <= span), s, NEG)
            m = _lanes(jnp.max(s, axis=-1, keepdims=True))
            e = jnp.exp(s - jnp.tile(m, (1, DIL_K // LANES)))
            den = _lanes(jnp.sum(e, axis=-1, keepdims=True))
            og_ref[gi, qrows, :] = _bdot(e, v_ref[krows, :]) / den
            lse_ref[gi, qrows, :] = m + jnp.log(den)
            return carry

        lax.fori_loop(0, dil * n_qb, body, 0, unroll=DIL_UNROLL)

    lse = lse_ref[...]
    mx = jnp.max(lse, axis=0, keepdims=True)
    w = jnp.exp(lse - mx)
    w = w / jnp.sum(w, axis=0, keepdims=True)
    o_ref[...] = jnp.sum(w * og_ref[...], axis=0)


def _dil_prompt(q, rows, batch):
    n, w = q.shape
    t = n // batch
    assert t % (DIL_K * DIL_PAIRS[-1][1]) == 0
    blk = lambda off: pl.BlockSpec((t, HEAD_DIM), lambda b, h: (b, h + off))
    return pl.pallas_call(
        _dil_prompt_kernel,
        grid=(batch, DIL_HEADS),
        in_specs=[blk(0), blk(0), blk(DIL_HEADS)],
        out_specs=blk(0),
        out_shape=jax.ShapeDtypeStruct((n, w), F32),
        scratch_shapes=[pltpu.VMEM((len(DIL_PAIRS), t, HEAD_DIM), F32),
                        pltpu.VMEM((len(DIL_PAIRS), t, LANES), F32)],
        compiler_params=_params("parallel", "parallel"),
        name="dil_prompt",
    )(q, rows, rows)


NSA_ROWS = 4 * NSA_KV_HEADS
MEANS_PAGES = 16


def _page_rows(page_ref, typ, g, pos0, n_pos):
    return page_ref[0, 0, pl.ds(pos0 * NSA_ROWS + typ * NSA_KV_HEADS + g, n_pos, stride=NSA_ROWS), :]


def _page_means_kernel(pt_ref, *refs, n_pg):
    del pt_ref
    page_refs, cpos_ref, o_ref = refs[:n_pg], refs[n_pg], refs[n_pg + 1]
    n_blk = page_refs[0].shape[2] // (NSA_ROWS * NSA_BLOCK)
    for k in range(n_pg):
        for j in range(n_blk):
            parts = []
            for typ in range(2):
                for g in range(NSA_KV_HEADS):
                    x = _page_rows(page_refs[k], typ, g, j * NSA_BLOCK, NSA_BLOCK)
                    if typ == 0:
                        x = x + cpos_ref[...]
                    parts.append(jnp.mean(x, axis=0, keepdims=True))
            o_ref[0, k * n_blk + j:k * n_blk + j + 1, :] = jnp.concatenate(parts, axis=1)


def _page_means(cache, layer, page_table, cmp_pos):
    b, n_pages = page_table.shape
    rows = cache.shape[2]
    n_blk = rows // (NSA_ROWS * NSA_BLOCK)
    n_pg = math.gcd(MEANS_PAGES, n_pages)
    w = 2 * NSA_KV_HEADS * HEAD_DIM
    page_spec = lambda k: pl.BlockSpec(
        (1, 1, rows, LANES), lambda bi, p, pt: (layer, pt[bi, p * n_pg + k], 0, 0))
    grid_spec = pltpu.PrefetchScalarGridSpec(
        num_scalar_prefetch=1,
        grid=(b, n_pages // n_pg),
        in_specs=[page_spec(k) for k in range(n_pg)]
        + [pl.BlockSpec(cmp_pos.shape, lambda bi, p, pt: (0, 0))],
        out_specs=pl.BlockSpec((1, n_pg * n_blk, w), lambda bi, p, pt: (bi, p, 0)),
    )
    return pl.pallas_call(
        functools.partial(_page_means_kernel, n_pg=n_pg),
        grid_spec=grid_spec,
        out_shape=jax.ShapeDtypeStruct((b, n_pages * n_blk, w), F32),
        compiler_params=_params("parallel", "arbitrary"),
        name="page_means",
    )(page_table, *([cache] * n_pg), cmp_pos)


def _nsa_cmp_sample_kernel(qn_ref, means_ref, gkc_ref, oc_ref, sel_ref, *, pos0):
    tq = qn_ref.shape[0]
    nbp = means_ref.shape[1]
    qpos1 = pos0 + lax.broadcasted_iota(jnp.int32, (tq, 1), 0)
    qpos = jnp.concatenate([qpos1] * NSA_GROUP, axis=0)
    blk_i = lax.broadcasted_iota(jnp.int32, (1, nbp), 1)
    blk = blk_i.astype(F32)
    gw = NSA_GROUP * HEAD_DIM
    half = NSA_KV_HEADS * HEAD_DIM
    for g in range(NSA_KV_HEADS):
        qn = _stack_heads(qn_ref[:, g * gw:(g + 1) * gw], NSA_GROUP)
        kc = _head_rms(means_ref[0, :, g * HEAD_DIM:(g + 1) * HEAD_DIM], gkc_ref[...])
        vc = means_ref[0, :, half + g * HEAD_DIM:half + (g + 1) * HEAD_DIM]
        s = _bdot_nt(qn, kc) * SCALE
        p_c, _, _ = _softmax_rows(s, (blk_i + 1) * NSA_BLOCK - 1 <= qpos)
        o_c = _bdot(p_c, vc)
        for r in range(NSA_GROUP):
            oc_ref[:, g * gw + r * HEAD_DIM:g * gw + (r + 1) * HEAD_DIM] = o_c[r * tq:(r + 1) * tq]
        imp = _importance(p_c, tq, qpos1, blk)
        sel_ref[0, g * tq:(g + 1) * tq, :] = _top_n_mask(imp, blk, NSA_TOP_N)


def _nsa_cmp_sample(qn, means, g_kc, batch, pos0):
    n, w = qn.shape
    tq = n // batch
    nbp = means.shape[1]
    return pl.pallas_call(
        functools.partial(_nsa_cmp_sample_kernel, pos0=pos0),
        grid=(batch,),
        in_specs=[pl.BlockSpec((tq, w), lambda b: (b, 0)),
                  pl.BlockSpec((1, nbp, means.shape[2]), lambda b: (b, 0, 0)),
                  pl.BlockSpec((1, HEAD_DIM), lambda b: (0, 0))],
        out_specs=[pl.BlockSpec((tq, w), lambda b: (b, 0)),
                   pl.BlockSpec((1, NSA_KV_HEADS * tq, nbp), lambda b: (b, 0, 0))],
        out_shape=[jax.ShapeDtypeStruct((n, w), F32),
                   jax.ShapeDtypeStruct((batch, NSA_KV_HEADS * tq, nbp), F32)],
        compiler_params=_params("parallel"),
        name="nsa_cmp_sample",
    )(qn, means, g_kc)


def _block_diag_queries(q, n_kv, reps):
    parts = []
    for g in range(n_kv):
        for r in range(reps):
            hq = q[:, (g * reps + r) * HEAD_DIM:(g * reps + r + 1) * HEAD_DIM]
            parts.append(jnp.concatenate(
                [hq if gg == g else jnp.zeros_like(hq) for gg in range(n_kv)], axis=1))
    return jnp.concatenate(parts, axis=0).astype(BF16)


def _rep_rows(x, n_kv, per):
    return jnp.concatenate(
        [jnp.broadcast_to(x[g:g + 1, :], (per, x.shape[1])) for g in range(n_kv)], axis=0)


def _online_update(s, mask, v, m_ref, l_ref, acc_ref):
    if mask is not None:
        s = jnp.where(mask, s, NEG)
    m_old = m_ref[...]
    m_new = jnp.maximum(m_old, jnp.max(s, axis=-1, keepdims=True))
    pr = jnp.exp(s - m_new)
    if mask is not None:
        pr = jnp.where(mask, pr, 0.0)
    alpha = jnp.exp(m_old - m_new)
    l_ref[...] = alpha * l_ref[...] + jnp.sum(pr, axis=-1, keepdims=True)
    acc_ref[...] = alpha * acc_ref[...] + _bdot(pr, v)
    m_ref[...] = m_new


def _pad_rows(x, rows):
    return jnp.concatenate([x, jnp.zeros((rows - x.shape[0], x.shape[1]), x.dtype)], axis=0)


def _write_diag(o_ref, l_ref, acc_ref, n_kv, reps, tq):
    for g in range(n_kv):
        for r in range(reps):
            rs = slice((g * reps + r) * tq, (g * reps + r + 1) * tq)
            o = acc_ref[rs, g * HEAD_DIM:(g + 1) * HEAD_DIM] / jnp.maximum(l_ref[rs, :], 1e-30)
            o_ref[:, (g * reps + r) * HEAD_DIM:(g * reps + r + 1) * HEAD_DIM] = o


FOX_PAGES = 8
HEAD_SHIFT = 3


def _fox_sample_kernel(pt_ref, q_ref, *refs, n_pg):
    del pt_ref
    page_refs, lf_refs = refs[:n_pg], refs[n_pg:2 * n_pg]
    new_ref, lfn_ref, o_ref, cq_ref, carry_ref, m_ref, l_ref, acc_ref = refs[2 * n_pg:]
    p = pl.program_id(1)
    tq = q_ref.shape[0]
    page = page_refs[0].shape[2]
    rows = FOX_HEADS * tq
    row_h = jnp.concatenate([jnp.full((tq, 1), h, jnp.int32) for h in range(FOX_HEADS)], axis=0)
    q_all = _stack_heads(q_ref[...], FOX_HEADS).astype(BF16)
    pr_i = lax.broadcasted_iota(jnp.int32, (LANES, LANES), 0)
    pc_i = lax.broadcasted_iota(jnp.int32, (LANES, LANES), 1)

    @pl.when(p == 0)
    def _():
        carry_ref[...] = jnp.zeros_like(carry_ref)
        lane = lax.broadcasted_iota(jnp.int32, (1, LANES), 1)
        cnew = _exact_dot(lfn_ref[0], (pr_i <= pc_i).astype(BF16))
        cnew_rows = _rep_rows(cnew, FOX_HEADS, tq)
        row_t = _token_of_row(FOX_HEADS, tq)
        cq = jnp.sum(jnp.where(lane == row_t, cnew_rows, 0.0), axis=-1, keepdims=True)
        cq_ref[...] = cq
        kw = FOX_HEADS * HEAD_DIM
        qbd = _block_diag_queries(q_ref[...], FOX_HEADS, 1)
        kn = _pad_rows(new_ref[:, 0:kw], LANES)
        vn = _pad_rows(new_ref[:, kw:2 * kw], LANES)
        mask = lane <= row_t
        s = jnp.where(mask, _bdot_nt(qbd, kn) * SCALE + (cq - cnew_rows), NEG)
        m = jnp.max(s, axis=-1, keepdims=True)
        pr = jnp.where(mask, jnp.exp(s - m), 0.0)
        o_full = _bdot(pr, vn)
        m_ref[...] = m
        l_ref[...] = jnp.sum(pr, axis=-1, keepdims=True)
        acc_ref[...] = jnp.concatenate(
            [o_full[h * tq:(h + 1) * tq, h * HEAD_DIM:(h + 1) * HEAD_DIM] for h in range(FOX_HEADS)],
            axis=0)

    same_head = (pr_i & (FOX_HEADS - 1)) == (pc_i & (FOX_HEADS - 1))
    later_in_row = (same_head & ((pr_i >> HEAD_SHIFT) > (pc_i >> HEAD_SHIFT))).astype(BF16)
    row_total = same_head.astype(BF16)
    n_r = page * FOX_HEADS // LANES
    lane_k = lax.broadcasted_iota(jnp.int32, (1, n_pg * page * FOX_HEADS), 1)
    head_ok = (lane_k & (FOX_HEADS - 1)) == row_h
    cq = cq_ref[...]
    carry = carry_ref[...]
    s_parts, v_parts = [], []
    for k in range(n_pg):
        lf = lf_refs[k][0, 0]
        within = _exact_dot(lf, later_in_row)
        tot = _exact_dot(lf, row_total)
        run = carry
        later = [None] * n_r
        for r in reversed(range(n_r)):
            later[r] = run
            run = run + tot[r:r + 1, :]
        carry = run
        suffix = within + jnp.concatenate(later, axis=0)
        xk = page_refs[k][0, 0, :, 0].reshape(page * FOX_HEADS, HEAD_DIM)
        s_all = _bdot_nt(q_all, xk) * SCALE
        s_parts += [s_all[:, r * LANES:(r + 1) * LANES] + (cq + suffix[r:r + 1, :]) for r in range(n_r)]
        v_parts.append(page_refs[k][0, 0, :, 1].reshape(page * FOX_HEADS, HEAD_DIM))
    carry_ref[...] = carry
    s = jnp.concatenate(s_parts, axis=1)
    _online_update(s, head_ok, jnp.concatenate(v_parts, axis=0), m_ref, l_ref, acc_ref)

    @pl.when(p == pl.num_programs(1) - 1)
    def _():
        o = acc_ref[...] / jnp.maximum(l_ref[...], 1e-30)
        for h in range(FOX_HEADS):
            o_ref[:, h * HEAD_DIM:(h + 1) * HEAD_DIM] = o[h * tq:(h + 1) * tq]


def _fox_sample(qb, cache_kv, cache_lf, layer, page_table, rows_new, lf_new_t):
    n, w = qb.shape
    b, n_pages = page_table.shape
    tq = n // b
    page = cache_kv.shape[2]
    rows = FOX_HEADS * tq
    n_pg = math.gcd(FOX_PAGES, n_pages)
    assert tq <= LANES and (page * FOX_HEADS) % LANES == 0
    pg6 = lambda k: (lambda bi, p, pt: (layer, pt[bi, n_pages - 1 - (p * n_pg + k)], 0, 0, 0, 0))
    pg4 = lambda k: (lambda bi, p, pt: (layer, pt[bi, n_pages - 1 - (p * n_pg + k)], 0, 0))
    grid_spec = pltpu.PrefetchScalarGridSpec(
        num_scalar_prefetch=1,
        grid=(b, n_pages // n_pg),
        in_specs=[pl.BlockSpec((tq, w), lambda bi, p, pt: (bi, 0))]
        + [pl.BlockSpec((1, 1) + cache_kv.shape[2:], pg6(k)) for k in range(n_pg)]
        + [pl.BlockSpec((1, 1) + cache_lf.shape[2:], pg4(k)) for k in range(n_pg)]
        + [pl.BlockSpec((tq, 2 * w), lambda bi, p, pt: (bi, 0)),
           pl.BlockSpec((1, FOX_HEADS, LANES), lambda bi, p, pt: (bi, 0, 0))],
        out_specs=pl.BlockSpec((tq, w), lambda bi, p, pt: (bi, 0)),
        scratch_shapes=[pltpu.VMEM((rows, 1), F32), pltpu.VMEM((1, LANES), F32),
                        pltpu.VMEM((rows, 1), F32), pltpu.VMEM((rows, 1), F32),
                        pltpu.VMEM((rows, HEAD_DIM), F32)],
    )
    return pl.pallas_call(
        functools.partial(_fox_sample_kernel, n_pg=n_pg),
        grid_spec=grid_spec,
        out_shape=jax.ShapeDtypeStruct((n, w), F32),
        compiler_params=_params("parallel", "arbitrary"),
        name="fox_sample",
    )(page_table, qb, *([cache_kv] * n_pg), *([cache_lf] * n_pg), rows_new, lf_new_t)


SEL_PAGES = 8


def _nsa_sel_sample_kernel(pt_ref, q_ref, *refs, n_pg, pos0):
    del pt_ref
    page_refs = refs[:n_pg]
    new_ref, sel_ref, o_ref, qbd_ref, m_ref, l_ref, acc_ref = refs[n_pg:]
    p = pl.program_id(1)
    tq = q_ref.shape[0]
    page = page_refs[0].shape[2] // NSA_ROWS
    kw = NSA_KV_HEADS * HEAD_DIM
    nbp = sel_ref.shape[2]
    selm = sel_ref[0].astype(BF16)

    def key_mask(first_block, n_keys):
        blk_r = lax.broadcasted_iota(jnp.int32, (nbp, n_keys), 0)
        blk_of_lane = lax.broadcasted_iota(jnp.int32, (nbp, n_keys), 1) >> BLOCK_SHIFT
        expand = (blk_r == first_block + blk_of_lane).astype(BF16)
        km = jnp.dot(selm, expand, preferred_element_type=F32)
        return jnp.concatenate(
            [km[g * tq:(g + 1) * tq] for g in range(NSA_KV_HEADS) for _ in range(NSA_GROUP)],
            axis=0) > 0.5

    @pl.when(p == 0)
    def _():
        m_ref[...] = jnp.full_like(m_ref, NEG)
        l_ref[...] = jnp.zeros_like(l_ref)
        acc_ref[...] = jnp.zeros_like(acc_ref)
        qbd = _block_diag_queries(q_ref[...], NSA_KV_HEADS, NSA_GROUP)
        qbd_ref[...] = qbd
        row_t = _token_of_row(NSA_HEADS, tq)
        lane = lax.broadcasted_iota(jnp.int32, (1, LANES), 1)
        kn = _pad_rows(new_ref[:, 0:kw], LANES)
        vn = _pad_rows(new_ref[:, kw:2 * kw], LANES)
        s = _bdot_nt(qbd, kn) * SCALE
        _online_update(s, key_mask(pos0 // NSA_BLOCK, LANES) & (lane <= row_t), vn,
                       m_ref, l_ref, acc_ref)

    def slab(typ):
        return jnp.concatenate(
            [jnp.concatenate([_page_rows(page_refs[k], typ, g, 0, page) for g in range(NSA_KV_HEADS)],
                             axis=1) for k in range(n_pg)], axis=0)

    s = _bdot_nt(qbd_ref[...], slab(2)) * SCALE
    _online_update(s, key_mask(p * (n_pg * page // NSA_BLOCK), n_pg * page), slab(3),
                   m_ref, l_ref, acc_ref)

    @pl.when(p == pl.num_programs(1) - 1)
    def _():
        _write_diag(o_ref, l_ref, acc_ref, NSA_KV_HEADS, NSA_GROUP, tq)


def _nsa_sel_sample(qr, cache, layer, page_table, rows_new, sel, pos0):
    n, w = qr.shape
    b, n_pages = page_table.shape
    tq = n // b
    rows_pp = cache.shape[2]
    page = rows_pp // NSA_ROWS
    kvw = 2 * NSA_KV_HEADS * HEAD_DIM
    rows = NSA_HEADS * tq
    n_pg = math.gcd(SEL_PAGES, n_pages)
    assert pos0 == n_pages * page and pos0 % NSA_BLOCK == 0 and tq <= NSA_BLOCK
    page_spec = lambda k: pl.BlockSpec(
        (1, 1, rows_pp, LANES), lambda bi, p, pt: (layer, pt[bi, p * n_pg + k], 0, 0))
    grid_spec = pltpu.PrefetchScalarGridSpec(
        num_scalar_prefetch=1,
        grid=(b, n_pages // n_pg),
        in_specs=[pl.BlockSpec((tq, w), lambda bi, p, pt: (bi, 0))]
        + [page_spec(k) for k in range(n_pg)]
        + [pl.BlockSpec((tq, kvw), lambda bi, p, pt: (bi, 1)),
           pl.BlockSpec((1,) + sel.shape[1:], lambda bi, p, pt: (bi, 0, 0))],
        out_specs=pl.BlockSpec((tq, w), lambda bi, p, pt: (bi, 0)),
        scratch_shapes=[pltpu.VMEM((rows, kvw // 2), BF16), pltpu.VMEM((rows, 1), F32),
                        pltpu.VMEM((rows, 1), F32), pltpu.VMEM((rows, kvw // 2), F32)],
    )
    return pl.pallas_call(
        functools.partial(_nsa_sel_sample_kernel, n_pg=n_pg, pos0=pos0),
        grid_spec=grid_spec,
        out_shape=jax.ShapeDtypeStruct((n, w), F32),
        compiler_params=_params("parallel", "arbitrary"),
        name="nsa_sel_sample",
    )(page_table, qr, *([cache] * n_pg), rows_new, sel)


def _nsa_win_sample_kernel(qr_ref, win_ref, new_ref, oc_ref, os_ref, gate_ref, gbias_ref,
                           o_ref, nwin_ref, kv_ref):
    tq = qr_ref.shape[0]
    lw = win_ref.shape[1]
    nk = kv_ref.shape[0]
    kw = NSA_KV_HEADS * HEAD_DIM
    gw = NSA_GROUP * HEAD_DIM
    kv_ref[0:lw, :] = win_ref[0]
    kv_ref[lw:nk, :] = _pad_rows(new_ref[...], nk - lw)
    nwin_ref[0, 0:lw - tq, :] = win_ref[0, tq:lw, :]
    nwin_ref[0, lw - tq:lw, :] = new_ref[...]

    idx = lax.broadcasted_iota(jnp.int32, (1, nk), 1)
    row_t = _token_of_row(NSA_GROUP, tq)
    dist = lw + row_t - idx
    mask = (dist >= 0) & (dist < NSA_WINDOW) & (idx < lw + tq)
    gates = _sigmoid(gate_ref[...] + gbias_ref[...])
    for g in range(NSA_KV_HEADS):
        qr = _stack_heads(qr_ref[:, g * gw:(g + 1) * gw], NSA_GROUP)
        s = _bdot_nt(qr, kv_ref[:, g * HEAD_DIM:(g + 1) * HEAD_DIM]) * SCALE
        p_w, _, _ = _softmax_rows(s, mask)
        o_w = _bdot(p_w, kv_ref[:, kw + g * HEAD_DIM:kw + (g + 1) * HEAD_DIM])
        for r in range(NSA_GROUP):
            cs = slice(g * gw + r * HEAD_DIM, g * gw + (r + 1) * HEAD_DIM)
            gc = g * LANES + 3 * r
            o_ref[:, cs] = (gates[:, gc:gc + 1] * oc_ref[:, cs] + gates[:, gc + 1:gc + 2] * os_ref[:, cs]
                            + gates[:, gc + 2:gc + 3] * o_w[r * tq:(r + 1) * tq])


def _nsa_win_sample(qr, win, rows_win, o_c, o_s, gates, gbias, batch):
    n, w = qr.shape
    tq = n // batch
    lw, ww = win.shape[1], win.shape[2]
    nk = lw + LANES
    assert lw >= NSA_WINDOW and tq % 8 == 0 and tq <= LANES
    row = lambda a: pl.BlockSpec((tq, a.shape[1]), lambda b: (b, 0))
    return pl.pallas_call(
        _nsa_win_sample_kernel,
        grid=(batch,),
        in_specs=[row(qr), pl.BlockSpec((1, lw, ww), lambda b: (b, 0, 0)), row(rows_win),
                  row(o_c), row(o_s), row(gates), pl.BlockSpec(gbias.shape, lambda b: (0, 0))],
        out_specs=[row(qr), pl.BlockSpec((1, lw, ww), lambda b: (b, 0, 0))],
        out_shape=[jax.ShapeDtypeStruct((n, w), F32), jax.ShapeDtypeStruct(win.shape, F32)],
        scratch_shapes=[pltpu.VMEM((nk, ww), F32)],
        compiler_params=_params("parallel"),
        name="nsa_win_sample",
    )(qr, win, rows_win, o_c, o_s, gates, gbias)


DIL_CHUNK = 256
DIL_HEAD_SHIFT = 4


def _dil_sample_kernel(q_ref, st_ref, new_ref, o_ref, m_ref, l_ref, acc_ref, *, lw):
    c = pl.program_id(1)
    n_chunks = pl.num_programs(1) - 1
    tq = q_ref.shape[0]
    rows = DIL_HEADS * tq
    n_g = len(DIL_PAIRS)
    q_all = _stack_heads(q_ref[...], DIL_HEADS).astype(BF16)
    row_t = _token_of_row(DIL_HEADS, tq)
    row_h = jnp.concatenate([jnp.full((tq, 1), h, jnp.int32) for h in range(DIL_HEADS)], axis=0)

    @pl.when(c == 0)
    def _():
        m_ref[...] = jnp.full_like(m_ref, NEG)
        l_ref[...] = jnp.zeros_like(l_ref)
        acc_ref[...] = jnp.zeros_like(acc_ref)

    def process(k_rows, v_rows, j0, first_chunk):
        n_l = k_rows.shape[0] * DIL_HEADS
        xk = k_rows[...].reshape(n_l, HEAD_DIM)
        s = _bdot_nt(q_all, xk) * SCALE
        lane = lax.broadcasted_iota(jnp.int32, (1, n_l), 1)
        head_ok = (lane & (DIL_HEADS - 1)) == row_h
        off = lw + row_t - (j0 + (lane >> DIL_HEAD_SHIFT))

        def group(gi, window, dil):
            mask = head_ok & (off >= 0) & (off <= window) & ((off & (dil - 1)) == 0)
            sm = jnp.where(mask, s, NEG)
            m_old = m_ref[gi]
            m_new = jnp.maximum(m_old, jnp.max(sm, axis=-1, keepdims=True))
            pr = jnp.where(mask, jnp.exp(sm - m_new), 0.0)
            alpha = jnp.exp(m_old - m_new)
            l_ref[gi] = alpha * l_ref[gi] + jnp.sum(pr, axis=-1, keepdims=True)
            acc_ref[gi] = alpha * acc_ref[gi] + _bdot(pr, v_rows[...].reshape(n_l, HEAD_DIM))
            m_ref[gi] = m_new

        for gi, (window, dil) in enumerate(DIL_PAIRS):
            pl.when(c >= first_chunk(window))(functools.partial(group, gi, window, dil))

    @pl.when(c < n_chunks)
    def _():
        n_pos = st_ref.shape[2]
        process(st_ref.at[0, 0, :, 0], st_ref.at[0, 0, :, 1], c * n_pos,
                lambda window: max(lw - window, 0) // n_pos)

    @pl.when(c == n_chunks)
    def _():
        process(new_ref.at[0, :, 0], new_ref.at[0, :, 1], lw, lambda window: 0)
        lses = [m_ref[gi] + jnp.log(l_ref[gi]) for gi in range(n_g)]
        mx = jnp.maximum(jnp.maximum(lses[0], lses[1]), lses[2])
        ws = [jnp.exp(l - mx) for l in lses]
        tot = ws[0] + ws[1] + ws[2]
        o = sum((ws[gi] / tot) * (acc_ref[gi] / l_ref[gi]) for gi in range(n_g))
        for h in range(DIL_HEADS):
            o_ref[:, h * HEAD_DIM:(h + 1) * HEAD_DIM] = o[h * tq:(h + 1) * tq]


def _dil_sample(q, state, layer, rows_new, batch):
    n, w = q.shape
    tq = n // batch
    lw = state.shape[2]
    assert lw >= DIL_PAIRS[-1][0] and lw % DIL_CHUNK == 0
    n_chunks = lw // DIL_CHUNK
    rows = DIL_HEADS * tq
    n_g = len(DIL_PAIRS)
    return pl.pallas_call(
        functools.partial(_dil_sample_kernel, lw=lw),
        grid=(batch, n_chunks + 1),
        in_specs=[pl.BlockSpec((tq, w), lambda b, c: (b, 0)),
                  pl.BlockSpec((1, 1, DIL_CHUNK) + state.shape[3:],
                               lambda b, c: (layer, b, jnp.minimum(c, n_chunks - 1), 0, 0, 0)),
                  pl.BlockSpec((1,) + rows_new.shape[1:], lambda b, c: (b, 0, 0, 0, 0))],
        out_specs=pl.BlockSpec((tq, w), lambda b, c: (b, 0)),
        out_shape=jax.ShapeDtypeStruct((n, w), F32),
        scratch_shapes=[pltpu.VMEM((n_g, rows, 1), F32), pltpu.VMEM((n_g, rows, 1), F32),
                        pltpu.VMEM((n_g, rows, HEAD_DIM), F32)],
        compiler_params=_params("parallel", "arbitrary"),
        name="dil_sample",
    )(q, state, rows_new)


def _state_shift_kernel(st_ref, new_ref, out_ref):
    lw, tq = st_ref.shape[0], new_ref.shape[0]
    out_ref[0:lw - tq] = st_ref[tq:lw]
    out_ref[lw - tq:lw] = new_ref[...]


def _state_shift(state, rows_new):
    n_l, b, lw, n_s = state.shape[:4]
    tq = rows_new.shape[2]
    tile = state.shape[4:]
    spec = lambda rows: pl.BlockSpec((None, None, rows, None) + tile,
                                     lambda li, bi, s: (li, bi, 0, s, 0, 0))
    return pl.pallas_call(
        _state_shift_kernel,
        grid=(n_l, b, n_s),
        in_specs=[spec(lw), spec(tq)],
        out_specs=spec(lw),
        out_shape=jax.ShapeDtypeStruct(state.shape, F32),
        compiler_params=_params("parallel", "parallel", "arbitrary"),
        name="state_shift",
    )(state, rows_new)


def _rope_tables(pos):
    half = HEAD_DIM // 2
    inv = ROPE_THETA ** (-jnp.arange(half, dtype=F32) / half)
    ang = pos.astype(F32)[:, None] * inv[None, :]
    cos, sin = jnp.cos(ang), jnp.sin(ang)
    return jnp.concatenate([cos, cos], axis=-1), jnp.concatenate([-sin, sin], axis=-1)


def _pad_lanes(a, width=LANES):
    return jnp.pad(a, ((0, 0),) * (a.ndim - 1) + ((0, width - a.shape[-1]),))


def _chunks(first, count, op="copy", gi=0):
    return [(first + c, op, gi) for c in range(count)]


NSA_W = NSA_HEADS * HEAD_DIM
NSA_KV_W = NSA_KV_HEADS * HEAD_DIM
FOX_W = FOX_HEADS * HEAD_DIM
DIL_W = DIL_HEADS * HEAD_DIM
GATES_PER_GROUP = 3 * NSA_GROUP


def _even_weights(w_in, w_out, gate_b, forget_b, nsa_g, fox_g):
    cuts = [0]
    for wd in ((NSA_W,) + (NSA_KV_W,) * 6
               + (3 * NSA_HEADS, NSA_W, FOX_W, FOX_W, FOX_W, FOX_HEADS, FOX_W)):
        cuts.append(cuts[-1] + wd)
    seg = lambda a, b: w_in[:, cuts[a]:cuts[b]]
    gate_w, f_w = seg(7, 8), seg(12, 13)
    small = [_pad_lanes(f_w)] + [_pad_lanes(gate_w[:, g * GATES_PER_GROUP:(g + 1) * GATES_PER_GROUP])
                                 for g in range(NSA_KV_HEADS)]
    gbias = jnp.concatenate(
        [_pad_lanes(gate_b[None, g * GATES_PER_GROUP:(g + 1) * GATES_PER_GROUP])
         for g in range(NSA_KV_HEADS)], axis=1)
    aux = jnp.stack([nsa_g[0], nsa_g[2], nsa_g[3], fox_g[0], fox_g[1], _pad_lanes(forget_b),
                     jnp.zeros((LANES,), F32), jnp.zeros((LANES,), F32)])
    return dict(
        w_a=seg(0, 7).astype(BF16),
        w_b=seg(8, 10).astype(BF16),
        w_c=seg(10, 12).astype(BF16),
        w_d=jnp.concatenate([seg(13, 14)] + small, axis=1).astype(BF16),
        w_out_a=w_out[:NSA_W].astype(BF16), w_out_b=w_out[NSA_W:].astype(BF16),
        aux=aux, gbias=gbias, g_kc=nsa_g[1][None, :])


def _even_projections(x, g_norm, wts, pos):
    h = _rmsnorm(x, g_norm)
    cos2, sin2 = _rope_tables(pos)
    nq, nkv = NSA_HEADS, NSA_KV_HEADS
    qn, qr, rows_nsa, rows_win = _proj(
        h, wts["w_a"], wts["aux"], cos2, sin2,
        [[_chunks(0, nq, "rms", 0)], [_chunks(0, nq, "rms_rope", 0)],
         [_chunks(nq, 2 * nkv) + _chunks(nq + 2 * nkv, nkv, "rms_rope", 1) + _chunks(nq + 3 * nkv, nkv)],
         [_chunks(nq + 4 * nkv, nkv, "rms_rope", 2) + _chunks(nq + 5 * nkv, nkv)]],
        wts["w_a"].shape[1])
    z_a, qb = _proj(h, wts["w_b"], wts["aux"], cos2, sin2,
                    [[_chunks(0, nq)], [_chunks(nq, FOX_HEADS, "rms", 3)]], wts["w_b"].shape[1])
    rows_fox, = _proj(h, wts["w_c"], wts["aux"], cos2, sin2,
                      [[_chunks(0, FOX_HEADS, "rms", 4) + _chunks(FOX_HEADS, FOX_HEADS)]],
                      wts["w_c"].shape[1])
    z_b, logf, gates = _proj(
        h, wts["w_d"], wts["aux"], cos2, sin2,
        [[_chunks(0, FOX_HEADS)], [_chunks(FOX_HEADS, 1, "logsig", 5)], [_chunks(FOX_HEADS + 1, nkv)]],
        wts["w_d"].shape[1])
    return qn, qr, rows_nsa, rows_win, z_a, qb, rows_fox, z_b, logf, gates


def _even_prompt(x, batch, g_norm, wts, cmp_pos):
    n = x.shape[0]
    t = n // batch
    pos = jnp.tile(jnp.arange(t, dtype=jnp.int32), batch)
    qn, qr, rows_nsa, rows_win, z_a, qb, rows_fox, z_b, logf, gates = _even_projections(
        x, g_norm, wts, pos)
    c_all = _cumsum_rows(logf, batch)
    ck_t = jnp.transpose(c_all[:, :FOX_HEADS].reshape(batch, t, FOX_HEADS), (0, 2, 1))
    o_b = _fox_prompt(qb, rows_fox, c_all, ck_t, batch)
    o_a = _nsa_prompt(qn, qr, rows_nsa, rows_win, gates, wts["gbias"], cmp_pos, wts["g_kc"], batch)
    y = _outproj(x, [o_a, o_b], [z_a, z_b], [wts["w_out_a"], wts["w_out_b"]])
    lw = min(NSA_WINDOW, t)
    new_win = rows_win.reshape(batch, t, 2, NSA_KV_HEADS, HEAD_DIM)[:, t - lw:]
    return y, (rows_nsa.reshape(batch, t, 4, NSA_KV_HEADS, HEAD_DIM),
               rows_fox.reshape(batch, t, 2, FOX_HEADS, HEAD_DIM),
               logf[:, :FOX_HEADS].reshape(batch, t, FOX_HEADS), new_win)


def _even_sample(x, batch, pos0, g_norm, wts, cmp_pos, layer, cache_nsa, cache_fox, cache_lf,
                 win_state, page_table):
    n = x.shape[0]
    tq = n // batch
    pos = jnp.tile(pos0 + jnp.arange(tq, dtype=jnp.int32), batch)
    qn, qr, rows_nsa, rows_win, z_a, qb, rows_fox, z_b, logf, gates = _even_projections(
        x, g_norm, wts, pos)
    means = _page_means(cache_nsa, layer, page_table, cmp_pos)
    n_blocks = means.shape[1]
    nbp = -(-(n_blocks + 1) // LANES) * LANES
    means = jnp.pad(means, ((0, 0), (0, nbp - n_blocks), (0, 0)))
    o_c, sel = _nsa_cmp_sample(qn, means, wts["g_kc"], batch, pos0)
    o_s = _nsa_sel_sample(qr, cache_nsa, layer, page_table, rows_nsa, sel, pos0)
    lw = win_state.shape[1]
    win = win_state.reshape(batch, lw, 2 * NSA_KV_W)
    o_a, new_win = _nsa_win_sample(qr, win, rows_win, o_c, o_s, gates, wts["gbias"], batch)
    lf_new_t = _pad_lanes(
        jnp.transpose(logf[:, :FOX_HEADS].reshape(batch, tq, FOX_HEADS), (0, 2, 1)))
    o_b = _fox_sample(qb, cache_fox, cache_lf, layer, page_table, rows_fox, lf_new_t)
    y = _outproj(x, [o_a, o_b], [z_a, z_b], [wts["w_out_a"], wts["w_out_b"]])
    return y, (rows_nsa.reshape(batch, tq, 4, NSA_KV_HEADS, HEAD_DIM),
               rows_fox.reshape(batch, tq, 2, FOX_HEADS, HEAD_DIM),
               logf[:, :FOX_HEADS].reshape(batch, tq, FOX_HEADS),
               new_win.reshape(batch, lw, 2, NSA_KV_HEADS, HEAD_DIM))


def _odd_weights(w_in, w_out, qk_g):
    aux = jnp.concatenate([qk_g, jnp.zeros((6, HEAD_DIM), F32)], axis=0)
    return dict(w_q=w_in[:, :DIL_W].astype(BF16), w_kv=w_in[:, DIL_W:3 * DIL_W].astype(BF16),
                w_z=w_in[:, 3 * DIL_W:].astype(BF16), w_out=w_out.astype(BF16), aux=aux)


def _odd_projections(x, g_norm, wts, pos, head_major_rows=False):
    h = _rmsnorm(x, g_norm)
    cos2, sin2 = _rope_tables(pos)
    q, = _proj(h, wts["w_q"], wts["aux"], cos2, sin2, [[_chunks(0, DIL_HEADS, "rms_rope", 0)]], DIL_W)
    rows, = _proj(h, wts["w_kv"], wts["aux"], cos2, sin2,
                  [[_chunks(0, DIL_HEADS, "rms_rope", 1), _chunks(0, DIL_HEADS)]], DIL_W,
                  head_major=[head_major_rows])
    z, = _proj(h, wts["w_z"], wts["aux"], cos2, sin2, [[_chunks(0, DIL_HEADS)]], DIL_W)
    return q, rows, z


def _odd_prompt(x, batch, g_norm, wts):
    n = x.shape[0]
    t = n // batch
    pos = jnp.tile(jnp.arange(t, dtype=jnp.int32), batch)
    q, rows, z = _odd_projections(x, g_norm, wts, pos)
    o = _dil_prompt(q, rows, batch)
    y = _outproj(x, [o], [z], [wts["w_out"]])
    lw = min(DIL_PAIRS[-1][0], t)
    return y, rows.reshape(batch, t, 2, DIL_HEADS, HEAD_DIM)[:, t - lw:]


STATE_TILE = 8


def _odd_sample(x, batch, pos0, g_norm, wts, state_all, layer):
    n = x.shape[0]
    tq = n // batch
    pos = jnp.tile(pos0 + jnp.arange(tq, dtype=jnp.int32), batch)
    q, rows, z = _odd_projections(x, g_norm, wts, pos, head_major_rows=True)
    o = _dil_sample(q, state_all, layer, rows.reshape(batch, tq, 2, DIL_HEADS, HEAD_DIM), batch)
    y = _outproj(x, [o], [z], [wts["w_out"]])
    return y, rows


def kernel(x_prompt, x_sample, cache_nsa_kv, cache_fox_kv, cache_fox_logf, state_nsa_win_kv, state_dil_kv, page_table, norm_even, w_in_even, w_out_even, nsa_gate_bias, fox_forget_bias, nsa_cmp_pos, nsa_qk_gain, fox_qk_gain, norm_odd, w_in_odd, w_out_odd, dil_qk_gain):
    bp, t, d = x_prompt.shape
    bs, tq, _ = x_sample.shape
    n_layers, n_pool, page = cache_nsa_kv.shape[:3]
    pos0 = page_table.shape[1] * page
    depth = norm_even.shape[0] + norm_odd.shape[0]
    cache_nsa = cache_nsa_kv.reshape(n_layers, n_pool, page * NSA_ROWS, HEAD_DIM)
    cache_lf = cache_fox_logf.reshape(n_layers, n_pool, page * FOX_HEADS // LANES, LANES)
    yp = x_prompt.reshape(bp * t, d)
    ys = x_sample.reshape(bs * tq, d)
    outs = [[] for _ in range(9)]
    dil_rows = []
    for layer in range(depth):
        i = layer // 2
        if layer % 2 == 0:
            wts = _even_weights(w_in_even[i], w_out_even[i], nsa_gate_bias[i], fox_forget_bias[i],
                                nsa_qk_gain[i], fox_qk_gain[i])
            yp, new_p = _even_prompt(yp, bp, norm_even[i], wts, nsa_cmp_pos[i])
            ys, new_s = _even_sample(ys, bs, pos0, norm_even[i], wts, nsa_cmp_pos[i], i, cache_nsa,
                                     cache_fox_kv, cache_lf, state_nsa_win_kv[i], page_table)
            for k in range(4):
                outs[2 * k].append(new_p[k])
                outs[2 * k + 1].append(new_s[k])
        else:
            wts = _odd_weights(w_in_odd[i], w_out_odd[i], dil_qk_gain[i])
            yp, buf_p = _odd_prompt(yp, bp, norm_odd[i], wts)
            ys, rows_s = _odd_sample(ys, bs, pos0, norm_odd[i], wts, state_dil_kv, i)
            outs[8].append(buf_p)
            dil_rows.append(rows_s)
    n_odd, _, lw = state_dil_kv.shape[:3]
    n_s = 2 * DIL_HEADS // STATE_TILE
    dil_s = _state_shift(state_dil_kv.reshape(n_odd, bs, lw, n_s, STATE_TILE, HEAD_DIM),
                         jnp.stack(dil_rows).reshape(n_odd, bs, tq, n_s, STATE_TILE, HEAD_DIM))
    return ((yp.reshape(bp, t, d), ys.reshape(bs, tq, d)) + tuple(jnp.stack(o) for o in outs)
            + (dil_s.reshape(state_dil_kv.shape),))
```

```python
import functools
import math

import jax
import jax.numpy as jnp
from jax import lax
from jax.experimental import pallas as pl
from jax.experimental.pallas import tpu as pltpu

F32 = jnp.float32
BF16 = jnp.bfloat16

HEAD_DIM = 128
LANES = 128
NSA_HEADS = 8
NSA_KV_HEADS = 2
NSA_GROUP = NSA_HEADS // NSA_KV_HEADS
NSA_BLOCK = 64
NSA_TOP_N = 16
NSA_WINDOW = 512
NSA_FORCE_SCORE = 1.0e4
FOX_HEADS = 8
DIL_HEADS = 16
DIL_PAIRS = ((128, 1), (512, 4), (2048, 16))
ROPE_THETA = 10000.0
EPS = 1e-6
SCALE = HEAD_DIM ** -0.5
NEG = -1.0e30
LOG2E = math.log2(math.e)
VMEM_LIMIT = 56 * 1024 * 1024
MATMUL_ROWS = 512


def _params(*sem):
    return pltpu.CompilerParams(dimension_semantics=sem, vmem_limit_bytes=VMEM_LIMIT)


def _bdot(a, b):
    return jnp.dot(a.astype(BF16), b.astype(BF16), preferred_element_type=F32)


def _bdot_nt(a, b):
    return lax.dot_general(a.astype(BF16), b.astype(BF16), (((1,), (1,)), ((), ())),
                           preferred_element_type=F32)


def _split3(x):
    hi = x.astype(BF16)
    r1 = x - hi.astype(F32)
    mid = r1.astype(BF16)
    lo = (r1 - mid.astype(F32)).astype(BF16)
    return hi, mid, lo


def _exact_dot(x, m):
    hi, mid, lo = _split3(x)
    return (jnp.dot(hi, m, preferred_element_type=F32)
            + jnp.dot(mid, m, preferred_element_type=F32)
            + jnp.dot(lo, m, preferred_element_type=F32))


def _exact_dot_left(m, x):
    hi, mid, lo = _split3(x)
    return (jnp.dot(m, hi, preferred_element_type=F32)
            + jnp.dot(m, mid, preferred_element_type=F32)
            + jnp.dot(m, lo, preferred_element_type=F32))


def _token_of_row(n_rep, tq):
    return jnp.concatenate([lax.broadcasted_iota(jnp.int32, (tq, 1), 0)] * n_rep, axis=0)


BLOCK_SHIFT = 6


def _sigmoid(x):
    return 1.0 / (1.0 + jnp.exp(-x))


def _log_sigmoid(x):
    return -(jnp.maximum(-x, 0.0) + jnp.log(1.0 + jnp.exp(-jnp.abs(x))))


def _head_rms(y, g):
    return y * lax.rsqrt(jnp.mean(y * y, axis=-1, keepdims=True) + EPS) * g


def _softmax_rows(s, mask):
    s = jnp.where(mask, s, -jnp.inf)
    m = jnp.max(s, axis=-1, keepdims=True)
    m = jnp.where(m > -jnp.inf, m, 0.0)
    e = jnp.exp(s - m)
    den = jnp.sum(e, axis=-1, keepdims=True)
    return e / jnp.maximum(den, 1e-30), m, den


def _rmsnorm_kernel(x_ref, g_ref, h_ref):
    x = x_ref[...]
    y = x * lax.rsqrt(jnp.mean(x * x, axis=-1, keepdims=True) + EPS)
    h_ref[...] = (y * g_ref[...]).astype(BF16)


def _rmsnorm(x, g):
    n, d = x.shape
    tm = min(n, 512)
    return pl.pallas_call(
        _rmsnorm_kernel,
        grid=(n // tm,),
        in_specs=[pl.BlockSpec((tm, d), lambda i: (i, 0)),
                  pl.BlockSpec((1, d), lambda i: (0, 0))],
        out_specs=pl.BlockSpec((tm, d), lambda i: (i, 0)),
        out_shape=jax.ShapeDtypeStruct((n, d), BF16),
        compiler_params=_params("parallel"),
        name="rmsnorm",
    )(x, g.reshape(1, d))


def _proj_kernel(h_ref, w_ref, aux_ref, cos_ref, sin_ref, *out_refs, plans, n_tiles, head_major):
    acc = jnp.dot(h_ref[...], w_ref[...], preferred_element_type=F32)
    tm = h_ref.shape[0]

    def emit(tile):
        for out_ref, plan, hm in zip(out_refs, plans, head_major):
            n_c = len(plan[tile])
            for c, (src, op, gi) in enumerate(plan[tile]):
                y = acc[:, src * LANES:(src + 1) * LANES]
                if op in ("rms", "rms_rope"):
                    y = _head_rms(y, aux_ref[gi:gi + 1, :])
                if op == "rms_rope":
                    y = y * cos_ref[...] + pltpu.roll(y, HEAD_DIM // 2, 1) * sin_ref[...]
                if op == "logsig":
                    y = _log_sigmoid(y + aux_ref[gi:gi + 1, :])
                if hm:
                    out_ref[pl.ds(tile * n_c + c, tm, stride=n_tiles * n_c), :] = y
                else:
                    out_ref[:, c * LANES:(c + 1) * LANES] = y

    if n_tiles == 1:
        emit(0)
    else:
        for t in range(n_tiles):
            pl.when(pl.program_id(0) == t)(functools.partial(emit, t))


def _proj(h, w, aux, cos2, sin2, plans, tn, head_major=None):
    n, k = h.shape
    n_tiles = w.shape[1] // tn
    tm = min(n, MATMUL_ROWS)
    head_major = head_major or [False] * len(plans)
    out_specs, out_shape = [], []
    for plan, hm in zip(plans, head_major):
        n_c = len(plan[0])
        if hm:
            assert n_tiles == 1 or n == tm
            out_specs.append(pl.BlockSpec((tm * n_tiles * n_c, LANES), lambda j, i: (i, 0)))
            out_shape.append(jax.ShapeDtypeStruct((n * n_tiles * n_c, LANES), F32))
        else:
            out_specs.append(pl.BlockSpec((tm, n_c * LANES), lambda j, i: (i, j)))
            out_shape.append(jax.ShapeDtypeStruct((n, n_tiles * n_c * LANES), F32))
    kern = functools.partial(_proj_kernel, plans=plans, n_tiles=n_tiles, head_major=head_major)
    outs = pl.pallas_call(
        kern,
        grid=(n_tiles, n // tm),
        in_specs=[pl.BlockSpec((tm, k), lambda j, i: (i, 0)),
                  pl.BlockSpec((k, tn), lambda j, i: (0, j), pipeline_mode=pl.Buffered(1)),
                  pl.BlockSpec(aux.shape, lambda j, i: (0, 0)),
                  pl.BlockSpec((tm, LANES), lambda j, i: (i, 0)),
                  pl.BlockSpec((tm, LANES), lambda j, i: (i, 0))],
        out_specs=out_specs,
        out_shape=out_shape,
        compiler_params=_params("arbitrary", "arbitrary"),
        name="proj",
    )(h, w, aux, cos2, sin2)
    return outs


def _outproj_kernel(*refs, n_parts):
    x_ref = refs[0]
    y_ref = refs[-1]
    acc = x_ref[...]
    for p in range(n_parts):
        o = refs[1 + p][...]
        z = refs[1 + n_parts + p][...]
        mix = o * (z * _sigmoid(z))
        acc = acc + jnp.dot(mix.astype(BF16), refs[1 + 2 * n_parts + p][...],
                            preferred_element_type=F32)
    y_ref[...] = acc


def _outproj(x, o_parts, z_parts, w_parts):
    n, d = x.shape
    tm = min(n, MATMUL_ROWS)
    n_parts = len(o_parts)
    row = lambda a: pl.BlockSpec((tm, a.shape[1]), lambda i: (i, 0))
    return pl.pallas_call(
        functools.partial(_outproj_kernel, n_parts=n_parts),
        grid=(n // tm,),
        in_specs=([row(x)] + [row(o) for o in o_parts] + [row(z) for z in z_parts]
                  + [pl.BlockSpec(w.shape, lambda i: (0, 0), pipeline_mode=pl.Buffered(1))
                     for w in w_parts]),
        out_specs=row(x),
        out_shape=jax.ShapeDtypeStruct((n, d), F32),
        compiler_params=_params("parallel"),
        name="outproj",
    )(x, *o_parts, *z_parts, *w_parts)


CUM_BLOCK = 256


def _cumsum_kernel(x_ref, tri_ref, c_ref, carry_ref):
    @pl.when(pl.program_id(1) == 0)
    def _():
        carry_ref[...] = jnp.zeros_like(carry_ref)

    c = _exact_dot_left(tri_ref[...], x_ref[...]) + carry_ref[...]
    c_ref[...] = c
    carry_ref[...] = c[CUM_BLOCK - 1:CUM_BLOCK, :]


def _cumsum_rows(x, batch):
    n, w = x.shape
    t = n // batch
    nb = t // CUM_BLOCK
    r = lax.broadcasted_iota(jnp.int32, (CUM_BLOCK, CUM_BLOCK), 0)
    c = lax.broadcasted_iota(jnp.int32, (CUM_BLOCK, CUM_BLOCK), 1)
    tri = (c <= r).astype(BF16)
    return pl.pallas_call(
        _cumsum_kernel,
        grid=(batch, nb),
        in_specs=[pl.BlockSpec((CUM_BLOCK, w), lambda b, j: (b * nb + j, 0)),
                  pl.BlockSpec((CUM_BLOCK, CUM_BLOCK), lambda b, j: (0, 0))],
        out_specs=pl.BlockSpec((CUM_BLOCK, w), lambda b, j: (b * nb + j, 0)),
        out_shape=jax.ShapeDtypeStruct((n, w), F32),
        scratch_shapes=[pltpu.VMEM((1, w), F32)],
        compiler_params=_params("parallel", "arbitrary"),
        name="logf_cumsum",
    )(x, tri)


FOX_TILE = 512


def _lanes(col):
    return jnp.broadcast_to(col, (col.shape[0], LANES))


def _fox_prompt_kernel(q_ref, k_ref, v_ref, cq_ref, ck_ref, o_ref, m_ref, l_ref, acc_ref, cq2_ref):
    i = pl.program_id(1)
    j = pl.program_id(2)
    tq = q_ref.shape[0]
    tk = k_ref.shape[0]

    @pl.when(j == 0)
    def _():
        m_ref[...] = jnp.full_like(m_ref, NEG)
        l_ref[...] = jnp.zeros_like(l_ref)
        acc_ref[...] = jnp.zeros_like(acc_ref)
        for h in range(FOX_HEADS):
            cq2_ref[h] = _lanes(cq_ref[:, h:h + 1] * LOG2E)

    def step(causal):
        if causal:
            mask = (lax.broadcasted_iota(jnp.int32, (tq, tk), 0)
                    >= lax.broadcasted_iota(jnp.int32, (tq, tk), 1))
        for h in range(FOX_HEADS):
            sl = slice(h * HEAD_DIM, (h + 1) * HEAD_DIM)
            u = _bdot_nt(q_ref[:, sl], k_ref[:, sl]) * (SCALE * LOG2E) - ck_ref[0, h:h + 1, :] * LOG2E
            if causal:
                u = jnp.where(mask, u, NEG)
            cq2 = cq2_ref[h]
            m_old = m_ref[h]
            m_new = jnp.maximum(m_old, _lanes(jnp.max(u, axis=-1, keepdims=True)) + cq2)
            p = jnp.exp2(u - jnp.tile(m_new - cq2, (1, tk // LANES)))
            alpha = jnp.exp2(m_old - m_new)
            l_ref[h] = alpha * l_ref[h] + _lanes(jnp.sum(p, axis=-1, keepdims=True))
            acc_ref[:, sl] = alpha * acc_ref[:, sl] + _bdot(p, v_ref[:, sl])
            m_ref[h] = m_new

    pl.when(j < i)(functools.partial(step, False))
    pl.when(j == i)(functools.partial(step, True))

    @pl.when(j == i)
    def _():
        for h in range(FOX_HEADS):
            sl = slice(h * HEAD_DIM, (h + 1) * HEAD_DIM)
            o_ref[:, sl] = acc_ref[:, sl] / jnp.maximum(l_ref[h], 1e-30)


def _fox_prompt(qb, rows_fox, c_all, ck_t, batch):
    n, w = qb.shape
    t = n // batch
    tq = min(FOX_TILE, t)
    nq = t // tq
    return pl.pallas_call(
        _fox_prompt_kernel,
        grid=(batch, nq, nq),
        in_specs=[pl.BlockSpec((tq, w), lambda b, i, j: (b * nq + i, 0)),
                  pl.BlockSpec((tq, w), lambda b, i, j: (b * nq + jnp.minimum(j, i), 0)),
                  pl.BlockSpec((tq, w), lambda b, i, j: (b * nq + jnp.minimum(j, i), 1)),
                  pl.BlockSpec((tq, LANES), lambda b, i, j: (b * nq + i, 0)),
                  pl.BlockSpec((1, FOX_HEADS, tq), lambda b, i, j: (b, 0, jnp.minimum(j, i)))],
        out_specs=pl.BlockSpec((tq, w), lambda b, i, j: (b * nq + i, 0)),
        out_shape=jax.ShapeDtypeStruct((n, w), F32),
        scratch_shapes=[pltpu.VMEM((FOX_HEADS, tq, LANES), F32),
                        pltpu.VMEM((FOX_HEADS, tq, LANES), F32),
                        pltpu.VMEM((tq, w), F32),
                        pltpu.VMEM((FOX_HEADS, tq, LANES), F32)],
        compiler_params=_params("parallel", "parallel", "arbitrary"),
        name="fox_prompt",
    )(qb, rows_fox, rows_fox, c_all, ck_t)


def _stack_heads(x, n):
    return jnp.concatenate([x[:, r * HEAD_DIM:(r + 1) * HEAD_DIM] for r in range(n)], axis=0)


def _top_n_mask(imp, blk, n_sel):
    work = imp
    selm = jnp.zeros(imp.shape, F32)
    for _ in range(n_sel):
        m = jnp.max(work, axis=-1, keepdims=True)
        idx = jnp.min(jnp.where(work == m, blk, 1.0e9), axis=-1, keepdims=True)
        pick = blk == idx
        selm = jnp.where(pick, 1.0, selm)
        work = jnp.where(pick, -jnp.inf, work)
    return selm


def _top_n_mask_t(imp, n_sel):
    work = imp.T
    blk = lax.broadcasted_iota(jnp.int32, work.shape, 0).astype(F32)
    selm = jnp.zeros(work.shape, F32)
    for _ in range(n_sel):
        m = jnp.max(work, axis=0, keepdims=True)
        idx = jnp.min(jnp.where(work == m, blk, 1.0e9), axis=0, keepdims=True)
        pick = blk == idx
        selm = jnp.where(pick, 1.0, selm)
        work = jnp.where(pick, -jnp.inf, work)
    return selm.T


def _importance(p_c, rows, qpos_col, blk):
    imp = p_c[0:rows]
    for r in range(1, NSA_GROUP):
        imp = imp + p_c[r * rows:(r + 1) * rows]
    cur = (qpos_col >> BLOCK_SHIFT).astype(F32)
    forced = (blk == 0.0) | (blk == cur) | (blk == cur - 1.0)
    imp = jnp.where(forced, NSA_FORCE_SCORE, imp)
    return jnp.where(blk <= cur, imp, NEG)


NSA_Q = 128
NSA_TK = 512
NSA_WKEYS = NSA_WINDOW + NSA_Q


def _nsa_prompt_kernel(qn_ref, qr_ref, kc_ref, vc_ref, ks_ref, vs_ref, kw_ref, vw_ref,
                       gate_ref, gbias_ref, cpos_ref, gkc_ref, e_ref, o_ref,
                       kcn_ref, vcm_ref):
    i = pl.program_id(2)
    t = kc_ref.shape[0]
    nb = t // NSA_BLOCK
    nbp = kcn_ref.shape[0]

    @pl.when(i == 0)
    def _():
        kcn_ref[...] = jnp.zeros_like(kcn_ref)
        vcm_ref[...] = jnp.zeros_like(vcm_ref)

        def body(jb, carry):
            rows = pl.ds(pl.multiple_of(jb * NSA_BLOCK, NSA_BLOCK), NSA_BLOCK)
            km = jnp.mean(kc_ref[rows, :] + cpos_ref[...], axis=0, keepdims=True)
            kcn_ref[pl.ds(jb, 1), :] = _head_rms(km, gkc_ref[...])
            vcm_ref[pl.ds(jb, 1), :] = jnp.mean(vc_ref[rows, :], axis=0, keepdims=True)
            return carry

        lax.fori_loop(0, nb, body, 0)

    rq = NSA_GROUP * NSA_Q
    qn = _stack_heads(qn_ref[...], NSA_GROUP)
    qr = _stack_heads(qr_ref[...], NSA_GROUP).astype(BF16)
    qpos1 = i * NSA_Q + lax.broadcasted_iota(jnp.int32, (NSA_Q, 1), 0)
    qpos = jnp.concatenate([qpos1] * NSA_GROUP, axis=0)

    blk_i = lax.broadcasted_iota(jnp.int32, (1, nbp), 1)
    s = _bdot_nt(qn, kcn_ref[...]) * SCALE
    p_c, _, _ = _softmax_rows(s, (blk_i + 1) * NSA_BLOCK - 1 <= qpos)
    o_c = _bdot(p_c, vcm_ref[...])

    blk = blk_i.astype(F32)
    imp = _importance(p_c, NSA_Q, qpos1, blk)
    selm = _top_n_mask_t(imp, min(NSA_TOP_N, nb)).astype(BF16)

    def sel_chunk(c, carry, causal):
        m_old, l_old, acc = carry
        rows = pl.ds(pl.multiple_of(c * NSA_TK, NSA_TK), NSA_TK)
        kmask = jnp.dot(selm, e_ref[c], preferred_element_type=F32)
        bias = (kmask - 1.0) * (-NEG)
        if causal:
            kpos = c * NSA_TK + lax.broadcasted_iota(jnp.int32, (1, NSA_TK), 1)
            bias = jnp.where(kpos <= qpos1, bias, NEG)
        sc = _bdot_nt(qr, ks_ref[rows, :]) * SCALE + jnp.concatenate([bias] * NSA_GROUP, axis=0)
        m_new = jnp.maximum(m_old, _lanes(jnp.max(sc, axis=-1, keepdims=True)))
        p = jnp.exp(sc - jnp.tile(m_new, (1, NSA_TK // LANES)))
        alpha = jnp.exp(m_old - m_new)
        l_new = alpha * l_old + _lanes(jnp.sum(p, axis=-1, keepdims=True))
        acc = alpha * acc + _bdot(p, vs_ref[rows, :])
        return m_new, l_new, acc

    last = (i * NSA_Q) // NSA_TK
    init = (jnp.full((rq, LANES), NEG, F32), jnp.zeros((rq, LANES), F32),
            jnp.zeros((rq, HEAD_DIM), F32))
    carry = lax.fori_loop(0, last, functools.partial(sel_chunk, causal=False), init)
    _, l_s, acc_s = sel_chunk(last, carry, True)
    o_s = acc_s / jnp.maximum(l_s, 1e-30)

    k0 = jnp.clip(i * NSA_Q - NSA_WINDOW, 0, t - NSA_WKEYS)
    k0 = pl.multiple_of(k0, NSA_Q)
    wrows = pl.ds(k0, NSA_WKEYS)
    kpos = k0 + lax.broadcasted_iota(jnp.int32, (1, NSA_WKEYS), 1)
    dist = qpos - kpos
    sw = jnp.where((dist >= 0) & (dist < NSA_WINDOW), _bdot_nt(qr, kw_ref[wrows, :]) * SCALE, NEG)
    m_w = _lanes(jnp.max(sw, axis=-1, keepdims=True))
    e_w = jnp.exp(sw - jnp.tile(m_w, (1, NSA_WKEYS // LANES)))
    o_w = _bdot(e_w, vw_ref[wrows, :]) / _lanes(jnp.sum(e_w, axis=-1, keepdims=True))

    gates = _sigmoid(gate_ref[...] + gbias_ref[...])
    for r in range(NSA_GROUP):
        rs = slice(r * NSA_Q, (r + 1) * NSA_Q)
        o = (gates[:, 3 * r:3 * r + 1] * o_c[rs] + gates[:, 3 * r + 1:3 * r + 2] * o_s[rs]
             + gates[:, 3 * r + 2:3 * r + 3] * o_w[rs])
        o_ref[:, r * HEAD_DIM:(r + 1) * HEAD_DIM] = o


def _expand_matrix(nbp, n_keys, tk):
    kb = (jnp.arange(n_keys, dtype=jnp.int32) // NSA_BLOCK).reshape(n_keys // tk, 1, tk)
    jb = jnp.arange(nbp, dtype=jnp.int32).reshape(1, nbp, 1)
    return (kb == jb).astype(BF16)


def _nsa_prompt(qn, qr, rows_nsa, rows_win, gates, gbias, cmp_pos, g_kc, batch):
    n = qn.shape[0]
    t = n // batch
    assert t % NSA_TK == 0 and t >= NSA_WKEYS
    nqb = t // NSA_Q
    nb = t // NSA_BLOCK
    nbp = -(-nb // LANES) * LANES
    e = _expand_matrix(nbp, t, NSA_TK)
    gw = NSA_GROUP * HEAD_DIM
    qspec = pl.BlockSpec((NSA_Q, gw), lambda b, g, i: (b * nqb + i, g))
    kv = lambda typ: pl.BlockSpec((t, HEAD_DIM), lambda b, g, i: (b, typ * NSA_KV_HEADS + g))
    return pl.pallas_call(
        _nsa_prompt_kernel,
        grid=(batch, NSA_KV_HEADS, nqb),
        in_specs=[qspec, qspec, kv(0), kv(1), kv(2), kv(3), kv(0), kv(1),
                  pl.BlockSpec((NSA_Q, LANES), lambda b, g, i: (b * nqb + i, g)),
                  pl.BlockSpec((1, LANES), lambda b, g, i: (0, g)),
                  pl.BlockSpec(cmp_pos.shape, lambda b, g, i: (0, 0)),
                  pl.BlockSpec((1, HEAD_DIM), lambda b, g, i: (0, 0)),
                  pl.BlockSpec(e.shape, lambda b, g, i: (0, 0, 0))],
        out_specs=qspec,
        out_shape=jax.ShapeDtypeStruct((n, NSA_HEADS * HEAD_DIM), F32),
        scratch_shapes=[pltpu.VMEM((nbp, HEAD_DIM), F32), pltpu.VMEM((nbp, HEAD_DIM), F32)],
        compiler_params=_params("parallel", "parallel", "arbitrary"),
        name="nsa_prompt",
    )(qn, qr, rows_nsa, rows_nsa, rows_nsa, rows_nsa, rows_win, rows_win,
      gates, gbias, cmp_pos, g_kc, e)


DIL_Q = 128
DIL_K = 2 * DIL_Q
DIL_UNROLL = 8


def _dil_prompt_kernel(q_ref, k_ref, v_ref, o_ref, og_ref, lse_ref):
    t = q_ref.shape[0]
    for gi, (window, dil) in enumerate(DIL_PAIRS):
        n_str = t // dil
        n_qb = n_str // DIL_Q
        span = window // dil

        def body(it, carry, gi=gi, dil=dil, n_qb=n_qb, span=span, n_str=n_str):
            r = it // n_qb
            ib = it % n_qb
            q0 = ib * DIL_Q
            k0 = jnp.clip(q0 - DIL_Q, 0, n_str - DIL_K)
            qrows = pl.ds(r + q0 * dil, DIL_Q, stride=dil) if dil > 1 else pl.ds(q0, DIL_Q)
            krows = pl.ds(r + k0 * dil, DIL_K, stride=dil) if dil > 1 else pl.ds(k0, DIL_K)
            s = _bdot_nt(q_ref[qrows, :], k_ref[krows, :]) * SCALE
            qi = q0 + lax.broadcasted_iota(jnp.int32, (DIL_Q, 1), 0)
            kj = k0 + lax.broadcasted_iota(jnp.int32, (1, DIL_K), 1)
            d = qi - kj
            s = jnp.where((d >= 0) & (d <= span), s, NEG)
            m = _lanes(jnp.max(s, axis=-1, keepdims=True))
            e = jnp.exp(s - jnp.tile(m, (1, DIL_K // LANES)))
            den = _lanes(jnp.sum(e, axis=-1, keepdims=True))
            og_ref[gi, qrows, :] = _bdot(e, v_ref[krows, :]) / den
            lse_ref[gi, qrows, :] = m + jnp.log(den)
            return carry

        lax.fori_loop(0, dil * n_qb, body, 0, unroll=DIL_UNROLL)

    lse = lse_ref[...]
    mx = jnp.max(lse, axis=0, keepdims=True)
    w = jnp.exp(lse - mx)
    w = w / jnp.sum(w, axis=0, keepdims=True)
    o_ref[...] = jnp.sum(w * og_ref[...], axis=0)


def _dil_prompt(q, rows, batch):
    n, w = q.shape
    t = n // batch
    assert t % (DIL_K * DIL_PAIRS[-1][1]) == 0
    blk = lambda off: pl.BlockSpec((t, HEAD_DIM), lambda b, h: (b, h + off))
    return pl.pallas_call(
        _dil_prompt_kernel,
        grid=(batch, DIL_HEADS),
        in_specs=[blk(0), blk(0), blk(DIL_HEADS)],
        out_specs=blk(0),
        out_shape=jax.ShapeDtypeStruct((n, w), F32),
        scratch_shapes=[pltpu.VMEM((len(DIL_PAIRS), t, HEAD_DIM), F32),
                        pltpu.VMEM((len(DIL_PAIRS), t, LANES), F32)],
        compiler_params=_params("parallel", "parallel"),
        name="dil_prompt",
    )(q, rows, rows)


NSA_ROWS = 4 * NSA_KV_HEADS
MEANS_PAGES = 16


def _page_rows(page_ref, typ, g, pos0, n_pos):
    return page_ref[0, 0, pl.ds(pos0 * NSA_ROWS + typ * NSA_KV_HEADS + g, n_pos, stride=NSA_ROWS), :]


def _page_means_kernel(pt_ref, *refs, n_pg):
    del pt_ref
    page_refs, cpos_ref, o_ref = refs[:n_pg], refs[n_pg], refs[n_pg + 1]
    n_blk = page_refs[0].shape[2] // (NSA_ROWS * NSA_BLOCK)
    for k in range(n_pg):
        for j in range(n_blk):
            parts = []
            for typ in range(2):
                for g in range(NSA_KV_HEADS):
                    x = _page_rows(page_refs[k], typ, g, j * NSA_BLOCK, NSA_BLOCK)
                    if typ == 0:
                        x = x + cpos_ref[...]
                    parts.append(jnp.mean(x, axis=0, keepdims=True))
            o_ref[0, k * n_blk + j:k * n_blk + j + 1, :] = jnp.concatenate(parts, axis=1)


def _page_means(cache, layer, page_table, cmp_pos):
    b, n_pages = page_table.shape
    rows = cache.shape[2]
    n_blk = rows // (NSA_ROWS * NSA_BLOCK)
    n_pg = math.gcd(MEANS_PAGES, n_pages)
    w = 2 * NSA_KV_HEADS * HEAD_DIM
    page_spec = lambda k: pl.BlockSpec(
        (1, 1, rows, LANES), lambda bi, p, pt: (layer, pt[bi, p * n_pg + k], 0, 0))
    grid_spec = pltpu.PrefetchScalarGridSpec(
        num_scalar_prefetch=1,
        grid=(b, n_pages // n_pg),
        in_specs=[page_spec(k) for k in range(n_pg)]
        + [pl.BlockSpec(cmp_pos.shape, lambda bi, p, pt: (0, 0))],
        out_specs=pl.BlockSpec((1, n_pg * n_blk, w), lambda bi, p, pt: (bi, p, 0)),
    )
    return pl.pallas_call(
        functools.partial(_page_means_kernel, n_pg=n_pg),
        grid_spec=grid_spec,
        out_shape=jax.ShapeDtypeStruct((b, n_pages * n_blk, w), F32),
        compiler_params=_params("parallel", "arbitrary"),
        name="page_means",
    )(page_table, *([cache] * n_pg), cmp_pos)


def _nsa_cmp_sample_kernel(qn_ref, means_ref, gkc_ref, oc_ref, sel_ref, *, pos0):
    tq = qn_ref.shape[0]
    nbp = means_ref.shape[1]
    qpos1 = pos0 + lax.broadcasted_iota(jnp.int32, (tq, 1), 0)
    qpos = jnp.concatenate([qpos1] * NSA_GROUP, axis=0)
    blk_i = lax.broadcasted_iota(jnp.int32, (1, nbp), 1)
    blk = blk_i.astype(F32)
    gw = NSA_GROUP * HEAD_DIM
    half = NSA_KV_HEADS * HEAD_DIM
    for g in range(NSA_KV_HEADS):
        qn = _stack_heads(qn_ref[:, g * gw:(g + 1) * gw], NSA_GROUP)
        kc = _head_rms(means_ref[0, :, g * HEAD_DIM:(g + 1) * HEAD_DIM], gkc_ref[...])
        vc = means_ref[0, :, half + g * HEAD_DIM:half + (g + 1) * HEAD_DIM]
        s = _bdot_nt(qn, kc) * SCALE
        p_c, _, _ = _softmax_rows(s, (blk_i + 1) * NSA_BLOCK - 1 <= qpos)
        o_c = _bdot(p_c, vc)
        for r in range(NSA_GROUP):
            oc_ref[:, g * gw + r * HEAD_DIM:g * gw + (r + 1) * HEAD_DIM] = o_c[r * tq:(r + 1) * tq]
        imp = _importance(p_c, tq, qpos1, blk)
        sel_ref[0, g * tq:(g + 1) * tq, :] = _top_n_mask(imp, blk, NSA_TOP_N)


def _nsa_cmp_sample(qn, means, g_kc, batch, pos0):
    n, w = qn.shape
    tq = n // batch
    nbp = means.shape[1]
    return pl.pallas_call(
        functools.partial(_nsa_cmp_sample_kernel, pos0=pos0),
        grid=(batch,),
        in_specs=[pl.BlockSpec((tq, w), lambda b: (b, 0)),
                  pl.BlockSpec((1, nbp, means.shape[2]), lambda b: (b, 0, 0)),
                  pl.BlockSpec((1, HEAD_DIM), lambda b: (0, 0))],
        out_specs=[pl.BlockSpec((tq, w), lambda b: (b, 0)),
                   pl.BlockSpec((1, NSA_KV_HEADS * tq, nbp), lambda b: (b, 0, 0))],
        out_shape=[jax.ShapeDtypeStruct((n, w), F32),
                   jax.ShapeDtypeStruct((batch, NSA_KV_HEADS * tq, nbp), F32)],
        compiler_params=_params("parallel"),
        name="nsa_cmp_sample",
    )(qn, means, g_kc)


def _block_diag_queries(q, n_kv, reps):
    parts = []
    for g in range(n_kv):
        for r in range(reps):
            hq = q[:, (g * reps + r) * HEAD_DIM:(g * reps + r + 1) * HEAD_DIM]
            parts.append(jnp.concatenate(
                [hq if gg == g else jnp.zeros_like(hq) for gg in range(n_kv)], axis=1))
    return jnp.concatenate(parts, axis=0).astype(BF16)


def _rep_rows(x, n_kv, per):
    return jnp.concatenate(
        [jnp.broadcast_to(x[g:g + 1, :], (per, x.shape[1])) for g in range(n_kv)], axis=0)


def _online_update(s, mask, v, m_ref, l_ref, acc_ref):
    if mask is not None:
        s = jnp.where(mask, s, NEG)
    m_old = m_ref[...]
    m_new = jnp.maximum(m_old, jnp.max(s, axis=-1, keepdims=True))
    pr = jnp.exp(s - m_new)
    if mask is not None:
        pr = jnp.where(mask, pr, 0.0)
    alpha = jnp.exp(m_old - m_new)
    l_ref[...] = alpha * l_ref[...] + jnp.sum(pr, axis=-1, keepdims=True)
    acc_ref[...] = alpha * acc_ref[...] + _bdot(pr, v)
    m_ref[...] = m_new


def _pad_rows(x, rows):
    return jnp.concatenate([x, jnp.zeros((rows - x.shape[0], x.shape[1]), x.dtype)], axis=0)


def _write_diag(o_ref, l_ref, acc_ref, n_kv, reps, tq):
    for g in range(n_kv):
        for r in range(reps):
            rs = slice((g * reps + r) * tq, (g * reps + r + 1) * tq)
            o = acc_ref[rs, g * HEAD_DIM:(g + 1) * HEAD_DIM] / jnp.maximum(l_ref[rs, :], 1e-30)
            o_ref[:, (g * reps + r) * HEAD_DIM:(g * reps + r + 1) * HEAD_DIM] = o


FOX_PAGES = 8
HEAD_SHIFT = 3


def _fox_sample_kernel(pt_ref, q_ref, *refs, n_pg):
    del pt_ref
    page_refs, lf_refs = refs[:n_pg], refs[n_pg:2 * n_pg]
    new_ref, lfn_ref, o_ref, cq_ref, carry_ref, m_ref, l_ref, acc_ref = refs[2 * n_pg:]
    p = pl.program_id(1)
    tq = q_ref.shape[0]
    page = page_refs[0].shape[2]
    rows = FOX_HEADS * tq
    row_h = jnp.concatenate([jnp.full((tq, 1), h, jnp.int32) for h in range(FOX_HEADS)], axis=0)
    q_all = _stack_heads(q_ref[...], FOX_HEADS).astype(BF16)
    pr_i = lax.broadcasted_iota(jnp.int32, (LANES, LANES), 0)
    pc_i = lax.broadcasted_iota(jnp.int32, (LANES, LANES), 1)

    @pl.when(p == 0)
    def _():
        carry_ref[...] = jnp.zeros_like(carry_ref)
        lane = lax.broadcasted_iota(jnp.int32, (1, LANES), 1)
        cnew = _exact_dot(lfn_ref[0], (pr_i <= pc_i).astype(BF16))
        cnew_rows = _rep_rows(cnew, FOX_HEADS, tq)
        row_t = _token_of_row(FOX_HEADS, tq)
        cq = jnp.sum(jnp.where(lane == row_t, cnew_rows, 0.0), axis=-1, keepdims=True)
        cq_ref[...] = cq
        kw = FOX_HEADS * HEAD_DIM
        qbd = _block_diag_queries(q_ref[...], FOX_HEADS, 1)
        kn = _pad_rows(new_ref[:, 0:kw], LANES)
        vn = _pad_rows(new_ref[:, kw:2 * kw], LANES)
        mask = lane <= row_t
        s = jnp.where(mask, _bdot_nt(qbd, kn) * SCALE + (cq - cnew_rows), NEG)
        m = jnp.max(s, axis=-1, keepdims=True)
        pr = jnp.where(mask, jnp.exp(s - m), 0.0)
        o_full = _bdot(pr, vn)
        m_ref[...] = m
        l_ref[...] = jnp.sum(pr, axis=-1, keepdims=True)
        acc_ref[...] = jnp.concatenate(
            [o_full[h * tq:(h + 1) * tq, h * HEAD_DIM:(h + 1) * HEAD_DIM] for h in range(FOX_HEADS)],
            axis=0)

    same_head = (pr_i & (FOX_HEADS - 1)) == (pc_i & (FOX_HEADS - 1))
    later_in_row = (same_head & ((pr_i >> HEAD_SHIFT) > (pc_i >> HEAD_SHIFT))).astype(BF16)
    row_total = same_head.astype(BF16)
    n_r = page * FOX_HEADS // LANES
    lane_k = lax.broadcasted_iota(jnp.int32, (1, n_pg * page * FOX_HEADS), 1)
    head_ok = (lane_k & (FOX_HEADS - 1)) == row_h
    cq = cq_ref[...]
    carry = carry_ref[...]
    s_parts, v_parts = [], []
    for k in range(n_pg):
        lf = lf_refs[k][0, 0]
        within = _exact_dot(lf, later_in_row)
        tot = _exact_dot(lf, row_total)
        run = carry
        later = [None] * n_r
        for r in reversed(range(n_r)):
            later[r] = run
            run = run + tot[r:r + 1, :]
        carry = run
        suffix = within + jnp.concatenate(later, axis=0)
        xk = page_refs[k][0, 0, :, 0].reshape(page * FOX_HEADS, HEAD_DIM)
        s_all = _bdot_nt(q_all, xk) * SCALE
        s_parts += [s_all[:, r * LANES:(r + 1) * LANES] + (cq + suffix[r:r + 1, :]) for r in range(n_r)]
        v_parts.append(page_refs[k][0, 0, :, 1].reshape(page * FOX_HEADS, HEAD_DIM))
    carry_ref[...] = carry
    s = jnp.concatenate(s_parts, axis=1)
    _online_update(s, head_ok, jnp.concatenate(v_parts, axis=0), m_ref, l_ref, acc_ref)

    @pl.when(p == pl.num_programs(1) - 1)
    def _():
        o = acc_ref[...] / jnp.maximum(l_ref[...], 1e-30)
        for h in range(FOX_HEADS):
            o_ref[:, h * HEAD_DIM:(h + 1) * HEAD_DIM] = o[h * tq:(h + 1) * tq]


def _fox_sample(qb, cache_kv, cache_lf, layer, page_table, rows_new, lf_new_t):
    n, w = qb.shape
    b, n_pages = page_table.shape
    tq = n // b
    page = cache_kv.shape[2]
    rows = FOX_HEADS * tq
    n_pg = math.gcd(FOX_PAGES, n_pages)
    assert tq <= LANES and (page * FOX_HEADS) % LANES == 0
    pg6 = lambda k: (lambda bi, p, pt: (layer, pt[bi, n_pages - 1 - (p * n_pg + k)], 0, 0, 0, 0))
    pg4 = lambda k: (lambda bi, p, pt: (layer, pt[bi, n_pages - 1 - (p * n_pg + k)], 0, 0))
    grid_spec = pltpu.PrefetchScalarGridSpec(
        num_scalar_prefetch=1,
        grid=(b, n_pages // n_pg),
        in_specs=[pl.BlockSpec((tq, w), lambda bi, p, pt: (bi, 0))]
        + [pl.BlockSpec((1, 1) + cache_kv.shape[2:], pg6(k)) for k in range(n_pg)]
        + [pl.BlockSpec((1, 1) + cache_lf.shape[2:], pg4(k)) for k in range(n_pg)]
        + [pl.BlockSpec((tq, 2 * w), lambda bi, p, pt: (bi, 0)),
           pl.BlockSpec((1, FOX_HEADS, LANES), lambda bi, p, pt: (bi, 0, 0))],
        out_specs=pl.BlockSpec((tq, w), lambda bi, p, pt: (bi, 0)),
        scratch_shapes=[pltpu.VMEM((rows, 1), F32), pltpu.VMEM((1, LANES), F32),
                        pltpu.VMEM((rows, 1), F32), pltpu.VMEM((rows, 1), F32),
                        pltpu.VMEM((rows, HEAD_DIM), F32)],
    )
    return pl.pallas_call(
        functools.partial(_fox_sample_kernel, n_pg=n_pg),
        grid_spec=grid_spec,
        out_shape=jax.ShapeDtypeStruct((n, w), F32),
        compiler_params=_params("parallel", "arbitrary"),
        name="fox_sample",
    )(page_table, qb, *([cache_kv] * n_pg), *([cache_lf] * n_pg), rows_new, lf_new_t)


SEL_PAGES = 8


def _nsa_sel_sample_kernel(pt_ref, q_ref, *refs, n_pg, pos0):
    del pt_ref
    page_refs = refs[:n_pg]
    new_ref, sel_ref, o_ref, qbd_ref, m_ref, l_ref, acc_ref = refs[n_pg:]
    p = pl.program_id(1)
    tq = q_ref.shape[0]
    page = page_refs[0].shape[2] // NSA_ROWS
    kw = NSA_KV_HEADS * HEAD_DIM
    nbp = sel_ref.shape[2]
    selm = sel_ref[0].astype(BF16)

    def key_mask(first_block, n_keys):
        blk_r = lax.broadcasted_iota(jnp.int32, (nbp, n_keys), 0)
        blk_of_lane = lax.broadcasted_iota(jnp.int32, (nbp, n_keys), 1) >> BLOCK_SHIFT
        expand = (blk_r == first_block + blk_of_lane).astype(BF16)
        km = jnp.dot(selm, expand, preferred_element_type=F32)
        return jnp.concatenate(
            [km[g * tq:(g + 1) * tq] for g in range(NSA_KV_HEADS) for _ in range(NSA_GROUP)],
            axis=0) > 0.5

    @pl.when(p == 0)
    def _():
        m_ref[...] = jnp.full_like(m_ref, NEG)
        l_ref[...] = jnp.zeros_like(l_ref)
        acc_ref[...] = jnp.zeros_like(acc_ref)
        qbd = _block_diag_queries(q_ref[...], NSA_KV_HEADS, NSA_GROUP)
        qbd_ref[...] = qbd
        row_t = _token_of_row(NSA_HEADS, tq)
        lane = lax.broadcasted_iota(jnp.int32, (1, LANES), 1)
        kn = _pad_rows(new_ref[:, 0:kw], LANES)
        vn = _pad_rows(new_ref[:, kw:2 * kw], LANES)
        s = _bdot_nt(qbd, kn) * SCALE
        _online_update(s, key_mask(pos0 // NSA_BLOCK, LANES) & (lane <= row_t), vn,
                       m_ref, l_ref, acc_ref)

    def slab(typ):
        return jnp.concatenate(
            [jnp.concatenate([_page_rows(page_refs[k], typ, g, 0, page) for g in range(NSA_KV_HEADS)],
                             axis=1) for k in range(n_pg)], axis=0)

    s = _bdot_nt(qbd_ref[...], slab(2)) * SCALE
    _online_update(s, key_mask(p * (n_pg * page // NSA_BLOCK), n_pg * page), slab(3),
                   m_ref, l_ref, acc_ref)

    @pl.when(p == pl.num_programs(1) - 1)
    def _():
        _write_diag(o_ref, l_ref, acc_ref, NSA_KV_HEADS, NSA_GROUP, tq)


def _nsa_sel_sample(qr, cache, layer, page_table, rows_new, sel, pos0):
    n, w = qr.shape
    b, n_pages = page_table.shape
    tq = n // b
    rows_pp = cache.shape[2]
    page = rows_pp // NSA_ROWS
    kvw = 2 * NSA_KV_HEADS * HEAD_DIM
    rows = NSA_HEADS * tq
    n_pg = math.gcd(SEL_PAGES, n_pages)
    assert pos0 == n_pages * page and pos0 % NSA_BLOCK == 0 and tq <= NSA_BLOCK
    page_spec = lambda k: pl.BlockSpec(
        (1, 1, rows_pp, LANES), lambda bi, p, pt: (layer, pt[bi, p * n_pg + k], 0, 0))
    grid_spec = pltpu.PrefetchScalarGridSpec(
        num_scalar_prefetch=1,
        grid=(b, n_pages // n_pg),
        in_specs=[pl.BlockSpec((tq, w), lambda bi, p, pt: (bi, 0))]
        + [page_spec(k) for k in range(n_pg)]
        + [pl.BlockSpec((tq, kvw), lambda bi, p, pt: (bi, 1)),
           pl.BlockSpec((1,) + sel.shape[1:], lambda bi, p, pt: (bi, 0, 0))],
        out_specs=pl.BlockSpec((tq, w), lambda bi, p, pt: (bi, 0)),
        scratch_shapes=[pltpu.VMEM((rows, kvw // 2), BF16), pltpu.VMEM((rows, 1), F32),
                        pltpu.VMEM((rows, 1), F32), pltpu.VMEM((rows, kvw // 2), F32)],
    )
    return pl.pallas_call(
        functools.partial(_nsa_sel_sample_kernel, n_pg=n_pg, pos0=pos0),
        grid_spec=grid_spec,
        out_shape=jax.ShapeDtypeStruct((n, w), F32),
        compiler_params=_params("parallel", "arbitrary"),
        name="nsa_sel_sample",
    )(page_table, qr, *([cache] * n_pg), rows_new, sel)


def _nsa_win_sample_kernel(qr_ref, win_ref, new_ref, oc_ref, os_ref, gate_ref, gbias_ref,
                           o_ref, nwin_ref, kv_ref):
    tq = qr_ref.shape[0]
    lw = win_ref.shape[1]
    nk = kv_ref.shape[0]
    kw = NSA_KV_HEADS * HEAD_DIM
    gw = NSA_GROUP * HEAD_DIM
    kv_ref[0:lw, :] = win_ref[0]
    kv_ref[lw:nk, :] = _pad_rows(new_ref[...], nk - lw)
    nwin_ref[0, 0:lw - tq, :] = win_ref[0, tq:lw, :]
    nwin_ref[0, lw - tq:lw, :] = new_ref[...]

    idx = lax.broadcasted_iota(jnp.int32, (1, nk), 1)
    row_t = _token_of_row(NSA_GROUP, tq)
    dist = lw + row_t - idx
    mask = (dist >= 0) & (dist < NSA_WINDOW) & (idx < lw + tq)
    gates = _sigmoid(gate_ref[...] + gbias_ref[...])
    for g in range(NSA_KV_HEADS):
        qr = _stack_heads(qr_ref[:, g * gw:(g + 1) * gw], NSA_GROUP)
        s = _bdot_nt(qr, kv_ref[:, g * HEAD_DIM:(g + 1) * HEAD_DIM]) * SCALE
        p_w, _, _ = _softmax_rows(s, mask)
        o_w = _bdot(p_w, kv_ref[:, kw + g * HEAD_DIM:kw + (g + 1) * HEAD_DIM])
        for r in range(NSA_GROUP):
            cs = slice(g * gw + r * HEAD_DIM, g * gw + (r + 1) * HEAD_DIM)
            gc = g * LANES + 3 * r
            o_ref[:, cs] = (gates[:, gc:gc + 1] * oc_ref[:, cs] + gates[:, gc + 1:gc + 2] * os_ref[:, cs]
                            + gates[:, gc + 2:gc + 3] * o_w[r * tq:(r + 1) * tq])


def _nsa_win_sample(qr, win, rows_win, o_c, o_s, gates, gbias, batch):
    n, w = qr.shape
    tq = n // batch
    lw, ww = win.shape[1], win.shape[2]
    nk = lw + LANES
    assert lw >= NSA_WINDOW and tq % 8 == 0 and tq <= LANES
    row = lambda a: pl.BlockSpec((tq, a.shape[1]), lambda b: (b, 0))
    return pl.pallas_call(
        _nsa_win_sample_kernel,
        grid=(batch,),
        in_specs=[row(qr), pl.BlockSpec((1, lw, ww), lambda b: (b, 0, 0)), row(rows_win),
                  row(o_c), row(o_s), row(gates), pl.BlockSpec(gbias.shape, lambda b: (0, 0))],
        out_specs=[row(qr), pl.BlockSpec((1, lw, ww), lambda b: (b, 0, 0))],
        out_shape=[jax.ShapeDtypeStruct((n, w), F32), jax.ShapeDtypeStruct(win.shape, F32)],
        scratch_shapes=[pltpu.VMEM((nk, ww), F32)],
        compiler_params=_params("parallel"),
        name="nsa_win_sample",
    )(qr, win, rows_win, o_c, o_s, gates, gbias)


DIL_CHUNK = 256
DIL_HEAD_SHIFT = 4


def _dil_sample_kernel(q_ref, st_ref, new_ref, o_ref, m_ref, l_ref, acc_ref, *, lw):
    c = pl.program_id(1)
    n_chunks = pl.num_programs(1) - 1
    tq = q_ref.shape[0]
    rows = DIL_HEADS * tq
    n_g = len(DIL_PAIRS)
    q_all = _stack_heads(q_ref[...], DIL_HEADS).astype(BF16)
    row_t = _token_of_row(DIL_HEADS, tq)
    row_h = jnp.concatenate([jnp.full((tq, 1), h, jnp.int32) for h in range(DIL_HEADS)], axis=0)

    @pl.when(c == 0)
    def _():
        m_ref[...] = jnp.full_like(m_ref, NEG)
        l_ref[...] = jnp.zeros_like(l_ref)
        acc_ref[...] = jnp.zeros_like(acc_ref)

    def process(k_rows, v_rows, j0, first_chunk):
        n_l = k_rows.shape[0] * DIL_HEADS
        xk = k_rows[...].reshape(n_l, HEAD_DIM)
        s = _bdot_nt(q_all, xk) * SCALE
        lane = lax.broadcasted_iota(jnp.int32, (1, n_l), 1)
        head_ok = (lane & (DIL_HEADS - 1)) == row_h
        off = lw + row_t - (j0 + (lane >> DIL_HEAD_SHIFT))

        def group(gi, window, dil):
            mask = head_ok & (off >= 0) & (off <= window) & ((off & (dil - 1)) == 0)
            sm = jnp.where(mask, s, NEG)
            m_old = m_ref[gi]
            m_new = jnp.maximum(m_old, jnp.max(sm, axis=-1, keepdims=True))
            pr = jnp.where(mask, jnp.exp(sm - m_new), 0.0)
            alpha = jnp.exp(m_old - m_new)
            l_ref[gi] = alpha * l_ref[gi] + jnp.sum(pr, axis=-1, keepdims=True)
            acc_ref[gi] = alpha * acc_ref[gi] + _bdot(pr, v_rows[...].reshape(n_l, HEAD_DIM))
            m_ref[gi] = m_new

        for gi, (window, dil) in enumerate(DIL_PAIRS):
            pl.when(c >= first_chunk(window))(functools.partial(group, gi, window, dil))

    @pl.when(c < n_chunks)
    def _():
        n_pos = st_ref.shape[2]
        process(st_ref.at[0, 0, :, 0], st_ref.at[0, 0, :, 1], c * n_pos,
                lambda window: max(lw - window, 0) // n_pos)

    @pl.when(c == n_chunks)
    def _():
        process(new_ref.at[0, :, 0], new_ref.at[0, :, 1], lw, lambda window: 0)
        lses = [m_ref[gi] + jnp.log(l_ref[gi]) for gi in range(n_g)]
        mx = jnp.maximum(jnp.maximum(lses[0], lses[1]), lses[2])
        ws = [jnp.exp(l - mx) for l in lses]
        tot = ws[0] + ws[1] + ws[2]
        o = sum((ws[gi] / tot) * (acc_ref[gi] / l_ref[gi]) for gi in range(n_g))
        for h in range(DIL_HEADS):
            o_ref[:, h * HEAD_DIM:(h + 1) * HEAD_DIM] = o[h * tq:(h + 1) * tq]


def _dil_sample(q, state, layer, rows_new, batch):
    n, w = q.shape
    tq = n // batch
    lw = state.shape[2]
    assert lw >= DIL_PAIRS[-1][0] and lw % DIL_CHUNK == 0
    n_chunks = lw // DIL_CHUNK
    rows = DIL_HEADS * tq
    n_g = len(DIL_PAIRS)
    return pl.pallas_call(
        functools.partial(_dil_sample_kernel, lw=lw),
        grid=(batch, n_chunks + 1),
        in_specs=[pl.BlockSpec((tq, w), lambda b, c: (b, 0)),
                  pl.BlockSpec((1, 1, DIL_CHUNK) + state.shape[3:],
                               lambda b, c: (layer, b, jnp.minimum(c, n_chunks - 1), 0, 0, 0)),
                  pl.BlockSpec((1,) + rows_new.shape[1:], lambda b, c: (b, 0, 0, 0, 0))],
        out_specs=pl.BlockSpec((tq, w), lambda b, c: (b, 0)),
        out_shape=jax.ShapeDtypeStruct((n, w), F32),
        scratch_shapes=[pltpu.VMEM((n_g, rows, 1), F32), pltpu.VMEM((n_g, rows, 1), F32),
                        pltpu.VMEM((n_g, rows, HEAD_DIM), F32)],
        compiler_params=_params("parallel", "arbitrary"),
        name="dil_sample",
    )(q, state, rows_new)


def _state_shift_kernel(st_ref, new_ref, out_ref):
    lw, tq = st_ref.shape[0], new_ref.shape[0]
    out_ref[0:lw - tq] = st_ref[tq:lw]
    out_ref[lw - tq:lw] = new_ref[...]


def _state_shift(state, rows_new):
    n_l, b, lw, n_s = state.shape[:4]
    tq = rows_new.shape[2]
    tile = state.shape[4:]
    spec = lambda rows: pl.BlockSpec((None, None, rows, None) + tile,
                                     lambda li, bi, s: (li, bi, 0, s, 0, 0))
    return pl.pallas_call(
        _state_shift_kernel,
        grid=(n_l, b, n_s),
        in_specs=[spec(lw), spec(tq)],
        out_specs=spec(lw),
        out_shape=jax.ShapeDtypeStruct(state.shape, F32),
        compiler_params=_params("parallel", "parallel", "arbitrary"),
        name="state_shift",
    )(state, rows_new)


def _rope_tables(pos):
    half = HEAD_DIM // 2
    inv = ROPE_THETA ** (-jnp.arange(half, dtype=F32) / half)
    ang = pos.astype(F32)[:, None] * inv[None, :]
    cos, sin = jnp.cos(ang), jnp.sin(ang)
    return jnp.concatenate([cos, cos], axis=-1), jnp.concatenate([-sin, sin], axis=-1)


def _pad_lanes(a, width=LANES):
    return jnp.pad(a, ((0, 0),) * (a.ndim - 1) + ((0, width - a.shape[-1]),))


def _chunks(first, count, op="copy", gi=0):
    return [(first + c, op, gi) for c in range(count)]


NSA_W = NSA_HEADS * HEAD_DIM
NSA_KV_W = NSA_KV_HEADS * HEAD_DIM
FOX_W = FOX_HEADS * HEAD_DIM
DIL_W = DIL_HEADS * HEAD_DIM
GATES_PER_GROUP = 3 * NSA_GROUP


def _even_weights(w_in, w_out, gate_b, forget_b, nsa_g, fox_g):
    cuts = [0]
    for wd in ((NSA_W,) + (NSA_KV_W,) * 6
               + (3 * NSA_HEADS, NSA_W, FOX_W, FOX_W, FOX_W, FOX_HEADS, FOX_W)):
        cuts.append(cuts[-1] + wd)
    seg = lambda a, b: w_in[:, cuts[a]:cuts[b]]
    gate_w, f_w = seg(7, 8), seg(12, 13)
    small = [_pad_lanes(f_w)] + [_pad_lanes(gate_w[:, g * GATES_PER_GROUP:(g + 1) * GATES_PER_GROUP])
                                 for g in range(NSA_KV_HEADS)]
    gbias = jnp.concatenate(
        [_pad_lanes(gate_b[None, g * GATES_PER_GROUP:(g + 1) * GATES_PER_GROUP])
         for g in range(NSA_KV_HEADS)], axis=1)
    aux = jnp.stack([nsa_g[0], nsa_g[2], nsa_g[3], fox_g[0], fox_g[1], _pad_lanes(forget_b),
                     jnp.zeros((LANES,), F32), jnp.zeros((LANES,), F32)])
    return dict(
        w_a=seg(0, 7).astype(BF16),
        w_b=seg(8, 10).astype(BF16),
        w_c=seg(10, 12).astype(BF16),
        w_d=jnp.concatenate([seg(13, 14)] + small, axis=1).astype(BF16),
        w_out_a=w_out[:NSA_W].astype(BF16), w_out_b=w_out[NSA_W:].astype(BF16),
        aux=aux, gbias=gbias, g_kc=nsa_g[1][None, :])


def _even_projections(x, g_norm, wts, pos):
    h = _rmsnorm(x, g_norm)
    cos2, sin2 = _rope_tables(pos)
    nq, nkv = NSA_HEADS, NSA_KV_HEADS
    qn, qr, rows_nsa, rows_win = _proj(
        h, wts["w_a"], wts["aux"], cos2, sin2,
        [[_chunks(0, nq, "rms", 0)], [_chunks(0, nq, "rms_rope", 0)],
         [_chunks(nq, 2 * nkv) + _chunks(nq + 2 * nkv, nkv, "rms_rope", 1) + _chunks(nq + 3 * nkv, nkv)],
         [_chunks(nq + 4 * nkv, nkv, "rms_rope", 2) + _chunks(nq + 5 * nkv, nkv)]],
        wts["w_a"].shape[1])
    z_a, qb = _proj(h, wts["w_b"], wts["aux"], cos2, sin2,
                    [[_chunks(0, nq)], [_chunks(nq, FOX_HEADS, "rms", 3)]], wts["w_b"].shape[1])
    rows_fox, = _proj(h, wts["w_c"], wts["aux"], cos2, sin2,
                      [[_chunks(0, FOX_HEADS, "rms", 4) + _chunks(FOX_HEADS, FOX_HEADS)]],
                      wts["w_c"].shape[1])
    z_b, logf, gates = _proj(
        h, wts["w_d"], wts["aux"], cos2, sin2,
        [[_chunks(0, FOX_HEADS)], [_chunks(FOX_HEADS, 1, "logsig", 5)], [_chunks(FOX_HEADS + 1, nkv)]],
        wts["w_d"].shape[1])
    return qn, qr, rows_nsa, rows_win, z_a, qb, rows_fox, z_b, logf, gates


def _even_prompt(x, batch, g_norm, wts, cmp_pos):
    n = x.shape[0]
    t = n // batch
    pos = jnp.tile(jnp.arange(t, dtype=jnp.int32), batch)
    qn, qr, rows_nsa, rows_win, z_a, qb, rows_fox, z_b, logf, gates = _even_projections(
        x, g_norm, wts, pos)
    c_all = _cumsum_rows(logf, batch)
    ck_t = jnp.transpose(c_all[:, :FOX_HEADS].reshape(batch, t, FOX_HEADS), (0, 2, 1))
    o_b = _fox_prompt(qb, rows_fox, c_all, ck_t, batch)
    o_a = _nsa_prompt(qn, qr, rows_nsa, rows_win, gates, wts["gbias"], cmp_pos, wts["g_kc"], batch)
    y = _outproj(x, [o_a, o_b], [z_a, z_b], [wts["w_out_a"], wts["w_out_b"]])
    lw = min(NSA_WINDOW, t)
    new_win = rows_win.reshape(batch, t, 2, NSA_KV_HEADS, HEAD_DIM)[:, t - lw:]
    return y, (rows_nsa.reshape(batch, t, 4, NSA_KV_HEADS, HEAD_DIM),
               rows_fox.reshape(batch, t, 2, FOX_HEADS, HEAD_DIM),
               logf[:, :FOX_HEADS].reshape(batch, t, FOX_HEADS), new_win)


def _even_sample(x, batch, pos0, g_norm, wts, cmp_pos, layer, cache_nsa, cache_fox, cache_lf,
                 win_state, page_table):
    n = x.shape[0]
    tq = n // batch
    pos = jnp.tile(pos0 + jnp.arange(tq, dtype=jnp.int32), batch)
    qn, qr, rows_nsa, rows_win, z_a, qb, rows_fox, z_b, logf, gates = _even_projections(
        x, g_norm, wts, pos)
    means = _page_means(cache_nsa, layer, page_table, cmp_pos)
    n_blocks = means.shape[1]
    nbp = -(-(n_blocks + 1) // LANES) * LANES
    means = jnp.pad(means, ((0, 0), (0, nbp - n_blocks), (0, 0)))
    o_c, sel = _nsa_cmp_sample(qn, means, wts["g_kc"], batch, pos0)
    o_s = _nsa_sel_sample(qr, cache_nsa, layer, page_table, rows_nsa, sel, pos0)
    lw = win_state.shape[1]
    win = win_state.reshape(batch, lw, 2 * NSA_KV_W)
    o_a, new_win = _nsa_win_sample(qr, win, rows_win, o_c, o_s, gates, wts["gbias"], batch)
    lf_new_t = _pad_lanes(
        jnp.transpose(logf[:, :FOX_HEADS].reshape(batch, tq, FOX_HEADS), (0, 2, 1)))
    o_b = _fox_sample(qb, cache_fox, cache_lf, layer, page_table, rows_fox, lf_new_t)
    y = _outproj(x, [o_a, o_b], [z_a, z_b], [wts["w_out_a"], wts["w_out_b"]])
    return y, (rows_nsa.reshape(batch, tq, 4, NSA_KV_HEADS, HEAD_DIM),
               rows_fox.reshape(batch, tq, 2, FOX_HEADS, HEAD_DIM),
               logf[:, :FOX_HEADS].reshape(batch, tq, FOX_HEADS),
               new_win.reshape(batch, lw, 2, NSA_KV_HEADS, HEAD_DIM))


def _odd_weights(w_in, w_out, qk_g):
    aux = jnp.concatenate([qk_g, jnp.zeros((6, HEAD_DIM), F32)], axis=0)
    return dict(w_q=w_in[:, :DIL_W].astype(BF16), w_kv=w_in[:, DIL_W:3 * DIL_W].astype(BF16),
                w_z=w_in[:, 3 * DIL_W:].astype(BF16), w_out=w_out.astype(BF16), aux=aux)


def _odd_projections(x, g_norm, wts, pos, head_major_rows=False):
    h = _rmsnorm(x, g_norm)
    cos2, sin2 = _rope_tables(pos)
    q, = _proj(h, wts["w_q"], wts["aux"], cos2, sin2, [[_chunks(0, DIL_HEADS, "rms_rope", 0)]], DIL_W)
    rows, = _proj(h, wts["w_kv"], wts["aux"], cos2, sin2,
                  [[_chunks(0, DIL_HEADS, "rms_rope", 1), _chunks(0, DIL_HEADS)]], DIL_W,
                  head_major=[head_major_rows])
    z, = _proj(h, wts["w_z"], wts["aux"], cos2, sin2, [[_chunks(0, DIL_HEADS)]], DIL_W)
    return q, rows, z


def _odd_prompt(x, batch, g_norm, wts):
    n = x.shape[0]
    t = n // batch
    pos = jnp.tile(jnp.arange(t, dtype=jnp.int32), batch)
    q, rows, z = _odd_projections(x, g_norm, wts, pos)
    o = _dil_prompt(q, rows, batch)
    y = _outproj(x, [o], [z], [wts["w_out"]])
    lw = min(DIL_PAIRS[-1][0], t)
    return y, rows.reshape(batch, t, 2, DIL_HEADS, HEAD_DIM)[:, t - lw:]


STATE_TILE = 8


def _odd_sample(x, batch, pos0, g_norm, wts, state_all, layer):
    n = x.shape[0]
    tq = n // batch
    pos = jnp.tile(pos0 + jnp.arange(tq, dtype=jnp.int32), batch)
    q, rows, z = _odd_projections(x, g_norm, wts, pos, head_major_rows=True)
    o = _dil_sample(q, state_all, layer, rows.reshape(batch, tq, 2, DIL_HEADS, HEAD_DIM), batch)
    y = _outproj(x, [o], [z], [wts["w_out"]])
    return y, rows


def kernel(x_prompt, x_sample, cache_nsa_kv, cache_fox_kv, cache_fox_logf, state_nsa_win_kv, state_dil_kv, page_table, norm_even, w_in_even, w_out_even, nsa_gate_bias, fox_forget_bias, nsa_cmp_pos, nsa_qk_gain, fox_qk_gain, norm_odd, w_in_odd, w_out_odd, dil_qk_gain):
    bp, t, d = x_prompt.shape
    bs, tq, _ = x_sample.shape
    n_layers, n_pool, page = cache_nsa_kv.shape[:3]
    pos0 = page_table.shape[1] * page
    depth = norm_even.shape[0] + norm_odd.shape[0]
    cache_nsa = cache_nsa_kv.reshape(n_layers, n_pool, page * NSA_ROWS, HEAD_DIM)
    cache_lf = cache_fox_logf.reshape(n_layers, n_pool, page * FOX_HEADS // LANES, LANES)
    yp = x_prompt.reshape(bp * t, d)
    ys = x_sample.reshape(bs * tq, d)
    outs = [[] for _ in range(9)]
    dil_rows = []
    for layer in range(depth):
        i = layer // 2
        if layer % 2 == 0:
            wts = _even_weights(w_in_even[i], w_out_even[i], nsa_gate_bias[i], fox_forget_bias[i],
                                nsa_qk_gain[i], fox_qk_gain[i])
            yp, new_p = _even_prompt(yp, bp, norm_even[i], wts, nsa_cmp_pos[i])
            ys, new_s = _even_sample(ys, bs, pos0, norm_even[i], wts, nsa_cmp_pos[i], i, cache_nsa,
                                     cache_fox_kv, cache_lf, state_nsa_win_kv[i], page_table)
            for k in range(4):
                outs[2 * k].append(new_p[k])
                outs[2 * k + 1].append(new_s[k])
        else:
            wts = _odd_weights(w_in_odd[i], w_out_odd[i], dil_qk_gain[i])
            yp, buf_p = _odd_prompt(yp, bp, norm_odd[i], wts)
            ys, rows_s = _odd_sample(ys, bs, pos0, norm_odd[i], wts, state_dil_kv, i)
            outs[8].append(buf_p)
            dil_rows.append(rows_s)
    n_odd, _, lw = state_dil_kv.shape[:3]
    n_s = 2 * DIL_HEADS // STATE_TILE
    dil_s = _state_shift(state_dil_kv.reshape(n_odd, bs, lw, n_s, STATE_TILE, HEAD_DIM),
                         jnp.stack(dil_rows).reshape(n_odd, bs, tq, n_s, STATE_TILE, HEAD_DIM))
    return ((yp.reshape(bp, t, d), ys.reshape(bs, tq, d)) + tuple(jnp.stack(o) for o in outs)
            + (dil_s.reshape(state_dil_kv.shape),))
```

```python
import functools
import math

import jax
import jax.numpy as jnp
from jax import lax
from jax.experimental import pallas as pl
from jax.experimental.pallas import tpu as pltpu

F32 = jnp.float32
BF16 = jnp.bfloat16

HEAD_DIM = 128
LANES = 128
NSA_HEADS = 8
NSA_KV_HEADS = 2
NSA_GROUP = NSA_HEADS // NSA_KV_HEADS
NSA_BLOCK = 64
NSA_TOP_N = 16
NSA_WINDOW = 512
NSA_FORCE_SCORE = 1.0e4
FOX_HEADS = 8
DIL_HEADS = 16
DIL_PAIRS = ((128, 1), (512, 4), (2048, 16))
ROPE_THETA = 10000.0
EPS = 1e-6
SCALE = HEAD_DIM ** -0.5
NEG = -1.0e30
LOG2E = math.log2(math.e)
VMEM_LIMIT = 56 * 1024 * 1024
MATMUL_ROWS = 512


def _params(*sem):
    return pltpu.CompilerParams(dimension_semantics=sem, vmem_limit_bytes=VMEM_LIMIT)


def _bdot(a, b):
    return jnp.dot(a.astype(BF16), b.astype(BF16), preferred_element_type=F32)


def _bdot_nt(a, b):
    return lax.dot_general(a.astype(BF16), b.astype(BF16), (((1,), (1,)), ((), ())),
                           preferred_element_type=F32)


def _split3(x):
    hi = x.astype(BF16)
    r1 = x - hi.astype(F32)
    mid = r1.astype(BF16)
    lo = (r1 - mid.astype(F32)).astype(BF16)
    return hi, mid, lo


def _exact_dot(x, m):
    hi, mid, lo = _split3(x)
    return (jnp.dot(hi, m, preferred_element_type=F32)
            + jnp.dot(mid, m, preferred_element_type=F32)
            + jnp.dot(lo, m, preferred_element_type=F32))


def _exact_dot_left(m, x):
    hi, mid, lo = _split3(x)
    return (jnp.dot(m, hi, preferred_element_type=F32)
            + jnp.dot(m, mid, preferred_element_type=F32)
            + jnp.dot(m, lo, preferred_element_type=F32))


def _token_of_row(n_rep, tq):
    return jnp.concatenate([lax.broadcasted_iota(jnp.int32, (tq, 1), 0)] * n_rep, axis=0)


BLOCK_SHIFT = 6


def _sigmoid(x):
    return 1.0 / (1.0 + jnp.exp(-x))


def _log_sigmoid(x):
    return -(jnp.maximum(-x, 0.0) + jnp.log(1.0 + jnp.exp(-jnp.abs(x))))


def _head_rms(y, g):
    return y * lax.rsqrt(jnp.mean(y * y, axis=-1, keepdims=True) + EPS) * g


def _softmax_rows(s, mask):
    s = jnp.where(mask, s, -jnp.inf)
    m = jnp.max(s, axis=-1, keepdims=True)
    m = jnp.where(m > -jnp.inf, m, 0.0)
    e = jnp.exp(s - m)
    den = jnp.sum(e, axis=-1, keepdims=True)
    return e / jnp.maximum(den, 1e-30), m, den


def _rmsnorm_kernel(x_ref, g_ref, h_ref):
    x = x_ref[...]
    y = x * lax.rsqrt(jnp.mean(x * x, axis=-1, keepdims=True) + EPS)
    h_ref[...] = (y * g_ref[...]).astype(BF16)


def _rmsnorm(x, g):
    n, d = x.shape
    tm = min(n, 512)
    return pl.pallas_call(
        _rmsnorm_kernel,
        grid=(n // tm,),
        in_specs=[pl.BlockSpec((tm, d), lambda i: (i, 0)),
                  pl.BlockSpec((1, d), lambda i: (0, 0))],
        out_specs=pl.BlockSpec((tm, d), lambda i: (i, 0)),
        out_shape=jax.ShapeDtypeStruct((n, d), BF16),
        compiler_params=_params("parallel"),
        name="rmsnorm",
    )(x, g.reshape(1, d))


def _proj_kernel(h_ref, w_ref, aux_ref, cos_ref, sin_ref, *out_refs, plans, n_tiles, head_major):
    acc = jnp.dot(h_ref[...], w_ref[...], preferred_element_type=F32)
    tm = h_ref.shape[0]

    def emit(tile):
        for out_ref, plan, hm in zip(out_refs, plans, head_major):
            n_c = len(plan[tile])
            for c, (src, op, gi) in enumerate(plan[tile]):
                y = acc[:, src * LANES:(src + 1) * LANES]
                if op in ("rms", "rms_rope"):
                    y = _head_rms(y, aux_ref[gi:gi + 1, :])
                if op == "rms_rope":
                    y = y * cos_ref[...] + pltpu.roll(y, HEAD_DIM // 2, 1) * sin_ref[...]
                if op == "logsig":
                    y = _log_sigmoid(y + aux_ref[gi:gi + 1, :])
                if hm:
                    out_ref[pl.ds(tile * n_c + c, tm, stride=n_tiles * n_c), :] = y
                else:
                    out_ref[:, c * LANES:(c + 1) * LANES] = y

    if n_tiles == 1:
        emit(0)
    else:
        for t in range(n_tiles):
            pl.when(pl.program_id(0) == t)(functools.partial(emit, t))


def _proj(h, w, aux, cos2, sin2, plans, tn, head_major=None):
    n, k = h.shape
    n_tiles = w.shape[1] // tn
    tm = min(n, MATMUL_ROWS)
    head_major = head_major or [False] * len(plans)
    out_specs, out_shape = [], []
    for plan, hm in zip(plans, head_major):
        n_c = len(plan[0])
        if hm:
            assert n_tiles == 1 or n == tm
            out_specs.append(pl.BlockSpec((tm * n_tiles * n_c, LANES), lambda j, i: (i, 0)))
            out_shape.append(jax.ShapeDtypeStruct((n * n_tiles * n_c, LANES), F32))
        else:
            out_specs.append(pl.BlockSpec((tm, n_c * LANES), lambda j, i: (i, j)))
            out_shape.append(jax.ShapeDtypeStruct((n, n_tiles * n_c * LANES), F32))
    kern = functools.partial(_proj_kernel, plans=plans, n_tiles=n_tiles, head_major=head_major)
    outs = pl.pallas_call(
        kern,
        grid=(n_tiles, n // tm),
        in_specs=[pl.BlockSpec((tm, k), lambda j, i: (i, 0)),
                  pl.BlockSpec((k, tn), lambda j, i: (0, j), pipeline_mode=pl.Buffered(1)),
                  pl.BlockSpec(aux.shape, lambda j, i: (0, 0)),
                  pl.BlockSpec((tm, LANES), lambda j, i: (i, 0)),
                  pl.BlockSpec((tm, LANES), lambda j, i: (i, 0))],
        out_specs=out_specs,
        out_shape=out_shape,
        compiler_params=_params("arbitrary", "arbitrary"),
        name="proj",
    )(h, w, aux, cos2, sin2)
    return outs


def _outproj_kernel(*refs, n_parts):
    x_ref = refs[0]
    y_ref = refs[-1]
    acc = x_ref[...]
    for p in range(n_parts):
        o = refs[1 + p][...]
        z = refs[1 + n_parts + p][...]
        mix = o * (z * _sigmoid(z))
        acc = acc + jnp.dot(mix.astype(BF16), refs[1 + 2 * n_parts + p][...],
                            preferred_element_type=F32)
    y_ref[...] = acc


def _outproj(x, o_parts, z_parts, w_parts):
    n, d = x.shape
    tm = min(n, MATMUL_ROWS)
    n_parts = len(o_parts)
    row = lambda a: pl.BlockSpec((tm, a.shape[1]), lambda i: (i, 0))
    return pl.pallas_call(
        functools.partial(_outproj_kernel, n_parts=n_parts),
        grid=(n // tm,),
        in_specs=([row(x)] + [row(o) for o in o_parts] + [row(z) for z in z_parts]
                  + [pl.BlockSpec(w.shape, lambda i: (0, 0), pipeline_mode=pl.Buffered(1))
                     for w in w_parts]),
        out_specs=row(x),
        out_shape=jax.ShapeDtypeStruct((n, d), F32),
        compiler_params=_params("parallel"),
        name="outproj",
    )(x, *o_parts, *z_parts, *w_parts)


CUM_BLOCK = 256


def _cumsum_kernel(x_ref, tri_ref, c_ref, carry_ref):
    @pl.when(pl.program_id(1) == 0)
    def _():
        carry_ref[...] = jnp.zeros_like(carry_ref)

    c = _exact_dot_left(tri_ref[...], x_ref[...]) + carry_ref[...]
    c_ref[...] = c
    carry_ref[...] = c[CUM_BLOCK - 1:CUM_BLOCK, :]


def _cumsum_rows(x, batch):
    n, w = x.shape
    t = n // batch
    nb = t // CUM_BLOCK
    r = lax.broadcasted_iota(jnp.int32, (CUM_BLOCK, CUM_BLOCK), 0)
    c = lax.broadcasted_iota(jnp.int32, (CUM_BLOCK, CUM_BLOCK), 1)
    tri = (c <= r).astype(BF16)
    return pl.pallas_call(
        _cumsum_kernel,
        grid=(batch, nb),
        in_specs=[pl.BlockSpec((CUM_BLOCK, w), lambda b, j: (b * nb + j, 0)),
                  pl.BlockSpec((CUM_BLOCK, CUM_BLOCK), lambda b, j: (0, 0))],
        out_specs=pl.BlockSpec((CUM_BLOCK, w), lambda b, j: (b * nb + j, 0)),
        out_shape=jax.ShapeDtypeStruct((n, w), F32),
        scratch_shapes=[pltpu.VMEM((1, w), F32)],
        compiler_params=_params("parallel", "arbitrary"),
        name="logf_cumsum",
    )(x, tri)


FOX_TILE = 512


def _lanes(col):
    return jnp.broadcast_to(col, (col.shape[0], LANES))


def _fox_prompt_kernel(q_ref, k_ref, v_ref, cq_ref, ck_ref, o_ref, m_ref, l_ref, acc_ref, cq2_ref):
    i = pl.program_id(1)
    j = pl.program_id(2)
    tq = q_ref.shape[0]
    tk = k_ref.shape[0]

    @pl.when(j == 0)
    def _():
        m_ref[...] = jnp.full_like(m_ref, NEG)
        l_ref[...] = jnp.zeros_like(l_ref)
        acc_ref[...] = jnp.zeros_like(acc_ref)
        for h in range(FOX_HEADS):
            cq2_ref[h] = _lanes(cq_ref[:, h:h + 1] * LOG2E)

    def step(causal):
        if causal:
            mask = (lax.broadcasted_iota(jnp.int32, (tq, tk), 0)
                    >= lax.broadcasted_iota(jnp.int32, (tq, tk), 1))
        for h in range(FOX_HEADS):
            sl = slice(h * HEAD_DIM, (h + 1) * HEAD_DIM)
            u = _bdot_nt(q_ref[:, sl], k_ref[:, sl]) * (SCALE * LOG2E) - ck_ref[0, h:h + 1, :] * LOG2E
            if causal:
                u = jnp.where(mask, u, NEG)
            cq2 = cq2_ref[h]
            m_old = m_ref[h]
            m_new = jnp.maximum(m_old, _lanes(jnp.max(u, axis=-1, keepdims=True)) + cq2)
            p = jnp.exp2(u - jnp.tile(m_new - cq2, (1, tk // LANES)))
            alpha = jnp.exp2(m_old - m_new)
            l_ref[h] = alpha * l_ref[h] + _lanes(jnp.sum(p, axis=-1, keepdims=True))
            acc_ref[:, sl] = alpha * acc_ref[:, sl] + _bdot(p, v_ref[:, sl])
            m_ref[h] = m_new

    pl.when(j < i)(functools.partial(step, False))
    pl.when(j == i)(functools.partial(step, True))

    @pl.when(j == i)
    def _():
        for h in range(FOX_HEADS):
            sl = slice(h * HEAD_DIM, (h + 1) * HEAD_DIM)
            o_ref[:, sl] = acc_ref[:, sl] / jnp.maximum(l_ref[h], 1e-30)


def _fox_prompt(qb, rows_fox, c_all, ck_t, batch):
    n, w = qb.shape
    t = n // batch
    tq = min(FOX_TILE, t)
    nq = t // tq
    return pl.pallas_call(
        _fox_prompt_kernel,
        grid=(batch, nq, nq),
        in_specs=[pl.BlockSpec((tq, w), lambda b, i, j: (b * nq + i, 0)),
                  pl.BlockSpec((tq, w), lambda b, i, j: (b * nq + jnp.minimum(j, i), 0)),
                  pl.BlockSpec((tq, w), lambda b, i, j: (b * nq + jnp.minimum(j, i), 1)),
                  pl.BlockSpec((tq, LANES), lambda b, i, j: (b * nq + i, 0)),
                  pl.BlockSpec((1, FOX_HEADS, tq), lambda b, i, j: (b, 0, jnp.minimum(j, i)))],
        out_specs=pl.BlockSpec((tq, w), lambda b, i, j: (b * nq + i, 0)),
        out_shape=jax.ShapeDtypeStruct((n, w), F32),
        scratch_shapes=[pltpu.VMEM((FOX_HEADS, tq, LANES), F32),
                        pltpu.VMEM((FOX_HEADS, tq, LANES), F32),
                        pltpu.VMEM((tq, w), F32),
                        pltpu.VMEM((FOX_HEADS, tq, LANES), F32)],
        compiler_params=_params("parallel", "parallel", "arbitrary"),
        name="fox_prompt",
    )(qb, rows_fox, rows_fox, c_all, ck_t)


def _stack_heads(x, n):
    return jnp.concatenate([x[:, r * HEAD_DIM:(r + 1) * HEAD_DIM] for r in range(n)], axis=0)


def _top_n_mask(imp, blk, n_sel):
    work = imp
    selm = jnp.zeros(imp.shape, F32)
    for _ in range(n_sel):
        m = jnp.max(work, axis=-1, keepdims=True)
        idx = jnp.min(jnp.where(work == m, blk, 1.0e9), axis=-1, keepdims=True)
        pick = blk == idx
        selm = jnp.where(pick, 1.0, selm)
        work = jnp.where(pick, -jnp.inf, work)
    return selm


def _top_n_mask_t(imp, n_sel):
    work = imp.T
    blk = lax.broadcasted_iota(jnp.int32, work.shape, 0).astype(F32)
    selm = jnp.zeros(work.shape, F32)
    for _ in range(n_sel):
        m = jnp.max(work, axis=0, keepdims=True)
        idx = jnp.min(jnp.where(work == m, blk, 1.0e9), axis=0, keepdims=True)
        pick = blk == idx
        selm = jnp.where(pick, 1.0, selm)
        work = jnp.where(pick, -jnp.inf, work)
    return selm.T


def _importance(p_c, rows, qpos_col, blk):
    imp = p_c[0:rows]
    for r in range(1, NSA_GROUP):
        imp = imp + p_c[r * rows:(r + 1) * rows]
    cur = (qpos_col >> BLOCK_SHIFT).astype(F32)
    forced = (blk == 0.0) | (blk == cur) | (blk == cur - 1.0)
    imp = jnp.where(forced, NSA_FORCE_SCORE, imp)
    return jnp.where(blk <= cur, imp, NEG)


NSA_Q = 128
NSA_TK = 512
NSA_WKEYS = NSA_WINDOW + NSA_Q


def _nsa_prompt_kernel(qn_ref, qr_ref, kc_ref, vc_ref, ks_ref, vs_ref, kw_ref, vw_ref,
                       gate_ref, gbias_ref, cpos_ref, gkc_ref, e_ref, o_ref,
                       kcn_ref, vcm_ref):
    i = pl.program_id(2)
    t = kc_ref.shape[0]
    nb = t // NSA_BLOCK
    nbp = kcn_ref.shape[0]

    @pl.when(i == 0)
    def _():
        kcn_ref[...] = jnp.zeros_like(kcn_ref)
        vcm_ref[...] = jnp.zeros_like(vcm_ref)

        def body(jb, carry):
            rows = pl.ds(pl.multiple_of(jb * NSA_BLOCK, NSA_BLOCK), NSA_BLOCK)
            km = jnp.mean(kc_ref[rows, :] + cpos_ref[...], axis=0, keepdims=True)
            kcn_ref[pl.ds(jb, 1), :] = _head_rms(km, gkc_ref[...])
            vcm_ref[pl.ds(jb, 1), :] = jnp.mean(vc_ref[rows, :], axis=0, keepdims=True)
            return carry

        lax.fori_loop(0, nb, body, 0)

    rq = NSA_GROUP * NSA_Q
    qn = _stack_heads(qn_ref[...], NSA_GROUP)
    qr = _stack_heads(qr_ref[...], NSA_GROUP).astype(BF16)
    qpos1 = i * NSA_Q + lax.broadcasted_iota(jnp.int32, (NSA_Q, 1), 0)
    qpos = jnp.concatenate([qpos1] * NSA_GROUP, axis=0)

    blk_i = lax.broadcasted_iota(jnp.int32, (1, nbp), 1)
    s = _bdot_nt(qn, kcn_ref[...]) * SCALE
    p_c, _, _ = _softmax_rows(s, (blk_i + 1) * NSA_BLOCK - 1 <= qpos)
    o_c = _bdot(p_c, vcm_ref[...])

    blk = blk_i.astype(F32)
    imp = _importance(p_c, NSA_Q, qpos1, blk)
    selm = _top_n_mask_t(imp, min(NSA_TOP_N, nb)).astype(BF16)

    def sel_chunk(c, carry, causal):
        m_old, l_old, acc = carry
        rows = pl.ds(pl.multiple_of(c * NSA_TK, NSA_TK), NSA_TK)
        kmask = jnp.dot(selm, e_ref[c], preferred_element_type=F32)
        bias = (kmask - 1.0) * (-NEG)
        if causal:
            kpos = c * NSA_TK + lax.broadcasted_iota(jnp.int32, (1, NSA_TK), 1)
            bias = jnp.where(kpos <= qpos1, bias, NEG)
        sc = _bdot_nt(qr, ks_ref[rows, :]) * SCALE + jnp.concatenate([bias] * NSA_GROUP, axis=0)
        m_new = jnp.maximum(m_old, _lanes(jnp.max(sc, axis=-1, keepdims=True)))
        p = jnp.exp(sc - jnp.tile(m_new, (1, NSA_TK // LANES)))
        alpha = jnp.exp(m_old - m_new)
        l_new = alpha * l_old + _lanes(jnp.sum(p, axis=-1, keepdims=True))
        acc = alpha * acc + _bdot(p, vs_ref[rows, :])
        return m_new, l_new, acc

    last = (i * NSA_Q) // NSA_TK
    init = (jnp.full((rq, LANES), NEG, F32), jnp.zeros((rq, LANES), F32),
            jnp.zeros((rq, HEAD_DIM), F32))
    carry = lax.fori_loop(0, last, functools.partial(sel_chunk, causal=False), init)
    _, l_s, acc_s = sel_chunk(last, carry, True)
    o_s = acc_s / jnp.maximum(l_s, 1e-30)

    k0 = jnp.clip(i * NSA_Q - NSA_WINDOW, 0, t - NSA_WKEYS)
    k0 = pl.multiple_of(k0, NSA_Q)
    wrows = pl.ds(k0, NSA_WKEYS)
    kpos = k0 + lax.broadcasted_iota(jnp.int32, (1, NSA_WKEYS), 1)
    dist = qpos - kpos
    sw = jnp.where((dist >= 0) & (dist < NSA_WINDOW), _bdot_nt(qr, kw_ref[wrows, :]) * SCALE, NEG)
    m_w = _lanes(jnp.max(sw, axis=-1, keepdims=True))
    e_w = jnp.exp(sw - jnp.tile(m_w, (1, NSA_WKEYS // LANES)))
    o_w = _bdot(e_w, vw_ref[wrows, :]) / _lanes(jnp.sum(e_w, axis=-1, keepdims=True))

    gates = _sigmoid(gate_ref[...] + gbias_ref[...])
    for r in range(NSA_GROUP):
        rs = slice(r * NSA_Q, (r + 1) * NSA_Q)
        o = (gates[:, 3 * r:3 * r + 1] * o_c[rs] + gates[:, 3 * r + 1:3 * r + 2] * o_s[rs]
             + gates[:, 3 * r + 2:3 * r + 3] * o_w[rs])
        o_ref[:, r * HEAD_DIM:(r + 1) * HEAD_DIM] = o


def _expand_matrix(nbp, n_keys, tk):
    kb = (jnp.arange(n_keys, dtype=jnp.int32) // NSA_BLOCK).reshape(n_keys // tk, 1, tk)
    jb = jnp.arange(nbp, dtype=jnp.int32).reshape(1, nbp, 1)
    return (kb == jb).astype(BF16)


def _nsa_prompt(qn, qr, rows_nsa, rows_win, gates, gbias, cmp_pos, g_kc, batch):
    n = qn.shape[0]
    t = n // batch
    assert t % NSA_TK == 0 and t >= NSA_WKEYS
    nqb = t // NSA_Q
    nb = t // NSA_BLOCK
    nbp = -(-nb // LANES) * LANES
    e = _expand_matrix(nbp, t, NSA_TK)
    gw = NSA_GROUP * HEAD_DIM
    qspec = pl.BlockSpec((NSA_Q, gw), lambda b, g, i: (b * nqb + i, g))
    kv = lambda typ: pl.BlockSpec((t, HEAD_DIM), lambda b, g, i: (b, typ * NSA_KV_HEADS + g))
    return pl.pallas_call(
        _nsa_prompt_kernel,
        grid=(batch, NSA_KV_HEADS, nqb),
        in_specs=[qspec, qspec, kv(0), kv(1), kv(2), kv(3), kv(0), kv(1),
                  pl.BlockSpec((NSA_Q, LANES), lambda b, g, i: (b * nqb + i, g)),
                  pl.BlockSpec((1, LANES), lambda b, g, i: (0, g)),
                  pl.BlockSpec(cmp_pos.shape, lambda b, g, i: (0, 0)),
                  pl.BlockSpec((1, HEAD_DIM), lambda b, g, i: (0, 0)),
                  pl.BlockSpec(e.shape, lambda b, g, i: (0, 0, 0))],
        out_specs=qspec,
        out_shape=jax.ShapeDtypeStruct((n, NSA_HEADS * HEAD_DIM), F32),
        scratch_shapes=[pltpu.VMEM((nbp, HEAD_DIM), F32), pltpu.VMEM((nbp, HEAD_DIM), F32)],
        compiler_params=_params("parallel", "parallel", "arbitrary"),
        name="nsa_prompt",
    )(qn, qr, rows_nsa, rows_nsa, rows_nsa, rows_nsa, rows_win, rows_win,
      gates, gbias, cmp_pos, g_kc, e)


DIL_Q = 128
DIL_K = 2 * DIL_Q
DIL_UNROLL = 8


def _dil_prompt_kernel(q_ref, k_ref, v_ref, o_ref, og_ref, lse_ref):
    t = q_ref.shape[0]
    for gi, (window, dil) in enumerate(DIL_PAIRS):
        n_str = t // dil
        n_qb = n_str // DIL_Q
        span = window // dil

        def body(it, carry, gi=gi, dil=dil, n_qb=n_qb, span=span, n_str=n_str):
            r = it // n_qb
            ib = it % n_qb
            q0 = ib * DIL_Q
            k0 = jnp.clip(q0 - DIL_Q, 0, n_str - DIL_K)
            qrows = pl.ds(r + q0 * dil, DIL_Q, stride=dil) if dil > 1 else pl.ds(q0, DIL_Q)
            krows = pl.ds(r + k0 * dil, DIL_K, stride=dil) if dil > 1 else pl.ds(k0, DIL_K)
            s = _bdot_nt(q_ref[qrows, :], k_ref[krows, :]) * SCALE
            qi = q0 + lax.broadcasted_iota(jnp.int32, (DIL_Q, 1), 0)
            kj = k0 + lax.broadcasted_iota(jnp.int32, (1, DIL_K), 1)
            d = qi - kj
            s = jnp.where((d >= 0) & (d <= span), s, NEG)
            m = _lanes(jnp.max(s, axis=-1, keepdims=True))
            e = jnp.exp(s - jnp.tile(m, (1, DIL_K // LANES)))
            den = _lanes(jnp.sum(e, axis=-1, keepdims=True))
            og_ref[gi, qrows, :] = _bdot(e, v_ref[krows, :]) / den
            lse_ref[gi, qrows, :] = m + jnp.log(den)
            return carry

        lax.fori_loop(0, dil * n_qb, body, 0, unroll=DIL_UNROLL)

    lse = lse_ref[...]
    mx = jnp.max(lse, axis=0, keepdims=True)
    w = jnp.exp(lse - mx)
    w = w / jnp.sum(w, axis=0, keepdims=True)
    o_ref[...] = jnp.sum(w * og_ref[...], axis=0)


def _dil_prompt(q, rows, batch):
    n, w = q.shape
    t = n // batch
    assert t % (DIL_K * DIL_PAIRS[-1][1]) == 0
    blk = lambda off: pl.BlockSpec((t, HEAD_DIM), lambda b, h: (b, h + off))
    return pl.pallas_call(
        _dil_prompt_kernel,
        grid=(batch, DIL_HEADS),
        in_specs=[blk(0), blk(0), blk(DIL_HEADS)],
        out_specs=blk(0),
        out_shape=jax.ShapeDtypeStruct((n, w), F32),
        scratch_shapes=[pltpu.VMEM((len(DIL_PAIRS), t, HEAD_DIM), F32),
                        pltpu.VMEM((len(DIL_PAIRS), t, LANES), F32)],
        compiler_params=_params("parallel", "parallel"),
        name="dil_prompt",
    )(q, rows, rows)


NSA_ROWS = 4 * NSA_KV_HEADS
MEANS_PAGES = 16


def _page_rows(page_ref, typ, g, pos0, n_pos):
    return page_ref[0, 0, pl.ds(pos0 * NSA_ROWS + typ * NSA_KV_HEADS + g, n_pos, stride=NSA_ROWS), :]


def _page_means_kernel(pt_ref, *refs, n_pg):
    del pt_ref
    page_refs, cpos_ref, o_ref = refs[:n_pg], refs[n_pg], refs[n_pg + 1]
    n_blk = page_refs[0].shape[2] // (NSA_ROWS * NSA_BLOCK)
    for k in range(n_pg):
        for j in range(n_blk):
            parts = []
            for typ in range(2):
                for g in range(NSA_KV_HEADS):
                    x = _page_rows(page_refs[k], typ, g, j * NSA_BLOCK, NSA_BLOCK)
                    if typ == 0:
                        x = x + cpos_ref[...]
                    parts.append(jnp.mean(x, axis=0, keepdims=True))
            o_ref[0, k * n_blk + j:k * n_blk + j + 1, :] = jnp.concatenate(parts, axis=1)


def _page_means(cache, layer, page_table, cmp_pos):
    b, n_pages = page_table.shape
    rows = cache.shape[2]
    n_blk = rows // (NSA_ROWS * NSA_BLOCK)
    n_pg = math.gcd(MEANS_PAGES, n_pages)
    w = 2 * NSA_KV_HEADS * HEAD_DIM
    page_spec = lambda k: pl.BlockSpec(
        (1, 1, rows, LANES), lambda bi, p, pt: (layer, pt[bi, p * n_pg + k], 0, 0))
    grid_spec = pltpu.PrefetchScalarGridSpec(
        num_scalar_prefetch=1,
        grid=(b, n_pages // n_pg),
        in_specs=[page_spec(k) for k in range(n_pg)]
        + [pl.BlockSpec(cmp_pos.shape, lambda bi, p, pt: (0, 0))],
        out_specs=pl.BlockSpec((1, n_pg * n_blk, w), lambda bi, p, pt: (bi, p, 0)),
    )
    return pl.pallas_call(
        functools.partial(_page_means_kernel, n_pg=n_pg),
        grid_spec=grid_spec,
        out_shape=jax.ShapeDtypeStruct((b, n_pages * n_blk, w), F32),
        compiler_params=_params("parallel", "arbitrary"),
        name="page_means",
    )(page_table, *([cache] * n_pg), cmp_pos)


def _nsa_cmp_sample_kernel(qn_ref, means_ref, gkc_ref, oc_ref, sel_ref, *, pos0):
    tq = qn_ref.shape[0]
    nbp = means_ref.shape[1]
    qpos1 = pos0 + lax.broadcasted_iota(jnp.int32, (tq, 1), 0)
    qpos = jnp.concatenate([qpos1] * NSA_GROUP, axis=0)
    blk_i = lax.broadcasted_iota(jnp.int32, (1, nbp), 1)
    blk = blk_i.astype(F32)
    gw = NSA_GROUP * HEAD_DIM
    half = NSA_KV_HEADS * HEAD_DIM
    for g in range(NSA_KV_HEADS):
        qn = _stack_heads(qn_ref[:, g * gw:(g + 1) * gw], NSA_GROUP)
        kc = _head_rms(means_ref[0, :, g * HEAD_DIM:(g + 1) * HEAD_DIM], gkc_ref[...])
        vc = means_ref[0, :, half + g * HEAD_DIM:half + (g + 1) * HEAD_DIM]
        s = _bdot_nt(qn, kc) * SCALE
        p_c, _, _ = _softmax_rows(s, (blk_i + 1) * NSA_BLOCK - 1 <= qpos)
        o_c = _bdot(p_c, vc)
        for r in range(NSA_GROUP):
            oc_ref[:, g * gw + r * HEAD_DIM:g * gw + (r + 1) * HEAD_DIM] = o_c[r * tq:(r + 1) * tq]
        imp = _importance(p_c, tq, qpos1, blk)
        sel_ref[0, g * tq:(g + 1) * tq, :] = _top_n_mask(imp, blk, NSA_TOP_N)


def _nsa_cmp_sample(qn, means, g_kc, batch, pos0):
    n, w = qn.shape
    tq = n // batch
    nbp = means.shape[1]
    return pl.pallas_call(
        functools.partial(_nsa_cmp_sample_kernel, pos0=pos0),
        grid=(batch,),
        in_specs=[pl.BlockSpec((tq, w), lambda b: (b, 0)),
                  pl.BlockSpec((1, nbp, means.shape[2]), lambda b: (b, 0, 0)),
                  pl.BlockSpec((1, HEAD_DIM), lambda b: (0, 0))],
        out_specs=[pl.BlockSpec((tq, w), lambda b: (b, 0)),
                   pl.BlockSpec((1, NSA_KV_HEADS * tq, nbp), lambda b: (b, 0, 0))],
        out_shape=[jax.ShapeDtypeStruct((n, w), F32),
                   jax.ShapeDtypeStruct((batch, NSA_KV_HEADS * tq, nbp), F32)],
        compiler_params=_params("parallel"),
        name="nsa_cmp_sample",
    )(qn, means, g_kc)


def _block_diag_queries(q, n_kv, reps):
    parts = []
    for g in range(n_kv):
        for r in range(reps):
            hq = q[:, (g * reps + r) * HEAD_DIM:(g * reps + r + 1) * HEAD_DIM]
            parts.append(jnp.concatenate(
                [hq if gg == g else jnp.zeros_like(hq) for gg in range(n_kv)], axis=1))
    return jnp.concatenate(parts, axis=0).astype(BF16)


def _rep_rows(x, n_kv, per):
    return jnp.concatenate(
        [jnp.broadcast_to(x[g:g + 1, :], (per, x.shape[1])) for g in range(n_kv)], axis=0)


def _online_update(s, mask, v, m_ref, l_ref, acc_ref):
    if mask is not None:
        s = jnp.where(mask, s, NEG)
    m_old = m_ref[...]
    m_new = jnp.maximum(m_old, jnp.max(s, axis=-1, keepdims=True))
    pr = jnp.exp(s - m_new)
    if mask is not None:
        pr = jnp.where(mask, pr, 0.0)
    alpha = jnp.exp(m_old - m_new)
    l_ref[...] = alpha * l_ref[...] + jnp.sum(pr, axis=-1, keepdims=True)
    acc_ref[...] = alpha * acc_ref[...] + _bdot(pr, v)
    m_ref[...] = m_new


def _pad_rows(x, rows):
    return jnp.concatenate([x, jnp.zeros((rows - x.shape[0], x.shape[1]), x.dtype)], axis=0)


def _write_diag(o_ref, l_ref, acc_ref, n_kv, reps, tq):
    for g in range(n_kv):
        for r in range(reps):
            rs = slice((g * reps + r) * tq, (g * reps + r + 1) * tq)
            o = acc_ref[rs, g * HEAD_DIM:(g + 1) * HEAD_DIM] / jnp.maximum(l_ref[rs, :], 1e-30)
            o_ref[:, (g * reps + r) * HEAD_DIM:(g * reps + r + 1) * HEAD_DIM] = o


FOX_PAGES = 8
HEAD_SHIFT = 3


def _fox_sample_kernel(pt_ref, q_ref, *refs, n_pg):
    del pt_ref
    page_refs, lf_refs = refs[:n_pg], refs[n_pg:2 * n_pg]
    new_ref, lfn_ref, o_ref, cq_ref, carry_ref, m_ref, l_ref, acc_ref = refs[2 * n_pg:]
    p = pl.program_id(1)
    tq = q_ref.shape[0]
    page = page_refs[0].shape[2]
    rows = FOX_HEADS * tq
    row_h = jnp.concatenate([jnp.full((tq, 1), h, jnp.int32) for h in range(FOX_HEADS)], axis=0)
    q_all = _stack_heads(q_ref[...], FOX_HEADS).astype(BF16)
    pr_i = lax.broadcasted_iota(jnp.int32, (LANES, LANES), 0)
    pc_i = lax.broadcasted_iota(jnp.int32, (LANES, LANES), 1)

    @pl.when(p == 0)
    def _():
        carry_ref[...] = jnp.zeros_like(carry_ref)
        lane = lax.broadcasted_iota(jnp.int32, (1, LANES), 1)
        cnew = _exact_dot(lfn_ref[0], (pr_i <= pc_i).astype(BF16))
        cnew_rows = _rep_rows(cnew, FOX_HEADS, tq)
        row_t = _token_of_row(FOX_HEADS, tq)
        cq = jnp.sum(jnp.where(lane == row_t, cnew_rows, 0.0), axis=-1, keepdims=True)
        cq_ref[...] = cq
        kw = FOX_HEADS * HEAD_DIM
        qbd = _block_diag_queries(q_ref[...], FOX_HEADS, 1)
        kn = _pad_rows(new_ref[:, 0:kw], LANES)
        vn = _pad_rows(new_ref[:, kw:2 * kw], LANES)
        mask = lane <= row_t
        s = jnp.where(mask, _bdot_nt(qbd, kn) * SCALE + (cq - cnew_rows), NEG)
        m = jnp.max(s, axis=-1, keepdims=True)
        pr = jnp.where(mask, jnp.exp(s - m), 0.0)
        o_full = _bdot(pr, vn)
        m_ref[...] = m
        l_ref[...] = jnp.sum(pr, axis=-1, keepdims=True)
        acc_ref[...] = jnp.concatenate(
            [o_full[h * tq:(h + 1) * tq, h * HEAD_DIM:(h + 1) * HEAD_DIM] for h in range(FOX_HEADS)],
            axis=0)

    same_head = (pr_i & (FOX_HEADS - 1)) == (pc_i & (FOX_HEADS - 1))
    later_in_row = (same_head & ((pr_i >> HEAD_SHIFT) > (pc_i >> HEAD_SHIFT))).astype(BF16)
    row_total = same_head.astype(BF16)
    n_r = page * FOX_HEADS // LANES
    lane_k = lax.broadcasted_iota(jnp.int32, (1, n_pg * page * FOX_HEADS), 1)
    head_ok = (lane_k & (FOX_HEADS - 1)) == row_h
    cq = cq_ref[...]
    carry = carry_ref[...]
    s_parts, v_parts = [], []
    for k in range(n_pg):
        lf = lf_refs[k][0, 0]
        within = _exact_dot(lf, later_in_row)
        tot = _exact_dot(lf, row_total)
        run = carry
        later = [None] * n_r
        for r in reversed(range(n_r)):
            later[r] = run
            run = run + tot[r:r + 1, :]
        carry = run
        suffix = within + jnp.concatenate(later, axis=0)
        xk = page_refs[k][0, 0, :, 0].reshape(page * FOX_HEADS, HEAD_DIM)
        s_all = _bdot_nt(q_all, xk) * SCALE
        s_parts += [s_all[:, r * LANES:(r + 1) * LANES] + (cq + suffix[r:r + 1, :]) for r in range(n_r)]
        v_parts.append(page_refs[k][0, 0, :, 1].reshape(page * FOX_HEADS, HEAD_DIM))
    carry_ref[...] = carry
    s = jnp.concatenate(s_parts, axis=1)
    _online_update(s, head_ok, jnp.concatenate(v_parts, axis=0), m_ref, l_ref, acc_ref)

    @pl.when(p == pl.num_programs(1) - 1)
    def _():
        o = acc_ref[...] / jnp.maximum(l_ref[...], 1e-30)
        for h in range(FOX_HEADS):
            o_ref[:, h * HEAD_DIM:(h + 1) * HEAD_DIM] = o[h * tq:(h + 1) * tq]


def _fox_sample(qb, cache_kv, cache_lf, layer, page_table, rows_new, lf_new_t):
    n, w = qb.shape
    b, n_pages = page_table.shape
    tq = n // b
    page = cache_kv.shape[2]
    rows = FOX_HEADS * tq
    n_pg = math.gcd(FOX_PAGES, n_pages)
    assert tq <= LANES and (page * FOX_HEADS) % LANES == 0
    pg6 = lambda k: (lambda bi, p, pt: (layer, pt[bi, n_pages - 1 - (p * n_pg + k)], 0, 0, 0, 0))
    pg4 = lambda k: (lambda bi, p, pt: (layer, pt[bi, n_pages - 1 - (p * n_pg + k)], 0, 0))
    grid_spec = pltpu.PrefetchScalarGridSpec(
        num_scalar_prefetch=1,
        grid=(b, n_pages // n_pg),
        in_specs=[pl.BlockSpec((tq, w), lambda bi, p, pt: (bi, 0))]
        + [pl.BlockSpec((1, 1) + cache_kv.shape[2:], pg6(k)) for k in range(n_pg)]
        + [pl.BlockSpec((1, 1) + cache_lf.shape[2:], pg4(k)) for k in range(n_pg)]
        + [pl.BlockSpec((tq, 2 * w), lambda bi, p, pt: (bi, 0)),
           pl.BlockSpec((1, FOX_HEADS, LANES), lambda bi, p, pt: (bi, 0, 0))],
        out_specs=pl.BlockSpec((tq, w), lambda bi, p, pt: (bi, 0)),
        scratch_shapes=[pltpu.VMEM((rows, 1), F32), pltpu.VMEM((1, LANES), F32),
                        pltpu.VMEM((rows, 1), F32), pltpu.VMEM((rows, 1), F32),
                        pltpu.VMEM((rows, HEAD_DIM), F32)],
    )
    return pl.pallas_call(
        functools.partial(_fox_sample_kernel, n_pg=n_pg),
        grid_spec=grid_spec,
        out_shape=jax.ShapeDtypeStruct((n, w), F32),
        compiler_params=_params("parallel", "arbitrary"),
        name="fox_sample",
    )(page_table, qb, *([cache_kv] * n_pg), *([cache_lf] * n_pg), rows_new, lf_new_t)


SEL_PAGES = 8


def _nsa_sel_sample_kernel(pt_ref, q_ref, *refs, n_pg, pos0):
    del pt_ref
    page_refs = refs[:n_pg]
    new_ref, sel_ref, o_ref, qbd_ref, m_ref, l_ref, acc_ref = refs[n_pg:]
    p = pl.program_id(1)
    tq = q_ref.shape[0]
    page = page_refs[0].shape[2] // NSA_ROWS
    kw = NSA_KV_HEADS * HEAD_DIM
    nbp = sel_ref.shape[2]
    selm = sel_ref[0].astype(BF16)

    def key_mask(first_block, n_keys):
        blk_r = lax.broadcasted_iota(jnp.int32, (nbp, n_keys), 0)
        blk_of_lane = lax.broadcasted_iota(jnp.int32, (nbp, n_keys), 1) >> BLOCK_SHIFT
        expand = (blk_r == first_block + blk_of_lane).astype(BF16)
        km = jnp.dot(selm, expand, preferred_element_type=F32)
        return jnp.concatenate(
            [km[g * tq:(g + 1) * tq] for g in range(NSA_KV_HEADS) for _ in range(NSA_GROUP)],
            axis=0) > 0.5

    @pl.when(p == 0)
    def _():
        m_ref[...] = jnp.full_like(m_ref, NEG)
        l_ref[...] = jnp.zeros_like(l_ref)
        acc_ref[...] = jnp.zeros_like(acc_ref)
        qbd = _block_diag_queries(q_ref[...], NSA_KV_HEADS, NSA_GROUP)
        qbd_ref[...] = qbd
        row_t = _token_of_row(NSA_HEADS, tq)
        lane = lax.broadcasted_iota(jnp.int32, (1, LANES), 1)
        kn = _pad_rows(new_ref[:, 0:kw], LANES)
        vn = _pad_rows(new_ref[:, kw:2 * kw], LANES)
        s = _bdot_nt(qbd, kn) * SCALE
        _online_update(s, key_mask(pos0 // NSA_BLOCK, LANES) & (lane <= row_t), vn,
                       m_ref, l_ref, acc_ref)

    def slab(typ):
        return jnp.concatenate(
            [jnp.concatenate([_page_rows(page_refs[k], typ, g, 0, page) for g in range(NSA_KV_HEADS)],
                             axis=1) for k in range(n_pg)], axis=0)

    s = _bdot_nt(qbd_ref[...], slab(2)) * SCALE
    _online_update(s, key_mask(p * (n_pg * page // NSA_BLOCK), n_pg * page), slab(3),
                   m_ref, l_ref, acc_ref)

    @pl.when(p == pl.num_programs(1) - 1)
    def _():
        _write_diag(o_ref, l_ref, acc_ref, NSA_KV_HEADS, NSA_GROUP, tq)


def _nsa_sel_sample(qr, cache, layer, page_table, rows_new, sel, pos0):
    n, w = qr.shape
    b, n_pages = page_table.shape
    tq = n // b
    rows_pp = cache.shape[2]
    page = rows_pp // NSA_ROWS
    kvw = 2 * NSA_KV_HEADS * HEAD_DIM
    rows = NSA_HEADS * tq
    n_pg = math.gcd(SEL_PAGES, n_pages)
    assert pos0 == n_pages * page and pos0 % NSA_BLOCK == 0 and tq <= NSA_BLOCK
    page_spec = lambda k: pl.BlockSpec(
        (1, 1, rows_pp, LANES), lambda bi, p, pt: (layer, pt[bi, p * n_pg + k], 0, 0))
    grid_spec = pltpu.PrefetchScalarGridSpec(
        num_scalar_prefetch=1,
        grid=(b, n_pages // n_pg),
        in_specs=[pl.BlockSpec((tq, w), lambda bi, p, pt: (bi, 0))]
        + [page_spec(k) for k in range(n_pg)]
        + [pl.BlockSpec((tq, kvw), lambda bi, p, pt: (bi, 1)),
           pl.BlockSpec((1,) + sel.shape[1:], lambda bi, p, pt: (bi, 0, 0))],
        out_specs=pl.BlockSpec((tq, w), lambda bi, p, pt: (bi, 0)),
        scratch_shapes=[pltpu.VMEM((rows, kvw // 2), BF16), pltpu.VMEM((rows, 1), F32),
                        pltpu.VMEM((rows, 1), F32), pltpu.VMEM((rows, kvw // 2), F32)],
    )
    return pl.pallas_call(
        functools.partial(_nsa_sel_sample_kernel, n_pg=n_pg, pos0=pos0),
        grid_spec=grid_spec,
        out_shape=jax.ShapeDtypeStruct((n, w), F32),
        compiler_params=_params("parallel", "arbitrary"),
        name="nsa_sel_sample",
    )(page_table, qr, *([cache] * n_pg), rows_new, sel)


def _nsa_win_sample_kernel(qr_ref, win_ref, new_ref, oc_ref, os_ref, gate_ref, gbias_ref,
                           o_ref, nwin_ref, kv_ref):
    tq = qr_ref.shape[0]
    lw = win_ref.shape[1]
    nk = kv_ref.shape[0]
    kw = NSA_KV_HEADS * HEAD_DIM
    gw = NSA_GROUP * HEAD_DIM
    kv_ref[0:lw, :] = win_ref[0]
    kv_ref[lw:nk, :] = _pad_rows(new_ref[...], nk - lw)
    nwin_ref[0, 0:lw - tq, :] = win_ref[0, tq:lw, :]
    nwin_ref[0, lw - tq:lw, :] = new_ref[...]

    idx = lax.broadcasted_iota(jnp.int32, (1, nk), 1)
    row_t = _token_of_row(NSA_GROUP, tq)
    dist = lw + row_t - idx
    mask = (dist >= 0) & (dist < NSA_WINDOW) & (idx < lw + tq)
    gates = _sigmoid(gate_ref[...] + gbias_ref[...])
    for g in range(NSA_KV_HEADS):
        qr = _stack_heads(qr_ref[:, g * gw:(g + 1) * gw], NSA_GROUP)
        s = _bdot_nt(qr, kv_ref[:, g * HEAD_DIM:(g + 1) * HEAD_DIM]) * SCALE
        p_w, _, _ = _softmax_rows(s, mask)
        o_w = _bdot(p_w, kv_ref[:, kw + g * HEAD_DIM:kw + (g + 1) * HEAD_DIM])
        for r in range(NSA_GROUP):
            cs = slice(g * gw + r * HEAD_DIM, g * gw + (r + 1) * HEAD_DIM)
            gc = g * LANES + 3 * r
            o_ref[:, cs] = (gates[:, gc:gc + 1] * oc_ref[:, cs] + gates[:, gc + 1:gc + 2] * os_ref[:, cs]
                            + gates[:, gc + 2:gc + 3] * o_w[r * tq:(r + 1) * tq])


def _nsa_win_sample(qr, win, rows_win, o_c, o_s, gates, gbias, batch):
    n, w = qr.shape
    tq = n // batch
    lw, ww = win.shape[1], win.shape[2]
    nk = lw + LANES
    assert lw >= NSA_WINDOW and tq % 8 == 0 and tq <= LANES
    row = lambda a: pl.BlockSpec((tq, a.shape[1]), lambda b: (b, 0))
    return pl.pallas_call(
        _nsa_win_sample_kernel,
        grid=(batch,),
        in_specs=[row(qr), pl.BlockSpec((1, lw, ww), lambda b: (b, 0, 0)), row(rows_win),
                  row(o_c), row(o_s), row(gates), pl.BlockSpec(gbias.shape, lambda b: (0, 0))],
        out_specs=[row(qr), pl.BlockSpec((1, lw, ww), lambda b: (b, 0, 0))],
        out_shape=[jax.ShapeDtypeStruct((n, w), F32), jax.ShapeDtypeStruct(win.shape, F32)],
        scratch_shapes=[pltpu.VMEM((nk, ww), F32)],
        compiler_params=_params("parallel"),
        name="nsa_win_sample",
    )(qr, win, rows_win, o_c, o_s, gates, gbias)


DIL_CHUNK = 256
DIL_HEAD_SHIFT = 4


def _dil_sample_kernel(q_ref, st_ref, new_ref, o_ref, m_ref, l_ref, acc_ref, *, lw):
    c = pl.program_id(1)
    n_chunks = pl.num_programs(1) - 1
    tq = q_ref.shape[0]
    rows = DIL_HEADS * tq
    n_g = len(DIL_PAIRS)
    q_all = _stack_heads(q_ref[...], DIL_HEADS).astype(BF16)
    row_t = _token_of_row(DIL_HEADS, tq)
    row_h = jnp.concatenate([jnp.full((tq, 1), h, jnp.int32) for h in range(DIL_HEADS)], axis=0)

    @pl.when(c == 0)
    def _():
        m_ref[...] = jnp.full_like(m_ref, NEG)
        l_ref[...] = jnp.zeros_like(l_ref)
        acc_ref[...] = jnp.zeros_like(acc_ref)

    def process(k_rows, v_rows, j0, first_chunk):
        n_l = k_rows.shape[0] * DIL_HEADS
        xk = k_rows[...].reshape(n_l, HEAD_DIM)
        s = _bdot_nt(q_all, xk) * SCALE
        lane = lax.broadcasted_iota(jnp.int32, (1, n_l), 1)
        head_ok = (lane & (DIL_HEADS - 1)) == row_h
        off = lw + row_t - (j0 + (lane >> DIL_HEAD_SHIFT))

        def group(gi, window, dil):
            mask = head_ok & (off >= 0) & (off <= window) & ((off & (dil - 1)) == 0)
            sm = jnp.where(mask, s, NEG)
            m_old = m_ref[gi]
            m_new = jnp.maximum(m_old, jnp.max(sm, axis=-1, keepdims=True))
            pr = jnp.where(mask, jnp.exp(sm - m_new), 0.0)
            alpha = jnp.exp(m_old - m_new)
            l_ref[gi] = alpha * l_ref[gi] + jnp.sum(pr, axis=-1, keepdims=True)
            acc_ref[gi] = alpha * acc_ref[gi] + _bdot(pr, v_rows[...].reshape(n_l, HEAD_DIM))
            m_ref[gi] = m_new

        for gi, (window, dil) in enumerate(DIL_PAIRS):
            pl.when(c >= first_chunk(window))(functools.partial(group, gi, window, dil))

    @pl.when(c < n_chunks)
    def _():
        n_pos = st_ref.shape[2]
        process(st_ref.at[0, 0, :, 0], st_ref.at[0, 0, :, 1], c * n_pos,
                lambda window: max(lw - window, 0) // n_pos)

    @pl.when(c == n_chunks)
    def _():
        process(new_ref.at[0, :, 0], new_ref.at[0, :, 1], lw, lambda window: 0)
        lses = [m_ref[gi] + jnp.log(l_ref[gi]) for gi in range(n_g)]
        mx = jnp.maximum(jnp.maximum(lses[0], lses[1]), lses[2])
        ws = [jnp.exp(l - mx) for l in lses]
        tot = ws[0] + ws[1] + ws[2]
        o = sum((ws[gi] / tot) * (acc_ref[gi] / l_ref[gi]) for gi in range(n_g))
        for h in range(DIL_HEADS):
            o_ref[:, h * HEAD_DIM:(h + 1) * HEAD_DIM] = o[h * tq:(h + 1) * tq]


def _dil_sample(q, state, layer, rows_new, batch):
    n, w = q.shape
    tq = n // batch
    lw = state.shape[2]
    assert lw >= DIL_PAIRS[-1][0] and lw % DIL_CHUNK == 0
    n_chunks = lw // DIL_CHUNK
    rows = DIL_HEADS * tq
    n_g = len(DIL_PAIRS)
    return pl.pallas_call(
        functools.partial(_dil_sample_kernel, lw=lw),
        grid=(batch, n_chunks + 1),
        in_specs=[pl.BlockSpec((tq, w), lambda b, c: (b, 0)),
                  pl.BlockSpec((1, 1, DIL_CHUNK) + state.shape[3:],
                               lambda b, c: (layer, b, jnp.minimum(c, n_chunks - 1), 0, 0, 0)),
                  pl.BlockSpec((1,) + rows_new.shape[1:], lambda b, c: (b, 0, 0, 0, 0))],
        out_specs=pl.BlockSpec((tq, w), lambda b, c: (b, 0)),
        out_shape=jax.ShapeDtypeStruct((n, w), F32),
        scratch_shapes=[pltpu.VMEM((n_g, rows, 1), F32), pltpu.VMEM((n_g, rows, 1), F32),
                        pltpu.VMEM((n_g, rows, HEAD_DIM), F32)],
        compiler_params=_params("parallel", "arbitrary"),
        name="dil_sample",
    )(q, state, rows_new)


def _state_shift_kernel(st_ref, new_ref, out_ref):
    lw, tq = st_ref.shape[0], new_ref.shape[0]
    out_ref[0:lw - tq] = st_ref[tq:lw]
    out_ref[lw - tq:lw] = new_ref[...]


def _state_shift(state, rows_new):
    n_l, b, lw, n_s = state.shape[:4]
    tq = rows_new.shape[2]
    tile = state.shape[4:]
    spec = lambda rows: pl.BlockSpec((None, None, rows, None) + tile,
                                     lambda li, bi, s: (li, bi, 0, s, 0, 0))
    return pl.pallas_call(
        _state_shift_kernel,
        grid=(n_l, b, n_s),
        in_specs=[spec(lw), spec(tq)],
        out_specs=spec(lw),
        out_shape=jax.ShapeDtypeStruct(state.shape, F32),
        compiler_params=_params("parallel", "parallel", "arbitrary"),
        name="state_shift",
    )(state, rows_new)


def _rope_tables(pos):
    half = HEAD_DIM // 2
    inv = ROPE_THETA ** (-jnp.arange(half, dtype=F32) / half)
    ang = pos.astype(F32)[:, None] * inv[None, :]
    cos, sin = jnp.cos(ang), jnp.sin(ang)
    return jnp.concatenate([cos, cos], axis=-1), jnp.concatenate([-sin, sin], axis=-1)


def _pad_lanes(a, width=LANES):
    return jnp.pad(a, ((0, 0),) * (a.ndim - 1) + ((0, width - a.shape[-1]),))


def _chunks(first, count, op="copy", gi=0):
    return [(first + c, op, gi) for c in range(count)]


NSA_W = NSA_HEADS * HEAD_DIM
NSA_KV_W = NSA_KV_HEADS * HEAD_DIM
FOX_W = FOX_HEADS * HEAD_DIM
DIL_W = DIL_HEADS * HEAD_DIM
GATES_PER_GROUP = 3 * NSA_GROUP


def _even_weights(w_in, w_out, gate_b, forget_b, nsa_g, fox_g):
    cuts = [0]
    for wd in ((NSA_W,) + (NSA_KV_W,) * 6
               + (3 * NSA_HEADS, NSA_W, FOX_W, FOX_W, FOX_W, FOX_HEADS, FOX_W)):
        cuts.append(cuts[-1] + wd)
    seg = lambda a, b: w_in[:, cuts[a]:cuts[b]]
    gate_w, f_w = seg(7, 8), seg(12, 13)
    small = [_pad_lanes(f_w)] + [_pad_lanes(gate_w[:, g * GATES_PER_GROUP:(g + 1) * GATES_PER_GROUP])
                                 for g in range(NSA_KV_HEADS)]
    gbias = jnp.concatenate(
        [_pad_lanes(gate_b[None, g * GATES_PER_GROUP:(g + 1) * GATES_PER_GROUP])
         for g in range(NSA_KV_HEADS)], axis=1)
    aux = jnp.stack([nsa_g[0], nsa_g[2], nsa_g[3], fox_g[0], fox_g[1], _pad_lanes(forget_b),
                     jnp.zeros((LANES,), F32), jnp.zeros((LANES,), F32)])
    return dict(
        w_a=seg(0, 7).astype(BF16),
        w_b=seg(8, 10).astype(BF16),
        w_c=seg(10, 12).astype(BF16),
        w_d=jnp.concatenate([seg(13, 14)] + small, axis=1).astype(BF16),
        w_out_a=w_out[:NSA_W].astype(BF16), w_out_b=w_out[NSA_W:].astype(BF16),
        aux=aux, gbias=gbias, g_kc=nsa_g[1][None, :])


def _even_projections(x, g_norm, wts, pos):
    h = _rmsnorm(x, g_norm)
    cos2, sin2 = _rope_tables(pos)
    nq, nkv = NSA_HEADS, NSA_KV_HEADS
    nsa_plan = _chunks(nq, 2 * nkv) + _chunks(nq + 2 * nkv, nkv, "rms_rope", 1) + _chunks(nq + 3 * nkv, nkv)
    qn, qr, rows_nsa, rows_win, leaf_nsa = _proj(
        h, wts["w_a"], wts["aux"], cos2, sin2,
        [[_chunks(0, nq, "rms", 0)], [_chunks(0, nq, "rms_rope", 0)], [nsa_plan],
         [_chunks(nq + 4 * nkv, nkv, "rms_rope", 2) + _chunks(nq + 5 * nkv, nkv)], [nsa_plan]],
        wts["w_a"].shape[1], head_major=[False, False, False, False, True])
    z_a, qb = _proj(h, wts["w_b"], wts["aux"], cos2, sin2,
                    [[_chunks(0, nq)], [_chunks(nq, FOX_HEADS, "rms", 3)]], wts["w_b"].shape[1])
    fox_plan = _chunks(0, FOX_HEADS, "rms", 4) + _chunks(FOX_HEADS, FOX_HEADS)
    rows_fox, leaf_fox = _proj(h, wts["w_c"], wts["aux"], cos2, sin2, [[fox_plan], [fox_plan]],
                               wts["w_c"].shape[1], head_major=[False, True])
    z_b, logf, gates = _proj(
        h, wts["w_d"], wts["aux"], cos2, sin2,
        [[_chunks(0, FOX_HEADS)], [_chunks(FOX_HEADS, 1, "logsig", 5)], [_chunks(FOX_HEADS + 1, nkv)]],
        wts["w_d"].shape[1])
    return qn, qr, rows_nsa, rows_win, z_a, qb, rows_fox, z_b, logf, gates, leaf_nsa, leaf_fox


def _even_prompt(x, batch, g_norm, wts, cmp_pos):
    n = x.shape[0]
    t = n // batch
    pos = jnp.tile(jnp.arange(t, dtype=jnp.int32), batch)
    (qn, qr, rows_nsa, rows_win, z_a, qb, rows_fox, z_b, logf, gates, leaf_nsa,
     leaf_fox) = _even_projections(
        x, g_norm, wts, pos)
    c_all = _cumsum_rows(logf, batch)
    ck_t = jnp.transpose(c_all[:, :FOX_HEADS].reshape(batch, t, FOX_HEADS), (0, 2, 1))
    o_b = _fox_prompt(qb, rows_fox, c_all, ck_t, batch)
    o_a = _nsa_prompt(qn, qr, rows_nsa, rows_win, gates, wts["gbias"], cmp_pos, wts["g_kc"], batch)
    y = _outproj(x, [o_a, o_b], [z_a, z_b], [wts["w_out_a"], wts["w_out_b"]])
    lw = min(NSA_WINDOW, t)
    new_win = rows_win.reshape(batch, t, 2, NSA_KV_HEADS, HEAD_DIM)[:, t - lw:]
    return y, (leaf_nsa.reshape(batch, t, 4, NSA_KV_HEADS, HEAD_DIM),
               leaf_fox.reshape(batch, t, 2, FOX_HEADS, HEAD_DIM),
               logf[:, :FOX_HEADS].reshape(batch, t, FOX_HEADS), new_win)


def _even_sample(x, batch, pos0, g_norm, wts, cmp_pos, layer, cache_nsa, cache_fox, cache_lf,
                 win_state, page_table):
    n = x.shape[0]
    tq = n // batch
    pos = jnp.tile(pos0 + jnp.arange(tq, dtype=jnp.int32), batch)
    (qn, qr, rows_nsa, rows_win, z_a, qb, rows_fox, z_b, logf, gates, leaf_nsa,
     leaf_fox) = _even_projections(
        x, g_norm, wts, pos)
    means = _page_means(cache_nsa, layer, page_table, cmp_pos)
    n_blocks = means.shape[1]
    nbp = -(-(n_blocks + 1) // LANES) * LANES
    means = jnp.pad(means, ((0, 0), (0, nbp - n_blocks), (0, 0)))
    o_c, sel = _nsa_cmp_sample(qn, means, wts["g_kc"], batch, pos0)
    o_s = _nsa_sel_sample(qr, cache_nsa, layer, page_table, rows_nsa, sel, pos0)
    lw = win_state.shape[1]
    win = win_state.reshape(batch, lw, 2 * NSA_KV_W)
    o_a, new_win = _nsa_win_sample(qr, win, rows_win, o_c, o_s, gates, wts["gbias"], batch)
    lf_new_t = _pad_lanes(
        jnp.transpose(logf[:, :FOX_HEADS].reshape(batch, tq, FOX_HEADS), (0, 2, 1)))
    o_b = _fox_sample(qb, cache_fox, cache_lf, layer, page_table, rows_fox, lf_new_t)
    y = _outproj(x, [o_a, o_b], [z_a, z_b], [wts["w_out_a"], wts["w_out_b"]])
    return y, (leaf_nsa.reshape(batch, tq, 4, NSA_KV_HEADS, HEAD_DIM),
               leaf_fox.reshape(batch, tq, 2, FOX_HEADS, HEAD_DIM),
               logf[:, :FOX_HEADS].reshape(batch, tq, FOX_HEADS),
               new_win.reshape(batch, lw, 2, NSA_KV_HEADS, HEAD_DIM))


def _odd_weights(w_in, w_out, qk_g):
    aux = jnp.concatenate([qk_g, jnp.zeros((6, HEAD_DIM), F32)], axis=0)
    return dict(w_q=w_in[:, :DIL_W].astype(BF16), w_kv=w_in[:, DIL_W:3 * DIL_W].astype(BF16),
                w_z=w_in[:, 3 * DIL_W:].astype(BF16), w_out=w_out.astype(BF16), aux=aux)


def _odd_projections(x, g_norm, wts, pos, head_major_rows=False):
    h = _rmsnorm(x, g_norm)
    cos2, sin2 = _rope_tables(pos)
    q, = _proj(h, wts["w_q"], wts["aux"], cos2, sin2, [[_chunks(0, DIL_HEADS, "rms_rope", 0)]], DIL_W)
    rows, = _proj(h, wts["w_kv"], wts["aux"], cos2, sin2,
                  [[_chunks(0, DIL_HEADS, "rms_rope", 1), _chunks(0, DIL_HEADS)]], DIL_W,
                  head_major=[head_major_rows])
    z, = _proj(h, wts["w_z"], wts["aux"], cos2, sin2, [[_chunks(0, DIL_HEADS)]], DIL_W)
    return q, rows, z


def _odd_prompt(x, batch, g_norm, wts):
    n = x.shape[0]
    t = n // batch
    pos = jnp.tile(jnp.arange(t, dtype=jnp.int32), batch)
    q, rows, z = _odd_projections(x, g_norm, wts, pos)
    o = _dil_prompt(q, rows, batch)
    y = _outproj(x, [o], [z], [wts["w_out"]])
    lw = min(DIL_PAIRS[-1][0], t)
    return y, rows.reshape(batch, t, 2, DIL_HEADS, HEAD_DIM)[:, t - lw:]


STATE_TILE = 8


def _odd_sample(x, batch, pos0, g_norm, wts, state_all, layer):
    n = x.shape[0]
    tq = n // batch
    pos = jnp.tile(pos0 + jnp.arange(tq, dtype=jnp.int32), batch)
    q, rows, z = _odd_projections(x, g_norm, wts, pos, head_major_rows=True)
    o = _dil_sample(q, state_all, layer, rows.reshape(batch, tq, 2, DIL_HEADS, HEAD_DIM), batch)
    y = _outproj(x, [o], [z], [wts["w_out"]])
    return y, rows


def kernel(x_prompt, x_sample, cache_nsa_kv, cache_fox_kv, cache_fox_logf, state_nsa_win_kv, state_dil_kv, page_table, norm_even, w_in_even, w_out_even, nsa_gate_bias, fox_forget_bias, nsa_cmp_pos, nsa_qk_gain, fox_qk_gain, norm_odd, w_in_odd, w_out_odd, dil_qk_gain):
    bp, t, d = x_prompt.shape
    bs, tq, _ = x_sample.shape
    n_layers, n_pool, page = cache_nsa_kv.shape[:3]
    pos0 = page_table.shape[1] * page
    depth = norm_even.shape[0] + norm_odd.shape[0]
    cache_nsa = cache_nsa_kv.reshape(n_layers, n_pool, page * NSA_ROWS, HEAD_DIM)
    cache_lf = cache_fox_logf.reshape(n_layers, n_pool, page * FOX_HEADS // LANES, LANES)
    yp = x_prompt.reshape(bp * t, d)
    ys = x_sample.reshape(bs * tq, d)
    outs = [[] for _ in range(9)]
    dil_rows = []
    for layer in range(depth):
        i = layer // 2
        if layer % 2 == 0:
            wts = _even_weights(w_in_even[i], w_out_even[i], nsa_gate_bias[i], fox_forget_bias[i],
                                nsa_qk_gain[i], fox_qk_gain[i])
            yp, new_p = _even_prompt(yp, bp, norm_even[i], wts, nsa_cmp_pos[i])
            ys, new_s = _even_sample(ys, bs, pos0, norm_even[i], wts, nsa_cmp_pos[i], i, cache_nsa,
                                     cache_fox_kv, cache_lf, state_nsa_win_kv[i], page_table)
            for k in range(4):
                outs[2 * k].append(new_p[k])
                outs[2 * k + 1].append(new_s[k])
        else:
            wts = _odd_weights(w_in_odd[i], w_out_odd[i], dil_qk_gain[i])
            yp, buf_p = _odd_prompt(yp, bp, norm_odd[i], wts)
            ys, rows_s = _odd_sample(ys, bs, pos0, norm_odd[i], wts, state_dil_kv, i)
            outs[8].append(buf_p)
            dil_rows.append(rows_s)
    n_odd, _, lw = state_dil_kv.shape[:3]
    n_s = 2 * DIL_HEADS // STATE_TILE
    dil_s = _state_shift(state_dil_kv.reshape(n_odd, bs, lw, n_s, STATE_TILE, HEAD_DIM),
                         jnp.stack(dil_rows).reshape(n_odd, bs, tq, n_s, STATE_TILE, HEAD_DIM))
    return ((yp.reshape(bp, t, d), ys.reshape(bs, tq, d)) + tuple(jnp.stack(o) for o in outs)
            + (dil_s.reshape(state_dil_kv.shape),))
```
